```python
import math
import jax, jax.numpy as jnp
from jax import lax
import numpy as np

D_MODEL = 1024
BATCH = 8
SEQ = 2048
DEPTH = 4

N_MIXERS = 2
N_HEADS = 16
HEAD_DIM = 64
NSA_KV_GROUPS = 4
NSA_Q_PER_GROUP = N_HEADS // NSA_KV_GROUPS
CMP_BLOCK = 32
CMP_STRIDE = 16
CMP_HIDDEN = 256
SEL_BLOCK = 64
SEL_TOPK = 16
SEL_Q_CHUNK = 64
WINDOW = 512
Q_BLOCK = 128
NUM_BUCKETS = 32
MAX_DISTANCE = 128
D_FF = 2816
RMS_EPS = 1e-6
NEG_INF = -1e30
FORCED_SCORE = 1e9
Q_DIM = N_HEADS * HEAD_DIM
KV_DIM = NSA_KV_GROUPS * HEAD_DIM
NSA_IN_DIM = Q_DIM + 6 * KV_DIM + 3 * N_HEADS
FOX_IN_DIM = 3 * Q_DIM + N_HEADS
N_NSA_LAYERS = (DEPTH + 1) // 2
N_FOX_LAYERS = DEPTH // 2

kernel_name = "nsa_fox_macaron_sandwich_hybrid"


def rms_norm(x, g):
    xf = x.astype(jnp.float32)
    y = xf * lax.rsqrt(jnp.mean(xf * xf, axis=-1, keepdims=True) + RMS_EPS)
    return (y * g.astype(jnp.float32)).astype(x.dtype)


def swiglu(x, w_gate, w_up, w_down):
    return (jax.nn.silu(x @ w_gate) * (x @ w_up)) @ w_down


def t5_bucket(rel):
    n = jnp.maximum(rel, 0)
    max_exact = NUM_BUCKETS // 2
    nf = jnp.maximum(n, 1).astype(jnp.float32)
    large = max_exact + (jnp.log(nf / max_exact) / math.log(MAX_DISTANCE / max_exact)
                         * (NUM_BUCKETS - max_exact)).astype(jnp.int32)
    large = jnp.minimum(large, NUM_BUCKETS - 1)
    return jnp.where(n < max_exact, n, large)


def masked_softmax(logits, mask):
    logits = jnp.where(mask, logits.astype(jnp.float32), NEG_INF)
    p = jax.nn.softmax(logits, axis=-1)
    return jnp.where(mask, p, 0.0)


def nsa_mixer(h, w_in, cmp_pe, cmp_w1, cmp_b1, cmp_w2, w_out, rel_bias):
    B, S, _ = h.shape
    G, R, dh = NSA_KV_GROUPS, NSA_Q_PER_GROUP, HEAD_DIM
    scale = dh ** -0.5
    splits = np.cumsum([Q_DIM] + [KV_DIM] * 6).tolist()
    q, kc, vc, ks, vs, kw, vw, gates = jnp.split(h @ w_in, splits, axis=-1)
    q = q.reshape(B, S, G, R, dh).transpose(0, 2, 3, 1, 4)
    to_kv = lambda a: a.reshape(B, S, G, dh).transpose(0, 2, 1, 3)
    kc, vc, ks, vs, kw, vw = map(to_kv, (kc, vc, ks, vs, kw, vw))
    t_np = np.arange(S)

    nc = (S - CMP_BLOCK) // CMP_STRIDE + 1
    blk_idx = np.arange(nc)[:, None] * CMP_STRIDE + np.arange(CMP_BLOCK)[None, :]

    def compress(a, pe, w1, b1, w2):
        blocks = a[:, :, blk_idx] + pe
        flat = blocks.reshape(B, G, nc, CMP_BLOCK * dh)
        return jax.nn.gelu(flat @ w1 + b1) @ w2

    k_cmp = compress(kc, cmp_pe[0], cmp_w1[0], cmp_b1[0], cmp_w2[0])
    v_cmp = compress(vc, cmp_pe[1], cmp_w1[1], cmp_b1[1], cmp_w2[1])
    blk_end = np.arange(nc) * CMP_STRIDE + CMP_BLOCK - 1
    rel_c = t_np[:, None] - blk_end[None, :]
    mask_c = rel_c >= 0
    bias_c = jnp.transpose(rel_bias[t5_bucket(jnp.asarray(rel_c))], (2, 0, 1)).reshape(G, R, S, nc)
    logits_c = jnp.einsum('bgrsd,bgcd->bgrsc', q, k_cmp) * scale + bias_c
    p_c = masked_softmax(logits_c, mask_c)
    o_cmp = jnp.einsum('bgrsc,bgcd->bgrsd', p_c.astype(v_cmp.dtype), v_cmp)

    n_sel = S // SEL_BLOCK
    c_start = np.arange(nc)[:, None] * CMP_STRIDE
    j_np = np.arange(n_sel)[None, :]
    overlap = ((c_start < (j_np + 1) * SEL_BLOCK) & (c_start + CMP_BLOCK > j_np * SEL_BLOCK)).astype(np.float32)
    imp = jnp.einsum('bgrsc,cj->bgsj', p_c, jnp.asarray(overlap))
    cur = (t_np // SEL_BLOCK)[:, None]
    forced = (j_np == 0) | (j_np == cur) | (j_np == cur - 1)
    valid_blk = j_np <= cur
    imp = jnp.where(forced, FORCED_SCORE, jnp.where(valid_blk, imp, NEG_INF))
    k_eff = min(SEL_TOPK, n_sel)
    _, sel_idx = lax.top_k(imp, k_eff)

    k_blocks = ks.reshape(B, G, n_sel, SEL_BLOCK, dh)
    v_blocks = vs.reshape(B, G, n_sel, SEL_BLOCK, dh)
    tbl = jnp.transpose(rel_bias.reshape(NUM_BUCKETS, G, R), (1, 0, 2))
    b_ix = jnp.arange(B)[:, None, None, None]
    g_ix = jnp.arange(G)[None, :, None, None]
    C = SEL_Q_CHUNK
    n_ch = S // C

    def sel_chunk(args):
        q_c, idx_c, t_c = args
        kg = k_blocks[b_ix, g_ix, idx_c]
        vg = v_blocks[b_ix, g_ix, idx_c]
        pos = idx_c[..., None] * SEL_BLOCK + jnp.arange(SEL_BLOCK)
        rel = t_c[None, None, :, None, None] - pos
        bias = jnp.moveaxis(tbl[g_ix[..., None], t5_bucket(rel)], -1, 2)
        logits = jnp.einsum('bgrcd,bgckld->bgrckl', q_c, kg) * scale + bias
        mask = (rel >= 0)[:, :, None].reshape(B, G, 1, C, -1)
        p = masked_softmax(logits.reshape(B, G, R, C, -1), mask).reshape(logits.shape)
        return jnp.einsum('bgrckl,bgckld->bgrcd', p.astype(vg.dtype), vg)

    q_x = jnp.moveaxis(q.reshape(B, G, R, n_ch, C, dh), 3, 0)
    idx_x = jnp.moveaxis(sel_idx.reshape(B, G, n_ch, C, k_eff), 2, 0)
    t_x = jnp.arange(S).reshape(n_ch, C)
    o_sel = lax.map(sel_chunk, (q_x, idx_x, t_x))
    o_sel = jnp.moveaxis(o_sel, 0, 3).reshape(B, G, R, S, dh)

    n_qb = S // Q_BLOCK
    band = WINDOW + Q_BLOCK
    kw_p = jnp.pad(kw, ((0, 0), (0, 0), (WINDOW, 0), (0, 0)))
    vw_p = jnp.pad(vw, ((0, 0), (0, 0), (WINDOW, 0), (0, 0)))
    rel_w = np.arange(Q_BLOCK)[:, None] + WINDOW - np.arange(band)[None, :]
    band_mask = (rel_w >= 0) & (rel_w < WINDOW)
    bias_w = jnp.transpose(rel_bias[t5_bucket(jnp.asarray(rel_w))], (2, 0, 1)).reshape(G, R, Q_BLOCK, band)

    def win_block(args):
        q_b, start = args
        k_b = lax.dynamic_slice_in_dim(kw_p, start, band, axis=2)
        v_b = lax.dynamic_slice_in_dim(vw_p, start, band, axis=2)
        key_pos = start - WINDOW + jnp.arange(band)
        mask = band_mask & (key_pos >= 0)[None, :]
        logits = jnp.einsum('bgrqd,bgkd->bgrqk', q_b, k_b) * scale + bias_w
        p = masked_softmax(logits, mask)
        return jnp.einsum('bgrqk,bgkd->bgrqd', p.astype(v_b.dtype), v_b)

    qw_x = jnp.moveaxis(q.reshape(B, G, R, n_qb, Q_BLOCK, dh), 3, 0)
    o_win = lax.map(win_block, (qw_x, jnp.arange(n_qb) * Q_BLOCK))
    o_win = jnp.moveaxis(o_win, 0, 3).reshape(B, G, R, S, dh)

    g = jax.nn.sigmoid(gates.reshape(B, S, 3, G, R)).transpose(2, 0, 3, 4, 1)[..., None]
    o = g[0] * o_cmp + g[1] * o_sel + g[2] * o_win
    o = o.transpose(0, 3, 1, 2, 4).reshape(B, S, Q_DIM)
    return o @ w_out


def fox_mixer(h, w_in, b_f, w_out):
    B, S, _ = h.shape
    H, dh = N_HEADS, HEAD_DIM
    scale = dh ** -0.5
    q, k, v, f_logit = jnp.split(h @ w_in, [Q_DIM, 2 * Q_DIM, 3 * Q_DIM], axis=-1)
    heads = lambda a: a.reshape(B, S, H, dh).transpose(0, 2, 1, 3)
    q, k, v = heads(q), heads(k), heads(v)
    log_f = jax.nn.log_sigmoid((f_logit + b_f).astype(jnp.float32))
    cum = jnp.cumsum(log_f, axis=1).transpose(0, 2, 1)
    n_qb = S // Q_BLOCK
    key_pos = jnp.arange(S)

    def blk(args):
        q_b, cum_b, start = args
        decay = cum_b[..., None] - cum[:, :, None, :]
        logits = jnp.einsum('bhqd,bhkd->bhqk', q_b, k).astype(jnp.float32) * scale + decay
        qpos = start + jnp.arange(Q_BLOCK)
        mask = key_pos[None, :] <= qpos[:, None]
        p = masked_softmax(logits, mask)
        return jnp.einsum('bhqk,bhkd->bhqd', p.astype(v.dtype), v)

    q_x = jnp.moveaxis(q.reshape(B, H, n_qb, Q_BLOCK, dh), 2, 0)
    c_x = jnp.moveaxis(cum.reshape(B, H, n_qb, Q_BLOCK), 2, 0)
    o = lax.map(blk, (q_x, c_x, jnp.arange(n_qb) * Q_BLOCK))
    o = o.transpose(1, 0, 3, 2, 4).reshape(B, S, Q_DIM)
    return o @ w_out


def setup_inputs(seed: int = 0) -> dict:
    key = jax.random.key(seed)
    ks = jax.random.split(key, 16)
    nrm = lambda k, shape, fan_in: jax.random.normal(k, shape, jnp.float32) * (fan_in ** -0.5)
    dh = HEAD_DIM
    return {
        "x": jax.random.normal(ks[0], (BATCH, SEQ, D_MODEL), jnp.float32),
        "norm_g": 1.0 + 0.05 * jax.random.normal(ks[1], (DEPTH, 6, D_MODEL), jnp.float32),
        "ffn_w_gate": nrm(ks[2], (DEPTH, 2, D_MODEL, D_FF), D_MODEL),
        "ffn_w_up": nrm(ks[3], (DEPTH, 2, D_MODEL, D_FF), D_MODEL),
        "ffn_w_down": nrm(ks[4], (DEPTH, 2, D_FF, D_MODEL), D_FF),
        "rel_bias": 0.5 * jax.random.normal(ks[5], (NUM_BUCKETS, N_HEADS), jnp.float32),
        "nsa_w_in": nrm(ks[6], (N_NSA_LAYERS, D_MODEL, NSA_IN_DIM), D_MODEL),
        "nsa_cmp_pe": 0.1 * jax.random.normal(ks[7], (N_NSA_LAYERS, 2, CMP_BLOCK, dh), jnp.float32),
        "nsa_cmp_w1": nrm(ks[8], (N_NSA_LAYERS, 2, CMP_BLOCK * dh, CMP_HIDDEN), CMP_BLOCK * dh),
        "nsa_cmp_b1": 0.01 * jax.random.normal(ks[9], (N_NSA_LAYERS, 2, CMP_HIDDEN), jnp.float32),
        "nsa_cmp_w2": nrm(ks[10], (N_NSA_LAYERS, 2, CMP_HIDDEN, dh), CMP_HIDDEN),
        "nsa_w_out": nrm(ks[11], (N_NSA_LAYERS, Q_DIM, D_MODEL), Q_DIM),
        "fox_w_in": nrm(ks[12], (N_FOX_LAYERS, D_MODEL, FOX_IN_DIM), D_MODEL),
        "fox_b_f": 2.0 + 0.1 * jax.random.normal(ks[13], (N_FOX_LAYERS, N_HEADS), jnp.float32),
        "fox_w_out": nrm(ks[14], (N_FOX_LAYERS, Q_DIM, D_MODEL), Q_DIM),
    }


def reference(x, norm_g, ffn_w_gate, ffn_w_up, ffn_w_down, rel_bias, nsa_w_in, nsa_cmp_pe,
              nsa_cmp_w1, nsa_cmp_b1, nsa_cmp_w2, nsa_w_out, fox_w_in, fox_b_f, fox_w_out):
    h = x
    for i in range(DEPTH):
        g = norm_g[i]
        j = i // N_MIXERS
        y = swiglu(rms_norm(h, g[0]), ffn_w_gate[i, 0], ffn_w_up[i, 0], ffn_w_down[i, 0])
        h = h + 0.5 * rms_norm(y, g[1])
        u = rms_norm(h, g[2])
        if i % N_MIXERS == 0:
            y = nsa_mixer(u, nsa_w_in[j], nsa_cmp_pe[j], nsa_cmp_w1[j], nsa_cmp_b1[j],
                          nsa_cmp_w2[j], nsa_w_out[j], rel_bias)
        else:
            y = fox_mixer(u, fox_w_in[j], fox_b_f[j], fox_w_out[j])
        h = h + rms_norm(y, g[3])
        y = swiglu(rms_norm(h, g[4]), ffn_w_gate[i, 1], ffn_w_up[i, 1], ffn_w_down[i, 1])
        h = h + 0.5 * rms_norm(y, g[5])
    return h
```

```python
import functools
import math

import numpy as np
import jax
import jax.numpy as jnp
from jax import lax
from jax.experimental import pallas as pl
from jax.experimental.pallas import tpu as pltpu

N_HEADS = 16
HEAD_DIM = 64
KV_GROUPS = 4
Q_PER_GROUP = N_HEADS // KV_GROUPS
CMP_BLOCK = 32
CMP_STRIDE = 16
SEL_BLOCK = 64
SEL_SHIFT = 6
SEL_TOPK = 16
WINDOW = 512
NUM_BUCKETS = 32
MAX_DISTANCE = 128
RMS_EPS = 1e-6
NEG_INF = -1e30
FORCED_SCORE = 1e9
Q_DIM = N_HEADS * HEAD_DIM
KV_DIM = KV_GROUPS * HEAD_DIM
N_GATES = 3 * N_HEADS

ATT_TILE = 128
WIN_TILES = WINDOW // ATT_TILE
FOX_TILE = 256
VMEM_LIMIT = 56 * 1024 * 1024

BF16 = jnp.bfloat16
F32 = jnp.float32


def _cparams(*sem):
    return pltpu.CompilerParams(dimension_semantics=sem, vmem_limit_bytes=VMEM_LIMIT)


def _rms(x, g):
    return x * lax.rsqrt(jnp.mean(x * x, axis=-1, keepdims=True) + RMS_EPS) * g


def _dot(a, b):
    return jnp.dot(a, b, preferred_element_type=F32)


def _dot_nt(a, b):
    return lax.dot_general(a, b, (((1,), (1,)), ((), ())), preferred_element_type=F32)


def _split3(x):
    hi = x.astype(BF16)
    r1 = x - hi.astype(F32)
    mid = r1.astype(BF16)
    lo = (r1 - mid.astype(F32)).astype(BF16)
    return hi, mid, lo


def _dot_exact_rhs(x, m_bf16):
    hi, mid, lo = _split3(x)
    return _dot(hi, m_bf16) + _dot(mid, m_bf16) + _dot(lo, m_bf16)


def _ffn_body(h_ref, gpre_ref, gpost_ref, wg_ref, wu_ref, wd_ref, o_ref, xn_ref, acc_ref):
    k = pl.program_id(1)

    @pl.when(k == 0)
    def _():
        xn_ref[...] = _rms(h_ref[...], gpre_ref[...]).astype(BF16)
        acc_ref[...] = jnp.zeros_like(acc_ref)

    xn = xn_ref[...]
    g = _dot(xn, wg_ref[...])
    u = _dot(xn, wu_ref[...])
    a = (g * jax.nn.sigmoid(g) * u).astype(BF16)
    acc_ref[...] += _dot(a, wd_ref[...])

    @pl.when(k == pl.num_programs(1) - 1)
    def _():
        o_ref[...] = h_ref[...] + 0.5 * _rms(acc_ref[...], gpost_ref[...])


def _ffn(h, g_pre, g_post, wg, wu, wd, layer, half, tm=512, tf=1408):
    n, d = h.shape
    f = wg.shape[-1]
    row = lambda i, k: (i, 0)
    vec = lambda i, k: (0, 0)
    return pl.pallas_call(
        _ffn_body,
        grid=(n // tm, f // tf),
        in_specs=[
            pl.BlockSpec((tm, d), row),
            pl.BlockSpec((1, d), vec),
            pl.BlockSpec((1, d), vec),
            pl.BlockSpec((None, None, d, tf), lambda i, k: (layer, half, 0, k)),
            pl.BlockSpec((None, None, d, tf), lambda i, k: (layer, half, 0, k)),
            pl.BlockSpec((None, None, tf, d), lambda i, k: (layer, half, k, 0)),
        ],
        out_specs=pl.BlockSpec((tm, d), row),
        out_shape=jax.ShapeDtypeStruct((n, d), F32),
        scratch_shapes=[pltpu.VMEM((tm, d), BF16), pltpu.VMEM((tm, d), F32)],
        compiler_params=_cparams("parallel", "arbitrary"),
        name="ffn",
    )(h, g_pre, g_post, wg, wu, wd)


def _outproj_body(o_ref, w_ref, h_ref, g_ref, out_ref):
    y = _dot(o_ref[...], w_ref[...])
    out_ref[...] = h_ref[...] + _rms(y, g_ref[...])


def _outproj(o, w, h, g, layer, tm=512):
    n, d = h.shape
    kdim = o.shape[-1]
    return pl.pallas_call(
        _outproj_body,
        grid=(n // tm,),
        in_specs=[
            pl.BlockSpec((tm, kdim), lambda i: (i, 0)),
            pl.BlockSpec((None, kdim, d), lambda i: (layer, 0, 0)),
            pl.BlockSpec((tm, d), lambda i: (i, 0)),
            pl.BlockSpec((1, d), lambda i: (0, 0)),
        ],
        out_specs=pl.BlockSpec((tm, d), lambda i: (i, 0)),
        out_shape=jax.ShapeDtypeStruct((n, d), F32),
        compiler_params=_cparams("parallel"),
        name="outproj",
    )(o, w, h, g)


def _nsa_proj_body(h_ref, g_ref, w_ref, q_ref, kv_ref, gate_ref):
    xn = _rms(h_ref[0], g_ref[...]).astype(BF16)
    res = _dot(xn, w_ref[...])
    scale = HEAD_DIM ** -0.5
    for hd in range(N_HEADS):
        q_ref[0, hd] = (res[:, hd * HEAD_DIM:(hd + 1) * HEAD_DIM] * scale).astype(BF16)
    for a in range(6 * KV_GROUPS):
        lo = Q_DIM + a * HEAD_DIM
        kv_ref[0, a] = res[:, lo:lo + HEAD_DIM].astype(BF16)
    gs = jax.nn.sigmoid(res[:, Q_DIM + 6 * KV_DIM:])
    width = 3 * Q_PER_GROUP
    for grp in range(KV_GROUPS):
        gate_ref[0, grp] = gs[:, grp * width:(grp + 1) * width]


def _nsa_proj(h3, g, w, layer, tm=512):
    b, s, d = h3.shape
    n_in = w.shape[-1]
    return pl.pallas_call(
        _nsa_proj_body,
        grid=(b, s // tm),
        in_specs=[
            pl.BlockSpec((1, tm, d), lambda i, j: (i, j, 0)),
            pl.BlockSpec((1, d), lambda i, j: (0, 0)),
            pl.BlockSpec((None, d, n_in), lambda i, j: (layer, 0, 0)),
        ],
        out_specs=[
            pl.BlockSpec((1, N_HEADS, tm, HEAD_DIM), lambda i, j: (i, 0, j, 0)),
            pl.BlockSpec((1, 6 * KV_GROUPS, tm, HEAD_DIM), lambda i, j: (i, 0, j, 0)),
            pl.BlockSpec((1, KV_GROUPS, tm, 3 * Q_PER_GROUP), lambda i, j: (i, 0, j, 0)),
        ],
        out_shape=[
            jax.ShapeDtypeStruct((b, N_HEADS, s, HEAD_DIM), BF16),
            jax.ShapeDtypeStruct((b, 6 * KV_GROUPS, s, HEAD_DIM), BF16),
            jax.ShapeDtypeStruct((b, KV_GROUPS, s, 3 * Q_PER_GROUP), F32),
        ],
        compiler_params=_cparams("parallel", "parallel"),
        name="nsa_proj",
    )(h3, g, w)


def _compress_body(x_ref, pe_ref, w1_ref, b1_ref, w2_ref, o_ref):
    n_chunk = x_ref.shape[3]
    half = CMP_STRIDE * HEAD_DIM
    x = x_ref[0, 0].reshape(KV_GROUPS * n_chunk, half)
    top = _dot(x, w1_ref[:half, :])
    bot = _dot(x, w1_ref[half:, :])
    bot_next = pltpu.roll(bot, KV_GROUPS * n_chunk - 1, 0)
    pe = jnp.broadcast_to(pe_ref[...].astype(BF16), (8, 2 * half))
    const = _dot(pe, w1_ref[...])[0:1] + b1_ref[...]
    hid = jax.nn.gelu(top + bot_next + const).astype(BF16)
    out = _dot(hid, w2_ref[...])
    row = lax.broadcasted_iota(jnp.int32, out.shape, 0) & (n_chunk - 1)
    out = jnp.where(row < n_chunk - 1, out, 0.0)
    o_ref[0, 0] = out.reshape(KV_GROUPS, n_chunk, HEAD_DIM).astype(BF16)


def _compress(kv, pe, w1, b1, w2, layer):
    b, _, s, dh = kv.shape
    n_chunk = s // CMP_STRIDE
    x = kv.reshape(b, 6, KV_GROUPS, n_chunk, CMP_STRIDE * dh)
    hidden = w1.shape[-1]
    return pl.pallas_call(
        _compress_body,
        grid=(b, 2),
        in_specs=[
            pl.BlockSpec((1, 1, KV_GROUPS, n_chunk, CMP_STRIDE * dh), lambda i, a: (i, a, 0, 0, 0)),
            pl.BlockSpec((None, None, 1, CMP_BLOCK * dh), lambda i, a: (layer, a, 0, 0)),
            pl.BlockSpec((None, None, CMP_BLOCK * dh, hidden), lambda i, a: (layer, a, 0, 0)),
            pl.BlockSpec((None, None, 1, hidden), lambda i, a: (layer, a, 0, 0)),
            pl.BlockSpec((None, None, hidden, dh), lambda i, a: (layer, a, 0, 0)),
        ],
        out_specs=pl.BlockSpec((1, 1, KV_GROUPS, n_chunk, dh), lambda i, a: (i, a, 0, 0, 0)),
        out_shape=jax.ShapeDtypeStruct((b, 2, KV_GROUPS, n_chunk, dh), BF16),
        compiler_params=_cparams("parallel", "parallel"),
        name="nsa_compress",
    )(x, pe, w1, b1, w2)


def _t5_bucket_np(rel):
    n = np.maximum(rel, 0)
    max_exact = NUM_BUCKETS // 2
    nf = np.maximum(n, 1).astype(np.float32)
    ratio = np.log(nf / np.float32(max_exact)) / np.float32(math.log(MAX_DISTANCE / max_exact))
    large = max_exact + (ratio * np.float32(NUM_BUCKETS - max_exact)).astype(np.int32)
    large = np.minimum(large, NUM_BUCKETS - 1)
    return np.where(n < max_exact, n, large).astype(np.int32)


def _bucket_maps(s):
    n_chunk = s // CMP_STRIDE
    t = np.arange(s)[:, None]
    blk_end = np.arange(n_chunk)[None, :] * CMP_STRIDE + CMP_BLOCK - 1
    rel_c = t - blk_end
    map_c = np.where(rel_c >= 0, _t5_bucket_np(rel_c), -1).astype(np.int32)
    i = np.arange(ATT_TILE)[:, None]
    j = np.arange(ATT_TILE)[None, :]
    diag = np.where(i - j >= 0, _t5_bucket_np(i - j), -1)
    sub = _t5_bucket_np(ATT_TILE + i - j)
    far = np.full((ATT_TILE, ATT_TILE), _t5_bucket_np(np.array(2 * ATT_TILE))[()])
    edge = np.where(j > i, _t5_bucket_np(WINDOW + i - j), -1)
    map_t = np.stack([diag, sub, far, edge]).astype(np.int32)
    return map_c, map_t


def _bias_body(rb_ref, mc_ref, mt_ref, bc_ref, bt_ref):
    hd = pl.program_id(0)

    def lookup(bucket):
        acc = jnp.where(bucket < 0, NEG_INF, 0.0).astype(F32)
        for bk in range(NUM_BUCKETS):
            acc = jnp.where(bucket == bk, rb_ref[bk, hd], acc)
        return acc

    bc_ref[0] = lookup(mc_ref[...])
    for d in range(4):
        bt_ref[d, 0] = lookup(mt_ref[d])


def _bias_tables(rel_bias, s):
    map_c, map_t = _bucket_maps(s)
    n_chunk = map_c.shape[1]
    return pl.pallas_call(
        _bias_body,
        grid=(N_HEADS,),
        in_specs=[
            pl.BlockSpec(memory_space=pltpu.SMEM),
            pl.BlockSpec((s, n_chunk), lambda i: (0, 0)),
            pl.BlockSpec((4, ATT_TILE, ATT_TILE), lambda i: (0, 0, 0)),
        ],
        out_specs=[
            pl.BlockSpec((1, s, n_chunk), lambda i: (i, 0, 0)),
            pl.BlockSpec((4, 1, ATT_TILE, ATT_TILE), lambda i: (0, i, 0, 0)),
        ],
        out_shape=[
            jax.ShapeDtypeStruct((N_HEADS, s, n_chunk), F32),
            jax.ShapeDtypeStruct((4, N_HEADS, ATT_TILE, ATT_TILE), F32),
        ],
        compiler_params=_cparams("parallel"),
        name="t5_bias_tables",
    )(rel_bias, jnp.asarray(map_c), jnp.asarray(map_t))


def _nsa_attn_body(q_ref, kc_ref, vc_ref, ks_ref, vs_ref, kw_ref, vw_ref, bc_ref, bt_ref, gate_ref,
                   ovl_ref, exp_ref, o_ref, m_ref, l_ref, acc_ref, selm_ref):
    T = ATT_TILE
    R = Q_PER_GROUP
    qi = pl.program_id(2)
    q = q_ref[0].reshape(R * T, HEAD_DIM)

    s_c = _dot_nt(q, kc_ref[0, 0, 0]) + bc_ref[...].reshape(R * T, -1)
    m_c = jnp.max(s_c, axis=-1, keepdims=True)
    p_c = jnp.exp(s_c - m_c)
    p_c = p_c * (1.0 / jnp.sum(p_c, axis=-1, keepdims=True))
    t_row = qi * T + (lax.broadcasted_iota(jnp.int32, (R * T, 1), 0) & (T - 1))
    p_c = jnp.where(t_row >= CMP_BLOCK - 1, p_c, 0.0)
    o_cmp = _dot(p_c.astype(BF16), vc_ref[0, 0, 0])

    n_sel = ovl_ref.shape[1]
    imp = _dot_exact_rhs(p_c.reshape(R, T, -1).sum(axis=0), ovl_ref[...])
    j_blk = lax.broadcasted_iota(jnp.int32, (T, n_sel), 1)
    cur = (qi * T + lax.broadcasted_iota(jnp.int32, (T, n_sel), 0)) >> SEL_SHIFT
    forced = (j_blk == 0) | (j_blk == cur) | (j_blk == cur - 1)
    imp = jnp.where(forced, FORCED_SCORE, jnp.where(j_blk <= cur, imp, NEG_INF))
    rank = jnp.zeros((T, n_sel), F32)
    for i in range(n_sel):
        col = imp[:, i:i + 1]
        ahead = (col > imp) | ((col == imp) & (j_blk > i))
        rank = rank + jnp.where(ahead, 1.0, 0.0)
    chosen = jnp.where(rank < min(SEL_TOPK, n_sel), 1.0, 0.0).astype(BF16)
    selm_ref[...] = _dot(chosen, exp_ref[...])

    def start():
        m_ref[...] = jnp.full_like(m_ref, NEG_INF)
        l_ref[...] = jnp.zeros_like(l_ref)
        acc_ref[...] = jnp.zeros_like(acc_ref)

    def tile(kt, k_ref, v_ref, bias, selected):
        off = pl.multiple_of(kt * T, T)
        s = _dot_nt(q, k_ref[0, 0, pl.ds(off, T), :]) + bias
        if selected:
            keep = selm_ref[:, pl.ds(off, T)] > 0.5
            s = jnp.where(keep[None], s.reshape(R, T, T), NEG_INF).reshape(R * T, T)
        m_old = m_ref[...]
        m_new = jnp.maximum(m_old, jnp.max(s, axis=-1, keepdims=True))
        alpha = jnp.exp(m_old - m_new)
        p = jnp.exp(s - m_new)
        l_ref[...] = alpha * l_ref[...] + jnp.sum(p, axis=-1, keepdims=True)
        acc_ref[...] = alpha * acc_ref[...] + _dot(p.astype(BF16), v_ref[0, 0, pl.ds(off, T), :])
        m_ref[...] = m_new

    def finish():
        return acc_ref[...] * (1.0 / l_ref[...])

    bias_of = lambda d: bt_ref[d].reshape(R * T, T)

    start()

    def far_tile(kt, carry):
        tile(kt, ks_ref, vs_ref, bias_of(2), True)
        return carry

    lax.fori_loop(0, jnp.maximum(qi - 1, 0), far_tile, 0)

    @pl.when(qi >= 1)
    def _():
        tile(qi - 1, ks_ref, vs_ref, bias_of(1), True)

    tile(qi, ks_ref, vs_ref, bias_of(0), True)
    o_sel = finish()

    start()
    tile(qi, kw_ref, vw_ref, bias_of(0), False)
    for d in range(1, WIN_TILES + 1):
        which = 1 if d == 1 else (3 if d == WIN_TILES else 2)

        @pl.when(qi >= d)
        def _(d=d, which=which):
            tile(qi - d, kw_ref, vw_ref, bias_of(which), False)

    o_win = finish()

    gate = gate_ref[0, 0]
    outs = []
    for r in range(R):
        rows = slice(r * T, (r + 1) * T)
        outs.append(gate[:, r:r + 1] * o_cmp[rows]
                    + gate[:, R + r:R + r + 1] * o_sel[rows]
                    + gate[:, 2 * R + r:2 * R + r + 1] * o_win[rows])
    o_ref[0] = jnp.concatenate(outs, axis=-1).astype(BF16)


def _sel_constants(s):
    n_chunk = s // CMP_STRIDE
    n_sel = s // SEL_BLOCK
    c_start = np.arange(n_chunk)[:, None] * CMP_STRIDE
    j = np.arange(n_sel)[None, :]
    overlap = (c_start < (j + 1) * SEL_BLOCK) & (c_start + CMP_BLOCK > j * SEL_BLOCK)
    overlap[n_chunk - 1] = False
    expand = np.arange(s)[None, :] // SEL_BLOCK == np.arange(n_sel)[:, None]
    return jnp.asarray(overlap, BF16), jnp.asarray(expand, BF16)


def _nsa_attn(q, kv, cmp_kv, bias_c, bias_t, gates):
    b, _, s, dh = q.shape
    T = ATT_TILE
    R = Q_PER_GROUP
    n_chunk = s // CMP_STRIDE
    n_sel = s // SEL_BLOCK
    ovl, expand = _sel_constants(s)
    kv_spec = lambda a: pl.BlockSpec((1, 1, s, dh), lambda i, g, t: (i, a * KV_GROUPS + g, 0, 0))
    cmp_spec = lambda a: pl.BlockSpec((1, 1, 1, n_chunk, dh), lambda i, g, t: (i, a, g, 0, 0))
    return pl.pallas_call(
        _nsa_attn_body,
        grid=(b, KV_GROUPS, s // T),
        in_specs=[
            pl.BlockSpec((1, R, T, dh), lambda i, g, t: (i, g, t, 0)),
            cmp_spec(0), cmp_spec(1),
            kv_spec(2), kv_spec(3), kv_spec(4), kv_spec(5),
            pl.BlockSpec((R, T, n_chunk), lambda i, g, t: (g, t, 0)),
            pl.BlockSpec((4, R, T, T), lambda i, g, t: (0, g, 0, 0)),
            pl.BlockSpec((1, 1, T, 3 * R), lambda i, g, t: (i, g, t, 0)),
            pl.BlockSpec((n_chunk, n_sel), lambda i, g, t: (0, 0)),
            pl.BlockSpec((n_sel, s), lambda i, g, t: (0, 0)),
        ],
        out_specs=pl.BlockSpec((1, T, R * dh), lambda i, g, t: (i, t, g)),
        out_shape=jax.ShapeDtypeStruct((b, s, Q_DIM), BF16),
        scratch_shapes=[
            pltpu.VMEM((R * T, 1), F32),
            pltpu.VMEM((R * T, 1), F32),
            pltpu.VMEM((R * T, dh), F32),
            pltpu.VMEM((T, s), F32),
        ],
        compiler_params=_cparams("parallel", "parallel", "arbitrary"),
        name="nsa_attn",
    )(q, cmp_kv, cmp_kv, kv, kv, kv, kv, bias_c, bias_t, gates, ovl, expand)


def _fox_proj_body(h_ref, g_ref, w_ref, bf_ref, q_ref, k_ref, v_ref, cum_ref, carry_ref):
    tm = h_ref.shape[1]

    @pl.when(pl.program_id(1) == 0)
    def _():
        carry_ref[...] = jnp.zeros_like(carry_ref)

    xn = _rms(h_ref[0], g_ref[...]).astype(BF16)
    res = _dot(xn, w_ref[...])
    scale = HEAD_DIM ** -0.5
    for hd in range(N_HEADS):
        cols = slice(hd * HEAD_DIM, (hd + 1) * HEAD_DIM)
        q_ref[0, hd] = (res[:, cols] * scale).astype(BF16)
        k_ref[0, hd] = res[:, Q_DIM + hd * HEAD_DIM:Q_DIM + (hd + 1) * HEAD_DIM].astype(BF16)
        v_ref[0, hd] = res[:, 2 * Q_DIM + hd * HEAD_DIM:2 * Q_DIM + (hd + 1) * HEAD_DIM].astype(BF16)
    log_f = jax.nn.log_sigmoid(res[:, 3 * Q_DIM:] + bf_ref[...])
    tri = jnp.where(lax.broadcasted_iota(jnp.int32, (tm, tm), 0)
                    >= lax.broadcasted_iota(jnp.int32, (tm, tm), 1), 1.0, 0.0).astype(BF16)
    hi, mid, lo = _split3(log_f)
    cum = _dot(tri, hi) + _dot(tri, mid) + _dot(tri, lo) + carry_ref[...]
    cum_ref[0] = cum
    carry_ref[...] = cum[tm - 1:tm]


def _fox_proj(h3, g, w, b_f, layer, tm=512):
    b, s, d = h3.shape
    n_in = w.shape[-1]
    head_spec = pl.BlockSpec((1, N_HEADS, tm, HEAD_DIM), lambda i, j: (i, 0, j, 0))
    head_shape = jax.ShapeDtypeStruct((b, N_HEADS, s, HEAD_DIM), BF16)
    return pl.pallas_call(
        _fox_proj_body,
        grid=(b, s // tm),
        in_specs=[
            pl.BlockSpec((1, tm, d), lambda i, j: (i, j, 0)),
            pl.BlockSpec((1, d), lambda i, j: (0, 0)),
            pl.BlockSpec((None, d, n_in), lambda i, j: (layer, 0, 0)),
            pl.BlockSpec((None, 1, N_HEADS), lambda i, j: (layer, 0, 0)),
        ],
        out_specs=[head_spec, head_spec, head_spec,
                   pl.BlockSpec((1, tm, N_HEADS), lambda i, j: (i, j, 0))],
        out_shape=[head_shape, head_shape, head_shape,
                   jax.ShapeDtypeStruct((b, s, N_HEADS), F32)],
        scratch_shapes=[pltpu.VMEM((1, N_HEADS), F32)],
        compiler_params=_cparams("parallel", "arbitrary"),
        name="fox_proj",
    )(h3, g, w, b_f)


def _fox_attn_body(q_ref, k_ref, v_ref, nc_ref, o_ref, m_ref, l_ref, acc_ref):
    T = FOX_TILE
    qi = pl.program_id(2)
    q = q_ref[0, 0]

    m_ref[...] = jnp.full_like(m_ref, NEG_INF)
    l_ref[...] = jnp.zeros_like(l_ref)
    acc_ref[...] = jnp.zeros_like(acc_ref)

    def tile(kt, diagonal):
        off = pl.multiple_of(kt * T, T)
        s = _dot_nt(q, k_ref[0, 0, pl.ds(off, T), :]) + nc_ref[0, 0, :, pl.ds(off, T)]
        if diagonal:
            causal = (lax.broadcasted_iota(jnp.int32, (T, T), 0)
                      >= lax.broadcasted_iota(jnp.int32, (T, T), 1))
            s = jnp.where(causal, s, NEG_INF)
        m_old = m_ref[...]
        m_new = jnp.maximum(m_old, jnp.max(s, axis=-1, keepdims=True))
        alpha = jnp.exp(m_old - m_new)
        p = jnp.exp(s - m_new)
        l_ref[...] = alpha * l_ref[...] + jnp.sum(p, axis=-1, keepdims=True)
        acc_ref[...] = alpha * acc_ref[...] + _dot(p.astype(BF16), v_ref[0, 0, pl.ds(off, T), :])
        m_ref[...] = m_new

    def full_tile(kt, carry):
        tile(kt, False)
        return carry

    lax.fori_loop(0, qi, full_tile, 0)
    tile(qi, True)
    o_ref[0, 0] = (acc_ref[...] * (1.0 / l_ref[...])).astype(BF16)


def _fox_attn(q, k, v, neg_cum):
    b, nh, s, dh = q.shape
    T = FOX_TILE
    full = pl.BlockSpec((1, 1, s, dh), lambda i, h, t: (i, h, 0, 0))
    return pl.pallas_call(
        _fox_attn_body,
        grid=(b, nh, s // T),
        in_specs=[
            pl.BlockSpec((1, 1, T, dh), lambda i, h, t: (i, h, t, 0)),
            full, full,
            pl.BlockSpec((1, 1, 1, s), lambda i, h, t: (i, h, 0, 0)),
        ],
        out_specs=pl.BlockSpec((1, 1, T, dh), lambda i, h, t: (i, h, t, 0)),
        out_shape=jax.ShapeDtypeStruct((b, nh, s, dh), BF16),
        scratch_shapes=[
            pltpu.VMEM((T, 1), F32),
            pltpu.VMEM((T, 1), F32),
            pltpu.VMEM((T, dh), F32),
        ],
        compiler_params=_cparams("parallel", "parallel", "arbitrary"),
        name="fox_attn",
    )(q, k, v, neg_cum)


def kernel(x, norm_g, ffn_w_gate, ffn_w_up, ffn_w_down, rel_bias, nsa_w_in, nsa_cmp_pe, nsa_cmp_w1,
           nsa_cmp_b1, nsa_cmp_w2, nsa_w_out, fox_w_in, fox_b_f, fox_w_out):
    b, s, d = x.shape
    depth = norm_g.shape[0]
    n = b * s
    wg, wu, wd = ffn_w_gate.astype(BF16), ffn_w_up.astype(BF16), ffn_w_down.astype(BF16)
    gate_cols = (Q_DIM + 6 * KV_DIM
                 + np.arange(N_GATES).reshape(3, KV_GROUPS, Q_PER_GROUP).transpose(1, 0, 2).reshape(-1))
    col_order = np.concatenate([np.arange(Q_DIM + 6 * KV_DIM), gate_cols])
    nsa_w_in_b, nsa_w_out_b = nsa_w_in[:, :, col_order].astype(BF16), nsa_w_out.astype(BF16)
    fox_w_in_b, fox_w_out_b = fox_w_in.astype(BF16), fox_w_out.astype(BF16)
    cmp_w1_b, cmp_w2_b = nsa_cmp_w1.astype(BF16), nsa_cmp_w2.astype(BF16)
    cmp_pe = nsa_cmp_pe.reshape(nsa_cmp_pe.shape[0], 2, 1, CMP_BLOCK * HEAD_DIM)
    cmp_b1 = nsa_cmp_b1[:, :, None, :]
    fox_bf = fox_b_f[:, None, :]
    gains = norm_g[:, :, None, :]

    bias_c, bias_t = _bias_tables(rel_bias, s)

    h = x.reshape(n, d)
    for i in range(depth):
        g = gains[i]
        j = i // 2
        h = _ffn(h, g[0], g[1], wg, wu, wd, i, 0)
        if i % 2 == 0:
            q, kv, gates = _nsa_proj(h.reshape(b, s, d), g[2], nsa_w_in_b, j)
            cmp_kv = _compress(kv, cmp_pe, cmp_w1_b, cmp_b1, cmp_w2_b, j)
            o = _nsa_attn(q, kv, cmp_kv, bias_c, bias_t, gates).reshape(n, Q_DIM)
            h = _outproj(o, nsa_w_out_b, h, g[3], j)
        else:
            q, k, v, cum = _fox_proj(h.reshape(b, s, d), g[2], fox_w_in_b, fox_bf, j)
            neg_cum = (-cum).transpose(0, 2, 1)[:, :, None, :]
            o = _fox_attn(q, k, v, neg_cum)
            o = o.transpose(0, 2, 1, 3).reshape(n, Q_DIM)
            h = _outproj(o, fox_w_out_b, h, g[3], j)
        h = _ffn(h, g[4], g[5], wg, wu, wd, i, 1)
    return h.reshape(b, s, d)
```

```python
import functools
import math

import numpy as np
import jax
import jax.numpy as jnp
from jax import lax
from jax.experimental import pallas as pl
from jax.experimental.pallas import tpu as pltpu

N_HEADS = 16
HEAD_DIM = 64
KV_GROUPS = 4
Q_PER_GROUP = N_HEADS // KV_GROUPS
CMP_BLOCK = 32
CMP_STRIDE = 16
SEL_BLOCK = 64
SEL_SHIFT = 6
SEL_TOPK = 16
WINDOW = 512
NUM_BUCKETS = 32
MAX_DISTANCE = 128
RMS_EPS = 1e-6
NEG_INF = -1e30
FORCED_SCORE = 1e9
Q_DIM = N_HEADS * HEAD_DIM
KV_DIM = KV_GROUPS * HEAD_DIM
N_GATES = 3 * N_HEADS

ATT_TILE = 128
WIN_TILES = WINDOW // ATT_TILE
FOX_TILE = 256
VMEM_LIMIT = 56 * 1024 * 1024

BF16 = jnp.bfloat16
F32 = jnp.float32


def _cparams(*sem):
    return pltpu.CompilerParams(dimension_semantics=sem, vmem_limit_bytes=VMEM_LIMIT)


def _rms(x, g):
    return x * lax.rsqrt(jnp.mean(x * x, axis=-1, keepdims=True) + RMS_EPS) * g


def _dot(a, b):
    return jnp.dot(a, b, preferred_element_type=F32)


def _dot_nt(a, b):
    return lax.dot_general(a, b, (((1,), (1,)), ((), ())), preferred_element_type=F32)


def _split3(x):
    hi = x.astype(BF16)
    r1 = x - hi.astype(F32)
    mid = r1.astype(BF16)
    lo = (r1 - mid.astype(F32)).astype(BF16)
    return hi, mid, lo


def _dot_exact_rhs(x, m_bf16):
    hi, mid, lo = _split3(x)
    return _dot(hi, m_bf16) + _dot(mid, m_bf16) + _dot(lo, m_bf16)


def _ffn_body(h_ref, gpre_ref, gpost_ref, wg_ref, wu_ref, wd_ref, o_ref, xn_ref, acc_ref):
    k = pl.program_id(1)

    @pl.when(k == 0)
    def _():
        xn_ref[...] = _rms(h_ref[...], gpre_ref[...]).astype(BF16)
        acc_ref[...] = jnp.zeros_like(acc_ref)

    xn = xn_ref[...]
    g = _dot(xn, wg_ref[...])
    u = _dot(xn, wu_ref[...])
    a = (g * jax.nn.sigmoid(g) * u).astype(BF16)
    acc_ref[...] += _dot(a, wd_ref[...])

    @pl.when(k == pl.num_programs(1) - 1)
    def _():
        o_ref[...] = h_ref[...] + 0.5 * _rms(acc_ref[...], gpost_ref[...])


def _ffn(h, g_pre, g_post, wg, wu, wd, layer, half, tm=512, tf=1408):
    n, d = h.shape
    f = wg.shape[-1]
    row = lambda i, k: (i, 0)
    vec = lambda i, k: (0, 0)
    return pl.pallas_call(
        _ffn_body,
        grid=(n // tm, f // tf),
        in_specs=[
            pl.BlockSpec((tm, d), row),
            pl.BlockSpec((1, d), vec),
            pl.BlockSpec((1, d), vec),
            pl.BlockSpec((None, None, d, tf), lambda i, k: (layer, half, 0, k)),
            pl.BlockSpec((None, None, d, tf), lambda i, k: (layer, half, 0, k)),
            pl.BlockSpec((None, None, tf, d), lambda i, k: (layer, half, k, 0)),
        ],
        out_specs=pl.BlockSpec((tm, d), row),
        out_shape=jax.ShapeDtypeStruct((n, d), F32),
        scratch_shapes=[pltpu.VMEM((tm, d), BF16), pltpu.VMEM((tm, d), F32)],
        compiler_params=_cparams("parallel", "arbitrary"),
        name="ffn",
    )(h, g_pre, g_post, wg, wu, wd)


def _outproj_body(o_ref, w_ref, h_ref, g_ref, out_ref):
    y = _dot(o_ref[...], w_ref[...])
    out_ref[...] = h_ref[...] + _rms(y, g_ref[...])


def _outproj(o, w, h, g, layer, tm=512):
    n, d = h.shape
    kdim = o.shape[-1]
    return pl.pallas_call(
        _outproj_body,
        grid=(n // tm,),
        in_specs=[
            pl.BlockSpec((tm, kdim), lambda i: (i, 0)),
            pl.BlockSpec((None, kdim, d), lambda i: (layer, 0, 0)),
            pl.BlockSpec((tm, d), lambda i: (i, 0)),
            pl.BlockSpec((1, d), lambda i: (0, 0)),
        ],
        out_specs=pl.BlockSpec((tm, d), lambda i: (i, 0)),
        out_shape=jax.ShapeDtypeStruct((n, d), F32),
        compiler_params=_cparams("parallel"),
        name="outproj",
    )(o, w, h, g)


def _nsa_proj_body(h_ref, g_ref, w_ref, q_ref, kv_ref, gate_ref):
    xn = _rms(h_ref[0], g_ref[...]).astype(BF16)
    res = _dot(xn, w_ref[...])
    scale = HEAD_DIM ** -0.5
    for hd in range(N_HEADS):
        q_ref[0, hd] = (res[:, hd * HEAD_DIM:(hd + 1) * HEAD_DIM] * scale).astype(BF16)
    for a in range(6 * KV_GROUPS):
        lo = Q_DIM + a * HEAD_DIM
        kv_ref[0, a] = res[:, lo:lo + HEAD_DIM].astype(BF16)
    gs = jax.nn.sigmoid(res[:, Q_DIM + 6 * KV_DIM:])
    width = 3 * Q_PER_GROUP
    for grp in range(KV_GROUPS):
        gate_ref[0, grp] = gs[:, grp * width:(grp + 1) * width]


def _nsa_proj(h3, g, w, layer, tm=512):
    b, s, d = h3.shape
    n_in = w.shape[-1]
    return pl.pallas_call(
        _nsa_proj_body,
        grid=(b, s // tm),
        in_specs=[
            pl.BlockSpec((1, tm, d), lambda i, j: (i, j, 0)),
            pl.BlockSpec((1, d), lambda i, j: (0, 0)),
            pl.BlockSpec((None, d, n_in), lambda i, j: (layer, 0, 0)),
        ],
        out_specs=[
            pl.BlockSpec((1, N_HEADS, tm, HEAD_DIM), lambda i, j: (i, 0, j, 0)),
            pl.BlockSpec((1, 6 * KV_GROUPS, tm, HEAD_DIM), lambda i, j: (i, 0, j, 0)),
            pl.BlockSpec((1, KV_GROUPS, tm, 3 * Q_PER_GROUP), lambda i, j: (i, 0, j, 0)),
        ],
        out_shape=[
            jax.ShapeDtypeStruct((b, N_HEADS, s, HEAD_DIM), BF16),
            jax.ShapeDtypeStruct((b, 6 * KV_GROUPS, s, HEAD_DIM), BF16),
            jax.ShapeDtypeStruct((b, KV_GROUPS, s, 3 * Q_PER_GROUP), F32),
        ],
        compiler_params=_cparams("parallel", "parallel"),
        name="nsa_proj",
    )(h3, g, w)


def _compress_body(x_ref, pe_ref, w1_ref, b1_ref, w2_ref, o_ref):
    n_chunk = x_ref.shape[3]
    half = CMP_STRIDE * HEAD_DIM
    x = x_ref[0, 0].reshape(KV_GROUPS * n_chunk, half)
    top = _dot(x, w1_ref[:half, :])
    bot = _dot(x, w1_ref[half:, :])
    bot_next = pltpu.roll(bot, KV_GROUPS * n_chunk - 1, 0)
    pe = jnp.broadcast_to(pe_ref[...].astype(BF16), (8, 2 * half))
    const = _dot(pe, w1_ref[...])[0:1] + b1_ref[...]
    hid = jax.nn.gelu(top + bot_next + const).astype(BF16)
    out = _dot(hid, w2_ref[...])
    row = lax.broadcasted_iota(jnp.int32, out.shape, 0) & (n_chunk - 1)
    out = jnp.where(row < n_chunk - 1, out, 0.0)
    o_ref[0, 0] = out.reshape(KV_GROUPS, n_chunk, HEAD_DIM).astype(BF16)


def _compress(kv, pe, w1, b1, w2, layer):
    b, _, s, dh = kv.shape
    n_chunk = s // CMP_STRIDE
    x = kv.reshape(b, 6, KV_GROUPS, n_chunk, CMP_STRIDE * dh)
    hidden = w1.shape[-1]
    return pl.pallas_call(
        _compress_body,
        grid=(b, 2),
        in_specs=[
            pl.BlockSpec((1, 1, KV_GROUPS, n_chunk, CMP_STRIDE * dh), lambda i, a: (i, a, 0, 0, 0)),
            pl.BlockSpec((None, None, 1, CMP_BLOCK * dh), lambda i, a: (layer, a, 0, 0)),
            pl.BlockSpec((None, None, CMP_BLOCK * dh, hidden), lambda i, a: (layer, a, 0, 0)),
            pl.BlockSpec((None, None, 1, hidden), lambda i, a: (layer, a, 0, 0)),
            pl.BlockSpec((None, None, hidden, dh), lambda i, a: (layer, a, 0, 0)),
        ],
        out_specs=pl.BlockSpec((1, 1, KV_GROUPS, n_chunk, dh), lambda i, a: (i, a, 0, 0, 0)),
        out_shape=jax.ShapeDtypeStruct((b, 2, KV_GROUPS, n_chunk, dh), BF16),
        compiler_params=_cparams("parallel", "parallel"),
        name="nsa_compress",
    )(x, pe, w1, b1, w2)


def _t5_bucket_np(rel):
    n = np.maximum(rel, 0)
    max_exact = NUM_BUCKETS // 2
    nf = np.maximum(n, 1).astype(np.float32)
    ratio = np.log(nf / np.float32(max_exact)) / np.float32(math.log(MAX_DISTANCE / max_exact))
    large = max_exact + (ratio * np.float32(NUM_BUCKETS - max_exact)).astype(np.int32)
    large = np.minimum(large, NUM_BUCKETS - 1)
    return np.where(n < max_exact, n, large).astype(np.int32)


def _bucket_maps(s):
    n_chunk = s // CMP_STRIDE
    t = np.arange(s)[:, None]
    blk_end = np.arange(n_chunk)[None, :] * CMP_STRIDE + CMP_BLOCK - 1
    rel_c = t - blk_end
    map_c = np.where(rel_c >= 0, _t5_bucket_np(rel_c), -1).astype(np.int32)
    i = np.arange(ATT_TILE)[:, None]
    j = np.arange(ATT_TILE)[None, :]
    diag = np.where(i - j >= 0, _t5_bucket_np(i - j), -1)
    sub = _t5_bucket_np(ATT_TILE + i - j)
    far = np.full((ATT_TILE, ATT_TILE), _t5_bucket_np(np.array(2 * ATT_TILE))[()])
    edge = np.where(j > i, _t5_bucket_np(WINDOW + i - j), -1)
    map_t = np.stack([diag, sub, far, edge]).astype(np.int32)
    return map_c, map_t


def _bias_body(rb_ref, mc_ref, mt_ref, bc_ref, bt_ref):
    hd = pl.program_id(0)

    def lookup(bucket):
        acc = jnp.where(bucket < 0, NEG_INF, 0.0).astype(F32)
        for bk in range(NUM_BUCKETS):
            acc = jnp.where(bucket == bk, rb_ref[bk, hd], acc)
        return acc

    bc_ref[0] = lookup(mc_ref[...])
    for d in range(4):
        bt_ref[d, 0] = lookup(mt_ref[d])


def _bias_tables(rel_bias, s):
    map_c, map_t = _bucket_maps(s)
    n_chunk = map_c.shape[1]
    return pl.pallas_call(
        _bias_body,
        grid=(N_HEADS,),
        in_specs=[
            pl.BlockSpec(memory_space=pltpu.SMEM),
            pl.BlockSpec((s, n_chunk), lambda i: (0, 0)),
            pl.BlockSpec((4, ATT_TILE, ATT_TILE), lambda i: (0, 0, 0)),
        ],
        out_specs=[
            pl.BlockSpec((1, s, n_chunk), lambda i: (i, 0, 0)),
            pl.BlockSpec((4, 1, ATT_TILE, ATT_TILE), lambda i: (0, i, 0, 0)),
        ],
        out_shape=[
            jax.ShapeDtypeStruct((N_HEADS, s, n_chunk), F32),
            jax.ShapeDtypeStruct((4, N_HEADS, ATT_TILE, ATT_TILE), F32),
        ],
        compiler_params=_cparams("parallel"),
        name="t5_bias_tables",
    )(rel_bias, jnp.asarray(map_c), jnp.asarray(map_t))


def _nsa_attn_body(q_ref, kc_ref, vc_ref, ks_ref, vs_ref, kw_ref, vw_ref, bc_ref, bt_ref, gate_ref,
                   ovl_ref, exp_ref, o_ref, m_ref, l_ref, acc_ref, selm_ref):
    T = ATT_TILE
    R = Q_PER_GROUP
    qi = pl.program_id(2)
    q = q_ref[0].reshape(R * T, HEAD_DIM)

    s_c = _dot_nt(q, kc_ref[0, 0, 0]) + bc_ref[...].reshape(R * T, -1)
    m_c = jnp.max(s_c, axis=-1, keepdims=True)
    p_c = jnp.exp(s_c - m_c)
    p_c = p_c * (1.0 / jnp.sum(p_c, axis=-1, keepdims=True))
    t_row = qi * T + (lax.broadcasted_iota(jnp.int32, (R * T, 1), 0) & (T - 1))
    p_c = jnp.where(t_row >= CMP_BLOCK - 1, p_c, 0.0)
    o_cmp = _dot(p_c.astype(BF16), vc_ref[0, 0, 0])

    n_sel = ovl_ref.shape[1]
    imp = _dot_exact_rhs(p_c.reshape(R, T, -1).sum(axis=0), ovl_ref[...])
    j_blk = lax.broadcasted_iota(jnp.int32, (T, n_sel), 1)
    cur = (qi * T + lax.broadcasted_iota(jnp.int32, (T, n_sel), 0)) >> SEL_SHIFT
    forced = (j_blk == 0) | (j_blk == cur) | (j_blk == cur - 1)
    imp = jnp.where(forced, FORCED_SCORE, jnp.where(j_blk <= cur, imp, NEG_INF))
    rank = jnp.zeros((T, n_sel), F32)
    for i in range(n_sel):
        col = imp[:, i:i + 1]
        ahead = (col > imp) | ((col == imp) & (j_blk > i))
        rank = rank + jnp.where(ahead, 1.0, 0.0)
    chosen = jnp.where(rank < min(SEL_TOPK, n_sel), 1.0, 0.0).astype(BF16)
    selm_ref[...] = _dot(chosen, exp_ref[...])

    def start():
        m_ref[...] = jnp.full_like(m_ref, NEG_INF)
        l_ref[...] = jnp.zeros_like(l_ref)
        acc_ref[...] = jnp.zeros_like(acc_ref)

    def tile(kt, k_ref, v_ref, bias, selected):
        off = pl.multiple_of(kt * T, T)
        s = _dot_nt(q, k_ref[0, 0, pl.ds(off, T), :]) + bias
        if selected:
            keep = selm_ref[:, pl.ds(off, T)] > 0.5
            s = jnp.where(keep[None], s.reshape(R, T, T), NEG_INF).reshape(R * T, T)
        m_old = m_ref[...]
        m_new = jnp.maximum(m_old, jnp.max(s, axis=-1, keepdims=True))
        alpha = jnp.exp(m_old - m_new)
        p = jnp.exp(s - m_new)
        l_ref[...] = alpha * l_ref[...] + jnp.sum(p, axis=-1, keepdims=True)
        acc_ref[...] = alpha * acc_ref[...] + _dot(p.astype(BF16), v_ref[0, 0, pl.ds(off, T), :])
        m_ref[...] = m_new

    def finish():
        return acc_ref[...] * (1.0 / l_ref[...])

    bias_of = lambda d: bt_ref[d].reshape(R * T, T)

    start()

    def far_tile(kt, carry):
        tile(kt, ks_ref, vs_ref, bias_of(2), True)
        return carry

    lax.fori_loop(0, jnp.maximum(qi - 1, 0), far_tile, 0)

    @pl.when(qi >= 1)
    def _():
        tile(qi - 1, ks_ref, vs_ref, bias_of(1), True)

    tile(qi, ks_ref, vs_ref, bias_of(0), True)
    o_sel = finish()

    start()
    tile(qi, kw_ref, vw_ref, bias_of(0), False)
    for d in range(1, WIN_TILES + 1):
        which = 1 if d == 1 else (3 if d == WIN_TILES else 2)

        @pl.when(qi >= d)
        def _(d=d, which=which):
            tile(qi - d, kw_ref, vw_ref, bias_of(which), False)

    o_win = finish()

    gate = gate_ref[0, 0]
    outs = []
    for r in range(R):
        rows = slice(r * T, (r + 1) * T)
        outs.append(gate[:, r:r + 1] * o_cmp[rows]
                    + gate[:, R + r:R + r + 1] * o_sel[rows]
                    + gate[:, 2 * R + r:2 * R + r + 1] * o_win[rows])
    o_ref[0] = jnp.concatenate(outs, axis=-1).astype(BF16)


def _sel_constants(s):
    n_chunk = s // CMP_STRIDE
    n_sel = s // SEL_BLOCK
    c_start = np.arange(n_chunk)[:, None] * CMP_STRIDE
    j = np.arange(n_sel)[None, :]
    overlap = (c_start < (j + 1) * SEL_BLOCK) & (c_start + CMP_BLOCK > j * SEL_BLOCK)
    overlap[n_chunk - 1] = False
    expand = np.arange(s)[None, :] // SEL_BLOCK == np.arange(n_sel)[:, None]
    return jnp.asarray(overlap, BF16), jnp.asarray(expand, BF16)


def _nsa_attn(q, kv, cmp_kv, bias_c, bias_t, gates):
    b, _, s, dh = q.shape
    T = ATT_TILE
    R = Q_PER_GROUP
    n_chunk = s // CMP_STRIDE
    n_sel = s // SEL_BLOCK
    ovl, expand = _sel_constants(s)
    kv_spec = lambda a: pl.BlockSpec((1, 1, s, dh), lambda i, g, t: (i, a * KV_GROUPS + g, 0, 0))
    cmp_spec = lambda a: pl.BlockSpec((1, 1, 1, n_chunk, dh), lambda i, g, t: (i, a, g, 0, 0))
    return pl.pallas_call(
        _nsa_attn_body,
        grid=(b, KV_GROUPS, s // T),
        in_specs=[
            pl.BlockSpec((1, R, T, dh), lambda i, g, t: (i, g, t, 0)),
            cmp_spec(0), cmp_spec(1),
            kv_spec(2), kv_spec(3), kv_spec(4), kv_spec(5),
            pl.BlockSpec((R, T, n_chunk), lambda i, g, t: (g, t, 0)),
            pl.BlockSpec((4, R, T, T), lambda i, g, t: (0, g, 0, 0)),
            pl.BlockSpec((1, 1, T, 3 * R), lambda i, g, t: (i, g, t, 0)),
            pl.BlockSpec((n_chunk, n_sel), lambda i, g, t: (0, 0)),
            pl.BlockSpec((n_sel, s), lambda i, g, t: (0, 0)),
        ],
        out_specs=pl.BlockSpec((1, T, R * dh), lambda i, g, t: (i, t, g)),
        out_shape=jax.ShapeDtypeStruct((b, s, Q_DIM), BF16),
        scratch_shapes=[
            pltpu.VMEM((R * T, 1), F32),
            pltpu.VMEM((R * T, 1), F32),
            pltpu.VMEM((R * T, dh), F32),
            pltpu.VMEM((T, s), F32),
        ],
        compiler_params=_cparams("parallel", "parallel", "arbitrary"),
        name="nsa_attn",
    )(q, cmp_kv, cmp_kv, kv, kv, kv, kv, bias_c, bias_t, gates, ovl, expand)


def _fox_proj_body(h_ref, g_ref, w_ref, bf_ref, q_ref, k_ref, v_ref, cum_ref, carry_ref):
    tm = h_ref.shape[1]

    @pl.when(pl.program_id(1) == 0)
    def _():
        carry_ref[...] = jnp.zeros_like(carry_ref)

    xn = _rms(h_ref[0], g_ref[...]).astype(BF16)
    res = _dot(xn, w_ref[...])
    scale = HEAD_DIM ** -0.5
    for hd in range(N_HEADS):
        cols = slice(hd * HEAD_DIM, (hd + 1) * HEAD_DIM)
        q_ref[0, hd] = (res[:, cols] * scale).astype(BF16)
        k_ref[0, hd] = res[:, Q_DIM + hd * HEAD_DIM:Q_DIM + (hd + 1) * HEAD_DIM].astype(BF16)
        v_ref[0, hd] = res[:, 2 * Q_DIM + hd * HEAD_DIM:2 * Q_DIM + (hd + 1) * HEAD_DIM].astype(BF16)
    log_f = jax.nn.log_sigmoid(res[:, 3 * Q_DIM:] + bf_ref[...])
    tri = jnp.where(lax.broadcasted_iota(jnp.int32, (tm, tm), 0)
                    >= lax.broadcasted_iota(jnp.int32, (tm, tm), 1), 1.0, 0.0).astype(BF16)
    hi, mid, lo = _split3(log_f)
    cum = _dot(tri, hi) + _dot(tri, mid) + _dot(tri, lo) + carry_ref[...]
    cum_ref[0] = cum
    carry_ref[...] = cum[tm - 1:tm]


def _fox_proj(h3, g, w, b_f, layer, tm=512):
    b, s, d = h3.shape
    n_in = w.shape[-1]
    head_spec = pl.BlockSpec((1, N_HEADS, tm, HEAD_DIM), lambda i, j: (i, 0, j, 0))
    head_shape = jax.ShapeDtypeStruct((b, N_HEADS, s, HEAD_DIM), BF16)
    return pl.pallas_call(
        _fox_proj_body,
        grid=(b, s // tm),
        in_specs=[
            pl.BlockSpec((1, tm, d), lambda i, j: (i, j, 0)),
            pl.BlockSpec((1, d), lambda i, j: (0, 0)),
            pl.BlockSpec((None, d, n_in), lambda i, j: (layer, 0, 0)),
            pl.BlockSpec((None, 1, N_HEADS), lambda i, j: (layer, 0, 0)),
        ],
        out_specs=[head_spec, head_spec, head_spec,
                   pl.BlockSpec((1, tm, N_HEADS), lambda i, j: (i, j, 0))],
        out_shape=[head_shape, head_shape, head_shape,
                   jax.ShapeDtypeStruct((b, s, N_HEADS), F32)],
        scratch_shapes=[pltpu.VMEM((1, N_HEADS), F32)],
        compiler_params=_cparams("parallel", "arbitrary"),
        name="fox_proj",
    )(h3, g, w, b_f)


def _fox_attn_body(q_ref, k_ref, v_ref, nc_ref, o_ref, s_ref, p_ref):
    T = FOX_TILE
    n_q = q_ref.shape[2] // T
    causal = (lax.broadcasted_iota(jnp.int32, (T, T), 0)
              >= lax.broadcasted_iota(jnp.int32, (T, T), 1))
    for qi in range(n_q):
        slot = qi % 2
        rows = slice(qi * T, (qi + 1) * T)
        outs = []
        for hh in range(q_ref.shape[1]):
            q = q_ref[0, hh, rows, :]
            peak = None
            for kt in range(qi + 1):
                cols = slice(kt * T, (kt + 1) * T)
                s = _dot_nt(q, k_ref[0, hh, cols, :]) + nc_ref[0, hh, :, cols]
                if kt == qi:
                    s = jnp.where(causal, s, NEG_INF)
                s_ref[slot, hh, :, cols] = s
                part = jnp.maximum(s[:, :T // 2], s[:, T // 2:])
                peak = part if peak is None else jnp.maximum(peak, part)
            m = jnp.max(peak, axis=-1, keepdims=True)
            norm = None
            for kt in range(qi + 1):
                cols = slice(kt * T, (kt + 1) * T)
                p = jnp.exp(s_ref[slot, hh, :, cols] - m)
                p_ref[slot, hh, :, cols] = p.astype(BF16)
                part = p[:, :T // 2] + p[:, T // 2:]
                norm = part if norm is None else norm + part
            extent = (qi + 1) * T
            acc = _dot(p_ref[slot, hh, :, :extent], v_ref[0, hh, :extent, :])
            outs.append(acc * (1.0 / jnp.sum(norm, axis=-1, keepdims=True)))
        o_ref[0, rows, :] = jnp.concatenate(outs, axis=-1).astype(BF16)


def _fox_attn(q, k, v, neg_cum, heads_per_step=2):
    b, nh, s, dh = q.shape
    T = FOX_TILE
    hp = heads_per_step
    full = pl.BlockSpec((1, hp, s, dh), lambda i, h: (i, h, 0, 0))
    return pl.pallas_call(
        _fox_attn_body,
        grid=(b, nh // hp),
        in_specs=[full, full, full, pl.BlockSpec((1, hp, 1, s), lambda i, h: (i, h, 0, 0))],
        out_specs=pl.BlockSpec((1, s, hp * dh), lambda i, h: (i, 0, h)),
        out_shape=jax.ShapeDtypeStruct((b, s, nh * dh), BF16),
        scratch_shapes=[
            pltpu.VMEM((2, hp, T, s), F32),
            pltpu.VMEM((2, hp, T, s), BF16),
        ],
        compiler_params=_cparams("parallel", "parallel"),
        name="fox_attn",
    )(q, k, v, neg_cum)


def kernel(x, norm_g, ffn_w_gate, ffn_w_up, ffn_w_down, rel_bias, nsa_w_in, nsa_cmp_pe, nsa_cmp_w1,
           nsa_cmp_b1, nsa_cmp_w2, nsa_w_out, fox_w_in, fox_b_f, fox_w_out):
    b, s, d = x.shape
    depth = norm_g.shape[0]
    n = b * s
    wg, wu, wd = ffn_w_gate.astype(BF16), ffn_w_up.astype(BF16), ffn_w_down.astype(BF16)
    gate_cols = (Q_DIM + 6 * KV_DIM
                 + np.arange(N_GATES).reshape(3, KV_GROUPS, Q_PER_GROUP).transpose(1, 0, 2).reshape(-1))
    col_order = np.concatenate([np.arange(Q_DIM + 6 * KV_DIM), gate_cols])
    nsa_w_in_b, nsa_w_out_b = nsa_w_in[:, :, col_order].astype(BF16), nsa_w_out.astype(BF16)
    fox_w_in_b, fox_w_out_b = fox_w_in.astype(BF16), fox_w_out.astype(BF16)
    cmp_w1_b, cmp_w2_b = nsa_cmp_w1.astype(BF16), nsa_cmp_w2.astype(BF16)
    cmp_pe = nsa_cmp_pe.reshape(nsa_cmp_pe.shape[0], 2, 1, CMP_BLOCK * HEAD_DIM)
    cmp_b1 = nsa_cmp_b1[:, :, None, :]
    fox_bf = fox_b_f[:, None, :]
    gains = norm_g[:, :, None, :]

    bias_c, bias_t = _bias_tables(rel_bias, s)

    h = x.reshape(n, d)
    for i in range(depth):
        g = gains[i]
        j = i // 2
        h = _ffn(h, g[0], g[1], wg, wu, wd, i, 0)
        if i % 2 == 0:
            q, kv, gates = _nsa_proj(h.reshape(b, s, d), g[2], nsa_w_in_b, j)
            cmp_kv = _compress(kv, cmp_pe, cmp_w1_b, cmp_b1, cmp_w2_b, j)
            o = _nsa_attn(q, kv, cmp_kv, bias_c, bias_t, gates).reshape(n, Q_DIM)
            h = _outproj(o, nsa_w_out_b, h, g[3], j)
        else:
            q, k, v, cum = _fox_proj(h.reshape(b, s, d), g[2], fox_w_in_b, fox_bf, j)
            neg_cum = (-cum).transpose(0, 2, 1)[:, :, None, :]
            o = _fox_attn(q, k, v, neg_cum).reshape(n, Q_DIM)
            h = _outproj(o, fox_w_out_b, h, g[3], j)
        h = _ffn(h, g[4], g[5], wg, wu, wd, i, 1)
    return h.reshape(b, s, d)
```

```python
import functools
import math

import numpy as np
import jax
import jax.numpy as jnp
from jax import lax
from jax.experimental import pallas as pl
from jax.experimental.pallas import tpu as pltpu

N_HEADS = 16
HEAD_DIM = 64
KV_GROUPS = 4
Q_PER_GROUP = N_HEADS // KV_GROUPS
CMP_BLOCK = 32
CMP_STRIDE = 16
SEL_BLOCK = 64
SEL_SHIFT = 6
SEL_TOPK = 16
WINDOW = 512
NUM_BUCKETS = 32
MAX_DISTANCE = 128
RMS_EPS = 1e-6
NEG_INF = -1e30
FORCED_SCORE = 1e9
Q_DIM = N_HEADS * HEAD_DIM
KV_DIM = KV_GROUPS * HEAD_DIM
N_GATES = 3 * N_HEADS

ATT_TILE = 256
WIN_TILES = WINDOW // ATT_TILE
FAR_CHUNK = 512
FOX_TILE = 256
VMEM_LIMIT = 56 * 1024 * 1024

BF16 = jnp.bfloat16
F32 = jnp.float32


def _cparams(*sem):
    return pltpu.CompilerParams(dimension_semantics=sem, vmem_limit_bytes=VMEM_LIMIT)


def _rms(x, g):
    return x * lax.rsqrt(jnp.mean(x * x, axis=-1, keepdims=True) + RMS_EPS) * g


def _dot(a, b):
    return jnp.dot(a, b, preferred_element_type=F32)


def _dot_nt(a, b):
    return lax.dot_general(a, b, (((1,), (1,)), ((), ())), preferred_element_type=F32)


def _split3(x):
    hi = x.astype(BF16)
    r1 = x - hi.astype(F32)
    mid = r1.astype(BF16)
    lo = (r1 - mid.astype(F32)).astype(BF16)
    return hi, mid, lo


def _dot_exact_rhs(x, m_bf16):
    hi, mid, lo = _split3(x)
    return _dot(hi, m_bf16) + _dot(mid, m_bf16) + _dot(lo, m_bf16)


def _ffn_body(h_ref, gpre_ref, gpost_ref, wg_ref, wu_ref, wd_ref, o_ref, xn_ref, acc_ref):
    k = pl.program_id(1)

    @pl.when(k == 0)
    def _():
        xn_ref[...] = _rms(h_ref[...], gpre_ref[...]).astype(BF16)
        acc_ref[...] = jnp.zeros_like(acc_ref)

    xn = xn_ref[...]
    g = _dot(xn, wg_ref[...])
    u = _dot(xn, wu_ref[...])
    a = (g * jax.nn.sigmoid(g) * u).astype(BF16)
    acc_ref[...] += _dot(a, wd_ref[...])

    @pl.when(k == pl.num_programs(1) - 1)
    def _():
        o_ref[...] = h_ref[...] + 0.5 * _rms(acc_ref[...], gpost_ref[...])


def _ffn(h, g_pre, g_post, wg, wu, wd, layer, half, tm=512, tf=1408):
    n, d = h.shape
    f = wg.shape[-1]
    row = lambda i, k: (i, 0)
    vec = lambda i, k: (0, 0)
    return pl.pallas_call(
        _ffn_body,
        grid=(n // tm, f // tf),
        in_specs=[
            pl.BlockSpec((tm, d), row),
            pl.BlockSpec((1, d), vec),
            pl.BlockSpec((1, d), vec),
            pl.BlockSpec((None, None, d, tf), lambda i, k: (layer, half, 0, k)),
            pl.BlockSpec((None, None, d, tf), lambda i, k: (layer, half, 0, k)),
            pl.BlockSpec((None, None, tf, d), lambda i, k: (layer, half, k, 0)),
        ],
        out_specs=pl.BlockSpec((tm, d), row),
        out_shape=jax.ShapeDtypeStruct((n, d), F32),
        scratch_shapes=[pltpu.VMEM((tm, d), BF16), pltpu.VMEM((tm, d), F32)],
        compiler_params=_cparams("parallel", "arbitrary"),
        name="ffn",
    )(h, g_pre, g_post, wg, wu, wd)


def _outproj_body(o_ref, w_ref, h_ref, g_ref, out_ref):
    y = _dot(o_ref[...], w_ref[...])
    out_ref[...] = h_ref[...] + _rms(y, g_ref[...])


def _outproj(o, w, h, g, layer, tm=512):
    n, d = h.shape
    kdim = o.shape[-1]
    return pl.pallas_call(
        _outproj_body,
        grid=(n // tm,),
        in_specs=[
            pl.BlockSpec((tm, kdim), lambda i: (i, 0)),
            pl.BlockSpec((None, kdim, d), lambda i: (layer, 0, 0)),
            pl.BlockSpec((tm, d), lambda i: (i, 0)),
            pl.BlockSpec((1, d), lambda i: (0, 0)),
        ],
        out_specs=pl.BlockSpec((tm, d), lambda i: (i, 0)),
        out_shape=jax.ShapeDtypeStruct((n, d), F32),
        compiler_params=_cparams("parallel"),
        name="outproj",
    )(o, w, h, g)


def _nsa_proj_body(h_ref, g_ref, w_ref, q_ref, kv_ref, gate_ref):
    xn = _rms(h_ref[0], g_ref[...]).astype(BF16)
    res = _dot(xn, w_ref[...])
    scale = HEAD_DIM ** -0.5
    for hd in range(N_HEADS):
        q_ref[0, hd] = (res[:, hd * HEAD_DIM:(hd + 1) * HEAD_DIM] * scale).astype(BF16)
    for a in range(6 * KV_GROUPS):
        lo = Q_DIM + a * HEAD_DIM
        kv_ref[0, a] = res[:, lo:lo + HEAD_DIM].astype(BF16)
    gs = jax.nn.sigmoid(res[:, Q_DIM + 6 * KV_DIM:])
    width = 3 * Q_PER_GROUP
    for grp in range(KV_GROUPS):
        gate_ref[0, grp] = gs[:, grp * width:(grp + 1) * width]


def _nsa_proj(h3, g, w, layer, tm=512):
    b, s, d = h3.shape
    n_in = w.shape[-1]
    return pl.pallas_call(
        _nsa_proj_body,
        grid=(b, s // tm),
        in_specs=[
            pl.BlockSpec((1, tm, d), lambda i, j: (i, j, 0)),
            pl.BlockSpec((1, d), lambda i, j: (0, 0)),
            pl.BlockSpec((None, d, n_in), lambda i, j: (layer, 0, 0)),
        ],
        out_specs=[
            pl.BlockSpec((1, N_HEADS, tm, HEAD_DIM), lambda i, j: (i, 0, j, 0)),
            pl.BlockSpec((1, 6 * KV_GROUPS, tm, HEAD_DIM), lambda i, j: (i, 0, j, 0)),
            pl.BlockSpec((1, KV_GROUPS, tm, 3 * Q_PER_GROUP), lambda i, j: (i, 0, j, 0)),
        ],
        out_shape=[
            jax.ShapeDtypeStruct((b, N_HEADS, s, HEAD_DIM), BF16),
            jax.ShapeDtypeStruct((b, 6 * KV_GROUPS, s, HEAD_DIM), BF16),
            jax.ShapeDtypeStruct((b, KV_GROUPS, s, 3 * Q_PER_GROUP), F32),
        ],
        compiler_params=_cparams("parallel", "parallel"),
        name="nsa_proj",
    )(h3, g, w)


def _compress_body(x_ref, pe_ref, w1_ref, b1_ref, w2_ref, o_ref):
    n_chunk = x_ref.shape[3]
    half = CMP_STRIDE * HEAD_DIM
    x = x_ref[0, 0].reshape(KV_GROUPS * n_chunk, half)
    top = _dot(x, w1_ref[:half, :])
    bot = _dot(x, w1_ref[half:, :])
    bot_next = pltpu.roll(bot, KV_GROUPS * n_chunk - 1, 0)
    pe = jnp.broadcast_to(pe_ref[...].astype(BF16), (8, 2 * half))
    const = _dot(pe, w1_ref[...])[0:1] + b1_ref[...]
    hid = jax.nn.gelu(top + bot_next + const).astype(BF16)
    out = _dot(hid, w2_ref[...])
    row = lax.broadcasted_iota(jnp.int32, out.shape, 0) & (n_chunk - 1)
    out = jnp.where(row < n_chunk - 1, out, 0.0)
    o_ref[0, 0] = out.reshape(KV_GROUPS, n_chunk, HEAD_DIM).astype(BF16)


def _compress(kv, pe, w1, b1, w2, layer):
    b, _, s, dh = kv.shape
    n_chunk = s // CMP_STRIDE
    x = kv.reshape(b, 6, KV_GROUPS, n_chunk, CMP_STRIDE * dh)
    hidden = w1.shape[-1]
    return pl.pallas_call(
        _compress_body,
        grid=(b, 2),
        in_specs=[
            pl.BlockSpec((1, 1, KV_GROUPS, n_chunk, CMP_STRIDE * dh), lambda i, a: (i, a, 0, 0, 0)),
            pl.BlockSpec((None, None, 1, CMP_BLOCK * dh), lambda i, a: (layer, a, 0, 0)),
            pl.BlockSpec((None, None, CMP_BLOCK * dh, hidden), lambda i, a: (layer, a, 0, 0)),
            pl.BlockSpec((None, None, 1, hidden), lambda i, a: (layer, a, 0, 0)),
            pl.BlockSpec((None, None, hidden, dh), lambda i, a: (layer, a, 0, 0)),
        ],
        out_specs=pl.BlockSpec((1, 1, KV_GROUPS, n_chunk, dh), lambda i, a: (i, a, 0, 0, 0)),
        out_shape=jax.ShapeDtypeStruct((b, 2, KV_GROUPS, n_chunk, dh), BF16),
        compiler_params=_cparams("parallel", "parallel"),
        name="nsa_compress",
    )(x, pe, w1, b1, w2)


def _t5_bucket_np(rel):
    n = np.maximum(rel, 0)
    max_exact = NUM_BUCKETS // 2
    nf = np.maximum(n, 1).astype(np.float32)
    ratio = np.log(nf / np.float32(max_exact)) / np.float32(math.log(MAX_DISTANCE / max_exact))
    large = max_exact + (ratio * np.float32(NUM_BUCKETS - max_exact)).astype(np.int32)
    large = np.minimum(large, NUM_BUCKETS - 1)
    return np.where(n < max_exact, n, large).astype(np.int32)


def _bucket_maps(s):
    n_chunk = s // CMP_STRIDE
    t = np.arange(s)[:, None]
    blk_end = np.arange(n_chunk)[None, :] * CMP_STRIDE + CMP_BLOCK - 1
    rel_c = t - blk_end
    map_c = np.where(rel_c >= 0, _t5_bucket_np(rel_c), -1).astype(np.int32)
    i = np.arange(ATT_TILE)[:, None]
    j = np.arange(ATT_TILE)[None, :]
    diag = np.where(i - j >= 0, _t5_bucket_np(i - j), -1)
    sub = _t5_bucket_np(ATT_TILE + i - j)
    edge = np.where(j > i, _t5_bucket_np(WINDOW + i - j), -1)
    map_t = np.stack([diag, sub, edge]).astype(np.int32)
    assert _t5_bucket_np(np.arange(ATT_TILE + 1, s + WINDOW)).min() == _FAR_BUCKET
    return map_c, map_t


_FAR_BUCKET = NUM_BUCKETS - 1
TILE_DIAG, TILE_SUB, TILE_EDGE = 0, 1, 2


def _bias_body(rb_ref, mc_ref, mt_ref, bc_ref, bt_ref):
    hd = pl.program_id(0)

    def lookup(bucket, shift):
        acc = jnp.zeros(bucket.shape, F32)
        for bk in range(NUM_BUCKETS):
            acc = jnp.where(bucket == bk, rb_ref[bk, hd] - shift, acc)
        return jnp.where(bucket < 0, NEG_INF, acc)

    bc_ref[0] = lookup(mc_ref[...], 0.0)
    for d in range(mt_ref.shape[0]):
        bt_ref[d, 0] = lookup(mt_ref[d], rb_ref[_FAR_BUCKET, hd])


def _bias_tables(rel_bias, s):
    map_c, map_t = _bucket_maps(s)
    n_chunk = map_c.shape[1]
    n_tab = map_t.shape[0]
    return pl.pallas_call(
        _bias_body,
        grid=(N_HEADS,),
        in_specs=[
            pl.BlockSpec(memory_space=pltpu.SMEM),
            pl.BlockSpec((s, n_chunk), lambda i: (0, 0)),
            pl.BlockSpec((n_tab, ATT_TILE, ATT_TILE), lambda i: (0, 0, 0)),
        ],
        out_specs=[
            pl.BlockSpec((1, s, n_chunk), lambda i: (i, 0, 0)),
            pl.BlockSpec((n_tab, 1, ATT_TILE, ATT_TILE), lambda i: (0, i, 0, 0)),
        ],
        out_shape=[
            jax.ShapeDtypeStruct((N_HEADS, s, n_chunk), F32),
            jax.ShapeDtypeStruct((n_tab, N_HEADS, ATT_TILE, ATT_TILE), F32),
        ],
        compiler_params=_cparams("parallel"),
        name="t5_bias_tables",
    )(rel_bias, jnp.asarray(map_c), jnp.asarray(map_t))


def _nsa_select_body(q_ref, kc_ref, vc_ref, bc_ref, ovl_ref, oc_ref, ch_ref):
    T = ATT_TILE
    R = Q_PER_GROUP
    n_sel = ovl_ref.shape[0]
    ovl = ovl_ref[...]
    j_blk = lax.broadcasted_iota(jnp.int32, (n_sel, T), 0)
    for u in range(q_ref.shape[2] // T):
        qi = pl.program_id(2) * (q_ref.shape[2] // T) + u
        rows = slice(u * T, (u + 1) * T)
        q = q_ref[0, :, rows, :].reshape(R * T, HEAD_DIM)
        s_c = _dot_nt(q, kc_ref[0, 0, 0]) + bc_ref[:, rows, :].reshape(R * T, -1)
        m_c = jnp.max(s_c, axis=-1, keepdims=True)
        p_c = jnp.exp(s_c - m_c)
        p_c = p_c * (1.0 / jnp.sum(p_c, axis=-1, keepdims=True))
        t_row = qi * T + (lax.broadcasted_iota(jnp.int32, (R * T, 1), 0) & (T - 1))
        p_c = jnp.where(t_row >= CMP_BLOCK - 1, p_c, 0.0)
        oc_ref[0, :, rows, :] = _dot(p_c.astype(BF16), vc_ref[0, 0, 0]).reshape(R, T, HEAD_DIM)

        hi, mid, lo = _split3(p_c.reshape(R, T, -1).sum(axis=0))
        imp = _dot_nt(ovl, hi) + _dot_nt(ovl, mid) + _dot_nt(ovl, lo)
        cur = (qi * T + lax.broadcasted_iota(jnp.int32, (n_sel, T), 1)) >> SEL_SHIFT
        forced = (j_blk == 0) | (j_blk == cur) | (j_blk == cur - 1)
        imp = jnp.where(forced, FORCED_SCORE, jnp.where(j_blk <= cur, imp, NEG_INF))
        rank = jnp.zeros((n_sel, T), F32)
        for i in range(n_sel):
            row = imp[i:i + 1, :]
            rank = rank + jnp.where(j_blk > i, jnp.where(row >= imp, 1.0, 0.0),
                                    jnp.where(row > imp, 1.0, 0.0))
        ch_ref[0, 0, :, rows] = jnp.where(rank < min(SEL_TOPK, n_sel), 1.0, 0.0)


def _nsa_select(q, cmp_kv, bias_c, tiles_per_step=2):
    b, _, s, dh = q.shape
    R = Q_PER_GROUP
    n_chunk = s // CMP_STRIDE
    n_sel = s // SEL_BLOCK
    tq = tiles_per_step * ATT_TILE
    c_start = np.arange(n_chunk)[None, :] * CMP_STRIDE
    j = np.arange(n_sel)[:, None]
    overlap = (c_start < (j + 1) * SEL_BLOCK) & (c_start + CMP_BLOCK > j * SEL_BLOCK)
    overlap[:, n_chunk - 1] = False
    cmp_spec = lambda a: pl.BlockSpec((1, 1, 1, n_chunk, dh), lambda i, g, t: (i, a, g, 0, 0))
    return pl.pallas_call(
        _nsa_select_body,
        grid=(b, KV_GROUPS, s // tq),
        in_specs=[
            pl.BlockSpec((1, R, tq, dh), lambda i, g, t: (i, g, t, 0)),
            cmp_spec(0), cmp_spec(1),
            pl.BlockSpec((R, tq, n_chunk), lambda i, g, t: (g, t, 0)),
            pl.BlockSpec((n_sel, n_chunk), lambda i, g, t: (0, 0)),
        ],
        out_specs=[
            pl.BlockSpec((1, R, tq, dh), lambda i, g, t: (i, g, t, 0)),
            pl.BlockSpec((1, 1, n_sel, tq), lambda i, g, t: (i, g, 0, t)),
        ],
        out_shape=[
            jax.ShapeDtypeStruct((b, N_HEADS, s, dh), F32),
            jax.ShapeDtypeStruct((b, KV_GROUPS, n_sel, s), F32),
        ],
        compiler_params=_cparams("parallel", "parallel", "parallel"),
        name="nsa_select",
    )(q, cmp_kv, cmp_kv, bias_c, jnp.asarray(overlap, BF16))


def _nsa_attn_body(q_ref, ks_ref, vs_ref, kw_ref, vw_ref, bt_ref, gate_ref, oc_ref, ch_ref, exp_ref,
                   o_ref, mask_ref, s_ref, near_ref, win_ref, peak_ref, norm_ref, acc_ref):
    T = ATT_TILE
    R = Q_PER_GROUP
    qi = pl.program_id(2)
    q = q_ref[0].reshape(R * T, HEAD_DIM)

    chosen = ch_ref[0, 0].astype(BF16)
    on_keys = lax.dot_general(chosen, exp_ref[...], (((0,), (0,)), ((), ())),
                              preferred_element_type=F32)
    mask_ref[...] = (on_keys - 1.0) * -NEG_INF

    def lane_fold(x, op):
        out = x[:, :T]
        for c in range(1, x.shape[1] // T):
            out = op(out, x[:, c * T:(c + 1) * T])
        return out

    def near_logits(k_ref, d, table, masked):
        off = pl.multiple_of(jnp.maximum(qi - d, 0) * T, T)
        s = _dot_nt(q, k_ref[0, 0, pl.ds(off, T), :])
        if table is not None:
            s = s + bt_ref[table].reshape(R * T, T)
        if masked:
            s = (s.reshape(R, T, T) + mask_ref[:, pl.ds(off, T)][None]).reshape(R * T, T)
        if d > 0:
            s = jnp.where(qi >= d, s, NEG_INF)
        return s, off

    win_tiles = [(0, TILE_DIAG), (1, TILE_SUB)]
    win_tiles += [(d, None) for d in range(2, WIN_TILES)] + [(WIN_TILES, TILE_EDGE)]
    win_offs = []
    peak = None
    for idx, (d, table) in enumerate(win_tiles):
        s, off = near_logits(kw_ref, d, table, False)
        win_ref[:, idx * T:(idx + 1) * T] = s
        peak = s if peak is None else jnp.maximum(peak, s)
        win_offs.append(off)
    m_win = jnp.max(peak, axis=-1, keepdims=True)

    near_sel = [(1, TILE_SUB), (0, TILE_DIAG)]
    sel_offs = []
    peak = None
    for idx, (d, table) in enumerate(near_sel):
        s, off = near_logits(ks_ref, d, table, True)
        near_ref[:, idx * T:(idx + 1) * T] = s
        peak = s if peak is None else jnp.maximum(peak, s)
        sel_offs.append(off)
    peak_ref[...] = peak

    C = FAR_CHUNK
    n_far = jnp.maximum(qi - 1, 0)
    n_chunks = (n_far + C // T - 1) // (C // T)
    limit = n_far * T

    def far_logits(c, carry):
        off = pl.multiple_of(c * C, C)
        pos = off + lax.broadcasted_iota(jnp.int32, (T, C), 1)
        keymask = jnp.where(pos < limit, mask_ref[:, pl.ds(off, C)], NEG_INF)
        s = _dot_nt(q, ks_ref[0, 0, pl.ds(off, C), :])
        s = (s.reshape(R, T, C) + keymask[None]).reshape(R * T, C)
        s_ref[:, pl.ds(off, C)] = s
        peak_ref[...] = jnp.maximum(peak_ref[...], lane_fold(s, jnp.maximum))
        return carry

    lax.fori_loop(0, n_chunks, far_logits, 0)
    m_sel = jnp.max(peak_ref[...], axis=-1, keepdims=True)

    def values(logit_ref, v_ref, offs, m):
        norm = acc = None
        for idx, off in enumerate(offs):
            p = jnp.exp(logit_ref[:, idx * T:(idx + 1) * T] - m)
            pv = _dot(p.astype(BF16), v_ref[0, 0, pl.ds(off, T), :])
            norm = p if norm is None else norm + p
            acc = pv if acc is None else acc + pv
        return norm, acc

    norm, acc = values(win_ref, vw_ref, win_offs, m_win)
    o_win = acc * (1.0 / jnp.sum(norm, axis=-1, keepdims=True))
    norm, acc = values(near_ref, vs_ref, sel_offs, m_sel)
    norm_ref[...] = norm
    acc_ref[...] = acc

    def far_values(c, carry):
        off = pl.multiple_of(c * C, C)
        p = jnp.exp(s_ref[:, pl.ds(off, C)] - m_sel)
        norm_ref[...] += lane_fold(p, jnp.add)
        acc_ref[...] += _dot(p.astype(BF16), vs_ref[0, 0, pl.ds(off, C), :])
        return carry

    lax.fori_loop(0, n_chunks, far_values, 0)
    o_sel = acc_ref[...] * (1.0 / jnp.sum(norm_ref[...], axis=-1, keepdims=True))

    gate = gate_ref[0, 0]
    outs = []
    for r in range(R):
        rows = slice(r * T, (r + 1) * T)
        outs.append(gate[:, r:r + 1] * oc_ref[0, r]
                    + gate[:, R + r:R + r + 1] * o_sel[rows]
                    + gate[:, 2 * R + r:2 * R + r + 1] * o_win[rows])
    o_ref[0] = jnp.concatenate(outs, axis=-1).astype(BF16)


def _nsa_attn(q, kv, o_cmp, chosen, bias_t, gates):
    b, _, s, dh = q.shape
    T = ATT_TILE
    R = Q_PER_GROUP
    n_sel = s // SEL_BLOCK
    expand = np.arange(s)[None, :] // SEL_BLOCK == np.arange(n_sel)[:, None]
    kv_spec = lambda a: pl.BlockSpec((1, 1, s, dh), lambda i, g, t: (i, a * KV_GROUPS + g, 0, 0))
    return pl.pallas_call(
        _nsa_attn_body,
        grid=(b, KV_GROUPS, s // T),
        in_specs=[
            pl.BlockSpec((1, R, T, dh), lambda i, g, t: (i, g, t, 0)),
            kv_spec(2), kv_spec(3), kv_spec(4), kv_spec(5),
            pl.BlockSpec((bias_t.shape[0], R, T, T), lambda i, g, t: (0, g, 0, 0)),
            pl.BlockSpec((1, 1, T, 3 * R), lambda i, g, t: (i, g, t, 0)),
            pl.BlockSpec((1, R, T, dh), lambda i, g, t: (i, g, t, 0)),
            pl.BlockSpec((1, 1, n_sel, T), lambda i, g, t: (i, g, 0, t)),
            pl.BlockSpec((n_sel, s), lambda i, g, t: (0, 0)),
        ],
        out_specs=pl.BlockSpec((1, T, R * dh), lambda i, g, t: (i, t, g)),
        out_shape=jax.ShapeDtypeStruct((b, s, Q_DIM), BF16),
        scratch_shapes=[
            pltpu.VMEM((T, s), F32),
            pltpu.VMEM((R * T, s), F32),
            pltpu.VMEM((R * T, 2 * T), F32),
            pltpu.VMEM((R * T, (WIN_TILES + 1) * T), F32),
            pltpu.VMEM((R * T, T), F32),
            pltpu.VMEM((R * T, T), F32),
            pltpu.VMEM((R * T, dh), F32),
        ],
        compiler_params=_cparams("parallel", "parallel", "arbitrary"),
        name="nsa_attn",
    )(q, kv, kv, kv, kv, bias_t, gates, o_cmp, chosen, jnp.asarray(expand, BF16))


def _fox_proj_body(h_ref, g_ref, w_ref, bf_ref, q_ref, k_ref, v_ref, cum_ref, carry_ref):
    tm = h_ref.shape[1]

    @pl.when(pl.program_id(1) == 0)
    def _():
        carry_ref[...] = jnp.zeros_like(carry_ref)

    xn = _rms(h_ref[0], g_ref[...]).astype(BF16)
    res = _dot(xn, w_ref[...])
    scale = HEAD_DIM ** -0.5
    for hd in range(N_HEADS):
        cols = slice(hd * HEAD_DIM, (hd + 1) * HEAD_DIM)
        q_ref[0, hd] = (res[:, cols] * scale).astype(BF16)
        k_ref[0, hd] = res[:, Q_DIM + hd * HEAD_DIM:Q_DIM + (hd + 1) * HEAD_DIM].astype(BF16)
        v_ref[0, hd] = res[:, 2 * Q_DIM + hd * HEAD_DIM:2 * Q_DIM + (hd + 1) * HEAD_DIM].astype(BF16)
    log_f = jax.nn.log_sigmoid(res[:, 3 * Q_DIM:] + bf_ref[...])
    tri = jnp.where(lax.broadcasted_iota(jnp.int32, (tm, tm), 0)
                    >= lax.broadcasted_iota(jnp.int32, (tm, tm), 1), 1.0, 0.0).astype(BF16)
    hi, mid, lo = _split3(log_f)
    cum = _dot(tri, hi) + _dot(tri, mid) + _dot(tri, lo) + carry_ref[...]
    cum_ref[0] = cum
    carry_ref[...] = cum[tm - 1:tm]


def _fox_proj(h3, g, w, b_f, layer, tm=512):
    b, s, d = h3.shape
    n_in = w.shape[-1]
    head_spec = pl.BlockSpec((1, N_HEADS, tm, HEAD_DIM), lambda i, j: (i, 0, j, 0))
    head_shape = jax.ShapeDtypeStruct((b, N_HEADS, s, HEAD_DIM), BF16)
    return pl.pallas_call(
        _fox_proj_body,
        grid=(b, s // tm),
        in_specs=[
            pl.BlockSpec((1, tm, d), lambda i, j: (i, j, 0)),
            pl.BlockSpec((1, d), lambda i, j: (0, 0)),
            pl.BlockSpec((None, d, n_in), lambda i, j: (layer, 0, 0)),
            pl.BlockSpec((None, 1, N_HEADS), lambda i, j: (layer, 0, 0)),
        ],
        out_specs=[head_spec, head_spec, head_spec,
                   pl.BlockSpec((1, tm, N_HEADS), lambda i, j: (i, j, 0))],
        out_shape=[head_shape, head_shape, head_shape,
                   jax.ShapeDtypeStruct((b, s, N_HEADS), F32)],
        scratch_shapes=[pltpu.VMEM((1, N_HEADS), F32)],
        compiler_params=_cparams("parallel", "arbitrary"),
        name="fox_proj",
    )(h3, g, w, b_f)


def _fox_attn_body(q_ref, k_ref, v_ref, nc_ref, o_ref, s_ref, p_ref):
    T = FOX_TILE
    n_q = q_ref.shape[2] // T
    causal = (lax.broadcasted_iota(jnp.int32, (T, T), 0)
              >= lax.broadcasted_iota(jnp.int32, (T, T), 1))
    for qi in range(n_q):
        slot = qi % 2
        rows = slice(qi * T, (qi + 1) * T)
        outs = []
        for hh in range(q_ref.shape[1]):
            q = q_ref[0, hh, rows, :]
            peak = None
            for kt in range(qi + 1):
                cols = slice(kt * T, (kt + 1) * T)
                s = _dot_nt(q, k_ref[0, hh, cols, :]) + nc_ref[0, hh, :, cols]
                if kt == qi:
                    s = jnp.where(causal, s, NEG_INF)
                s_ref[slot, hh, :, cols] = s
                part = jnp.maximum(s[:, :T // 2], s[:, T // 2:])
                peak = part if peak is None else jnp.maximum(peak, part)
            m = jnp.max(peak, axis=-1, keepdims=True)
            norm = None
            for kt in range(qi + 1):
                cols = slice(kt * T, (kt + 1) * T)
                p = jnp.exp(s_ref[slot, hh, :, cols] - m)
                p_ref[slot, hh, :, cols] = p.astype(BF16)
                part = p[:, :T // 2] + p[:, T // 2:]
                norm = part if norm is None else norm + part
            extent = (qi + 1) * T
            acc = _dot(p_ref[slot, hh, :, :extent], v_ref[0, hh, :extent, :])
            outs.append(acc * (1.0 / jnp.sum(norm, axis=-1, keepdims=True)))
        o_ref[0, rows, :] = jnp.concatenate(outs, axis=-1).astype(BF16)


def _fox_attn(q, k, v, neg_cum, heads_per_step=2):
    b, nh, s, dh = q.shape
    T = FOX_TILE
    hp = heads_per_step
    full = pl.BlockSpec((1, hp, s, dh), lambda i, h: (i, h, 0, 0))
    return pl.pallas_call(
        _fox_attn_body,
        grid=(b, nh // hp),
        in_specs=[full, full, full, pl.BlockSpec((1, hp, 1, s), lambda i, h: (i, h, 0, 0))],
        out_specs=pl.BlockSpec((1, s, hp * dh), lambda i, h: (i, 0, h)),
        out_shape=jax.ShapeDtypeStruct((b, s, nh * dh), BF16),
        scratch_shapes=[
            pltpu.VMEM((2, hp, T, s), F32),
            pltpu.VMEM((2, hp, T, s), BF16),
        ],
        compiler_params=_cparams("parallel", "parallel"),
        name="fox_attn",
    )(q, k, v, neg_cum)


def kernel(x, norm_g, ffn_w_gate, ffn_w_up, ffn_w_down, rel_bias, nsa_w_in, nsa_cmp_pe, nsa_cmp_w1,
           nsa_cmp_b1, nsa_cmp_w2, nsa_w_out, fox_w_in, fox_b_f, fox_w_out):
    b, s, d = x.shape
    depth = norm_g.shape[0]
    n = b * s
    wg, wu, wd = ffn_w_gate.astype(BF16), ffn_w_up.astype(BF16), ffn_w_down.astype(BF16)
    gate_cols = (Q_DIM + 6 * KV_DIM
                 + np.arange(N_GATES).reshape(3, KV_GROUPS, Q_PER_GROUP).transpose(1, 0, 2).reshape(-1))
    col_order = np.concatenate([np.arange(Q_DIM + 6 * KV_DIM), gate_cols])
    nsa_w_in_b, nsa_w_out_b = nsa_w_in[:, :, col_order].astype(BF16), nsa_w_out.astype(BF16)
    fox_w_in_b, fox_w_out_b = fox_w_in.astype(BF16), fox_w_out.astype(BF16)
    cmp_w1_b, cmp_w2_b = nsa_cmp_w1.astype(BF16), nsa_cmp_w2.astype(BF16)
    cmp_pe = nsa_cmp_pe.reshape(nsa_cmp_pe.shape[0], 2, 1, CMP_BLOCK * HEAD_DIM)
    cmp_b1 = nsa_cmp_b1[:, :, None, :]
    fox_bf = fox_b_f[:, None, :]
    gains = norm_g[:, :, None, :]

    bias_c, bias_t = _bias_tables(rel_bias, s)

    h = x.reshape(n, d)
    for i in range(depth):
        g = gains[i]
        j = i // 2
        h = _ffn(h, g[0], g[1], wg, wu, wd, i, 0)
        if i % 2 == 0:
            q, kv, gates = _nsa_proj(h.reshape(b, s, d), g[2], nsa_w_in_b, j)
            cmp_kv = _compress(kv, cmp_pe, cmp_w1_b, cmp_b1, cmp_w2_b, j)
            o_cmp, chosen = _nsa_select(q, cmp_kv, bias_c)
            o = _nsa_attn(q, kv, o_cmp, chosen, bias_t, gates).reshape(n, Q_DIM)
            h = _outproj(o, nsa_w_out_b, h, g[3], j)
        else:
            q, k, v, cum = _fox_proj(h.reshape(b, s, d), g[2], fox_w_in_b, fox_bf, j)
            neg_cum = (-cum).transpose(0, 2, 1)[:, :, None, :]
            o = _fox_attn(q, k, v, neg_cum).reshape(n, Q_DIM)
            h = _outproj(o, fox_w_out_b, h, g[3], j)
        h = _ffn(h, g[4], g[5], wg, wu, wd, i, 1)
    return h.reshape(b, s, d)
```

```python
import math

import numpy as np
import jax
import jax.numpy as jnp
from jax import lax
from jax.experimental import pallas as pl
from jax.experimental.pallas import tpu as pltpu

N_HEADS = 16
HEAD_DIM = 64
KV_GROUPS = 4
Q_PER_GROUP = N_HEADS // KV_GROUPS
CMP_BLOCK = 32
CMP_STRIDE = 16
SEL_BLOCK = 64
SEL_SHIFT = 6
SEL_TOPK = 16
WINDOW = 512
NUM_BUCKETS = 32
MAX_DISTANCE = 128
RMS_EPS = 1e-6
NEG_INF = -1e30
FORCED_SCORE = 1e9
Q_DIM = N_HEADS * HEAD_DIM
KV_DIM = KV_GROUPS * HEAD_DIM
N_GATES = 3 * N_HEADS
LOG2E = math.log2(math.e)
Q_SCALE = HEAD_DIM ** -0.5 * LOG2E

ATT_TILE = 256
WIN_TILES = WINDOW // ATT_TILE
FAR_CHUNK = 512
GROUPS_PER_STEP = 2
FOX_TILE = 256
VMEM_LIMIT = 56 * 1024 * 1024

BF16 = jnp.bfloat16
F32 = jnp.float32


def _cparams(*sem):
    return pltpu.CompilerParams(dimension_semantics=sem, vmem_limit_bytes=VMEM_LIMIT)


def _rms(x, g):
    return x * lax.rsqrt(jnp.mean(x * x, axis=-1, keepdims=True) + RMS_EPS) * g


def _dot(a, b):
    return jnp.dot(a, b, preferred_element_type=F32)


def _dot_nt(a, b):
    return lax.dot_general(a, b, (((1,), (1,)), ((), ())), preferred_element_type=F32)


def _split3(x):
    hi = x.astype(BF16)
    r1 = x - hi.astype(F32)
    mid = r1.astype(BF16)
    lo = (r1 - mid.astype(F32)).astype(BF16)
    return hi, mid, lo


def _ffn_body(h_ref, gpre_ref, gpost_ref, wg_ref, wu_ref, wd_ref, o_ref, xn_ref, acc_ref):
    k = pl.program_id(1)

    @pl.when(k == 0)
    def _():
        xn_ref[...] = _rms(h_ref[...], gpre_ref[...]).astype(BF16)
        acc_ref[...] = jnp.zeros_like(acc_ref)

    xn = xn_ref[...]
    g = _dot(xn, wg_ref[...])
    u = _dot(xn, wu_ref[...])
    a = (g * jax.nn.sigmoid(g) * u).astype(BF16)
    acc_ref[...] += _dot(a, wd_ref[...])

    @pl.when(k == pl.num_programs(1) - 1)
    def _():
        o_ref[...] = h_ref[...] + 0.5 * _rms(acc_ref[...], gpost_ref[...])


def _ffn(h, g_pre, g_post, wg, wu, wd, layer, half, tm=1024, tf=1408):
    n, d = h.shape
    f = wg.shape[-1]
    row = lambda i, k: (i, 0)
    vec = lambda i, k: (0, 0)
    return pl.pallas_call(
        _ffn_body,
        grid=(n // tm, f // tf),
        in_specs=[
            pl.BlockSpec((tm, d), row),
            pl.BlockSpec((1, d), vec),
            pl.BlockSpec((1, d), vec),
            pl.BlockSpec((None, None, d, tf), lambda i, k: (layer, half, 0, k)),
            pl.BlockSpec((None, None, d, tf), lambda i, k: (layer, half, 0, k)),
            pl.BlockSpec((None, None, tf, d), lambda i, k: (layer, half, k, 0)),
        ],
        out_specs=pl.BlockSpec((tm, d), row),
        out_shape=jax.ShapeDtypeStruct((n, d), F32),
        scratch_shapes=[pltpu.VMEM((tm, d), BF16), pltpu.VMEM((tm, d), F32)],
        compiler_params=_cparams("parallel", "arbitrary"),
        name="ffn",
    )(h, g_pre, g_post, wg, wu, wd)


def _outproj_body(o_ref, w_ref, h_ref, g_ref, out_ref):
    y = _dot(o_ref[...], w_ref[...])
    out_ref[...] = h_ref[...] + _rms(y, g_ref[...])


def _outproj(o, w, h, g, layer, tm=512):
    n, d = h.shape
    kdim = o.shape[-1]
    return pl.pallas_call(
        _outproj_body,
        grid=(n // tm,),
        in_specs=[
            pl.BlockSpec((tm, kdim), lambda i: (i, 0)),
            pl.BlockSpec((None, kdim, d), lambda i: (layer, 0, 0)),
            pl.BlockSpec((tm, d), lambda i: (i, 0)),
            pl.BlockSpec((1, d), lambda i: (0, 0)),
        ],
        out_specs=pl.BlockSpec((tm, d), lambda i: (i, 0)),
        out_shape=jax.ShapeDtypeStruct((n, d), F32),
        compiler_params=_cparams("parallel"),
        name="outproj",
    )(o, w, h, g)


def _nsa_proj_body(h_ref, g_ref, w_ref, wgate_ref, q_ref, kv_ref, gate_ref):
    xn = _rms(h_ref[0], g_ref[...]).astype(BF16)
    res = _dot(xn, w_ref[...])
    for hd in range(N_HEADS):
        q_ref[0, hd] = (res[:, hd * HEAD_DIM:(hd + 1) * HEAD_DIM] * Q_SCALE).astype(BF16)
    for a in range(6 * KV_GROUPS):
        lo = Q_DIM + a * HEAD_DIM
        kv_ref[0, a] = res[:, lo:lo + HEAD_DIM].astype(BF16)
    gs = jax.nn.sigmoid(_dot(xn, wgate_ref[...]))
    width = 3 * Q_PER_GROUP
    for grp in range(KV_GROUPS):
        gate_ref[0, grp] = gs[:, grp * width:(grp + 1) * width]


def _nsa_proj(h3, g, w, w_gate, layer, tm=512):
    b, s, d = h3.shape
    return pl.pallas_call(
        _nsa_proj_body,
        grid=(b, s // tm),
        in_specs=[
            pl.BlockSpec((1, tm, d), lambda i, j: (i, j, 0)),
            pl.BlockSpec((1, d), lambda i, j: (0, 0)),
            pl.BlockSpec((None, d, w.shape[-1]), lambda i, j: (layer, 0, 0)),
            pl.BlockSpec((None, d, N_GATES), lambda i, j: (layer, 0, 0)),
        ],
        out_specs=[
            pl.BlockSpec((1, N_HEADS, tm, HEAD_DIM), lambda i, j: (i, 0, j, 0)),
            pl.BlockSpec((1, 6 * KV_GROUPS, tm, HEAD_DIM), lambda i, j: (i, 0, j, 0)),
            pl.BlockSpec((1, KV_GROUPS, tm, 3 * Q_PER_GROUP), lambda i, j: (i, 0, j, 0)),
        ],
        out_shape=[
            jax.ShapeDtypeStruct((b, N_HEADS, s, HEAD_DIM), BF16),
            jax.ShapeDtypeStruct((b, 6 * KV_GROUPS, s, HEAD_DIM), BF16),
            jax.ShapeDtypeStruct((b, KV_GROUPS, s, 3 * Q_PER_GROUP), F32),
        ],
        compiler_params=_cparams("parallel", "parallel"),
        name="nsa_proj",
    )(h3, g, w, w_gate)


def _compress_body(x_ref, pe_ref, w1_ref, b1_ref, w2_ref, o_ref):
    n_chunk = x_ref.shape[3]
    half = CMP_STRIDE * HEAD_DIM
    x = x_ref[0, 0].reshape(KV_GROUPS * n_chunk, half)
    top = _dot(x, w1_ref[:half, :])
    bot = _dot(x, w1_ref[half:, :])
    bot_next = pltpu.roll(bot, KV_GROUPS * n_chunk - 1, 0)
    pe = jnp.broadcast_to(pe_ref[...].astype(BF16), (8, 2 * half))
    const = _dot(pe, w1_ref[...])[0:1] + b1_ref[...]
    hid = jax.nn.gelu(top + bot_next + const).astype(BF16)
    out = _dot(hid, w2_ref[...])
    row = lax.broadcasted_iota(jnp.int32, out.shape, 0) & (n_chunk - 1)
    out = jnp.where(row < n_chunk - 1, out, 0.0)
    o_ref[0, 0] = out.reshape(KV_GROUPS, n_chunk, HEAD_DIM).astype(BF16)


def _compress(kv, pe, w1, b1, w2, layer):
    b, _, s, dh = kv.shape
    n_chunk = s // CMP_STRIDE
    x = kv[:, :2 * KV_GROUPS].reshape(b, 2, KV_GROUPS, n_chunk, CMP_STRIDE * dh)
    hidden = w1.shape[-1]
    return pl.pallas_call(
        _compress_body,
        grid=(b, 2),
        in_specs=[
            pl.BlockSpec((1, 1, KV_GROUPS, n_chunk, CMP_STRIDE * dh), lambda i, a: (i, a, 0, 0, 0)),
            pl.BlockSpec((None, None, 1, CMP_BLOCK * dh), lambda i, a: (layer, a, 0, 0)),
            pl.BlockSpec((None, None, CMP_BLOCK * dh, hidden), lambda i, a: (layer, a, 0, 0)),
            pl.BlockSpec((None, None, 1, hidden), lambda i, a: (layer, a, 0, 0)),
            pl.BlockSpec((None, None, hidden, dh), lambda i, a: (layer, a, 0, 0)),
        ],
        out_specs=pl.BlockSpec((1, 1, KV_GROUPS, n_chunk, dh), lambda i, a: (i, a, 0, 0, 0)),
        out_shape=jax.ShapeDtypeStruct((b, 2, KV_GROUPS, n_chunk, dh), BF16),
        compiler_params=_cparams("parallel", "parallel"),
        name="nsa_compress",
    )(x, pe, w1, b1, w2)


def _t5_bucket_np(rel):
    n = np.maximum(rel, 0)
    max_exact = NUM_BUCKETS // 2
    nf = np.maximum(n, 1).astype(np.float32)
    ratio = np.log(nf / np.float32(max_exact)) / np.float32(math.log(MAX_DISTANCE / max_exact))
    large = max_exact + (ratio * np.float32(NUM_BUCKETS - max_exact)).astype(np.int32)
    large = np.minimum(large, NUM_BUCKETS - 1)
    return np.where(n < max_exact, n, large).astype(np.int32)


def _bucket_maps(s):
    n_chunk = s // CMP_STRIDE
    t = np.arange(s)[:, None]
    blk_end = np.arange(n_chunk)[None, :] * CMP_STRIDE + CMP_BLOCK - 1
    rel_c = t - blk_end
    map_c = np.where(rel_c >= 0, _t5_bucket_np(rel_c), -1).astype(np.int32)
    i = np.arange(ATT_TILE)[:, None]
    j = np.arange(ATT_TILE)[None, :]
    diag = np.where(i - j >= 0, _t5_bucket_np(i - j), -1)
    sub = _t5_bucket_np(ATT_TILE + i - j)
    edge = np.where(j > i, _t5_bucket_np(WINDOW + i - j), -1)
    map_t = np.stack([diag, sub, edge]).astype(np.int32)
    assert _t5_bucket_np(np.arange(ATT_TILE + 1, s + WINDOW)).min() == _FAR_BUCKET
    return map_c, map_t


_FAR_BUCKET = NUM_BUCKETS - 1
TILE_DIAG, TILE_SUB, TILE_EDGE = 0, 1, 2


def _bias_body(rb_ref, mc_ref, mt_ref, bc_ref, bt_ref):
    hd = pl.program_id(0)

    def lookup(bucket, shift):
        acc = jnp.zeros(bucket.shape, F32)
        for bk in range(NUM_BUCKETS):
            acc = jnp.where(bucket == bk, (rb_ref[bk, hd] - shift) * LOG2E, acc)
        return jnp.where(bucket < 0, NEG_INF, acc)

    bc_ref[0] = lookup(mc_ref[...], 0.0)
    for d in range(mt_ref.shape[0]):
        bt_ref[d, 0] = lookup(mt_ref[d], rb_ref[_FAR_BUCKET, hd])


def _bias_tables(rel_bias, s):
    map_c, map_t = _bucket_maps(s)
    n_chunk = map_c.shape[1]
    n_tab = map_t.shape[0]
    return pl.pallas_call(
        _bias_body,
        grid=(N_HEADS,),
        in_specs=[
            pl.BlockSpec(memory_space=pltpu.SMEM),
            pl.BlockSpec((s, n_chunk), lambda i: (0, 0)),
            pl.BlockSpec((n_tab, ATT_TILE, ATT_TILE), lambda i: (0, 0, 0)),
        ],
        out_specs=[
            pl.BlockSpec((1, s, n_chunk), lambda i: (i, 0, 0)),
            pl.BlockSpec((n_tab, 1, ATT_TILE, ATT_TILE), lambda i: (0, i, 0, 0)),
        ],
        out_shape=[
            jax.ShapeDtypeStruct((N_HEADS, s, n_chunk), F32),
            jax.ShapeDtypeStruct((n_tab, N_HEADS, ATT_TILE, ATT_TILE), F32),
        ],
        compiler_params=_cparams("parallel"),
        name="t5_bias_tables",
    )(rel_bias, jnp.asarray(map_c), jnp.asarray(map_t))


def _nsa_select_body(q_ref, kc_ref, vc_ref, bc_ref, ovl_ref, oc_ref, ch_ref):
    T = ATT_TILE
    R = Q_PER_GROUP
    n_sel = ovl_ref.shape[0]
    ovl = ovl_ref[...]
    j_blk = lax.broadcasted_iota(jnp.int32, (n_sel, T), 0)
    for u in range(q_ref.shape[2] // T):
        qi = pl.program_id(2) * (q_ref.shape[2] // T) + u
        rows = slice(u * T, (u + 1) * T)
        q = q_ref[0, :, rows, :].reshape(R * T, HEAD_DIM)
        s_c = _dot_nt(q, kc_ref[0, 0, 0]) + bc_ref[:, rows, :].reshape(R * T, -1)
        m_c = jnp.max(s_c, axis=-1, keepdims=True)
        p_c = jnp.exp2(s_c - m_c)
        p_c = p_c * (1.0 / jnp.sum(p_c, axis=-1, keepdims=True))
        t_row = qi * T + (lax.broadcasted_iota(jnp.int32, (R * T, 1), 0) & (T - 1))
        p_c = jnp.where(t_row >= CMP_BLOCK - 1, p_c, 0.0)
        oc_ref[0, :, rows, :] = _dot(p_c.astype(BF16), vc_ref[0, 0, 0]).reshape(R, T, HEAD_DIM)

        hi, mid, lo = _split3(p_c.reshape(R, T, -1).sum(axis=0))
        imp = _dot_nt(ovl, hi) + _dot_nt(ovl, mid) + _dot_nt(ovl, lo)
        cur = (qi * T + lax.broadcasted_iota(jnp.int32, (n_sel, T), 1)) >> SEL_SHIFT
        forced = (j_blk == 0) | (j_blk == cur) | (j_blk == cur - 1)
        imp = jnp.where(forced, FORCED_SCORE, jnp.where(j_blk <= cur, imp, NEG_INF))
        rank = jnp.zeros((n_sel, T), F32)
        for i in range(n_sel):
            row = imp[i:i + 1, :]
            rank = rank + jnp.where(j_blk > i, jnp.where(row >= imp, 1.0, 0.0),
                                    jnp.where(row > imp, 1.0, 0.0))
        ch_ref[0, 0, :, rows] = jnp.where(rank < min(SEL_TOPK, n_sel), 1.0, 0.0)


def _nsa_select(q, cmp_kv, bias_c, tiles_per_step=2):
    b, _, s, dh = q.shape
    R = Q_PER_GROUP
    n_chunk = s // CMP_STRIDE
    n_sel = s // SEL_BLOCK
    tq = tiles_per_step * ATT_TILE
    c_start = np.arange(n_chunk)[None, :] * CMP_STRIDE
    j = np.arange(n_sel)[:, None]
    overlap = (c_start < (j + 1) * SEL_BLOCK) & (c_start + CMP_BLOCK > j * SEL_BLOCK)
    overlap[:, n_chunk - 1] = False
    cmp_spec = lambda a: pl.BlockSpec((1, 1, 1, n_chunk, dh), lambda i, g, t: (i, a, g, 0, 0))
    return pl.pallas_call(
        _nsa_select_body,
        grid=(b, KV_GROUPS, s // tq),
        in_specs=[
            pl.BlockSpec((1, R, tq, dh), lambda i, g, t: (i, g, t, 0)),
            cmp_spec(0), cmp_spec(1),
            pl.BlockSpec((R, tq, n_chunk), lambda i, g, t: (g, t, 0)),
            pl.BlockSpec((n_sel, n_chunk), lambda i, g, t: (0, 0)),
        ],
        out_specs=[
            pl.BlockSpec((1, R, tq, dh), lambda i, g, t: (i, g, t, 0)),
            pl.BlockSpec((1, 1, n_sel, tq), lambda i, g, t: (i, g, 0, t)),
        ],
        out_shape=[
            jax.ShapeDtypeStruct((b, N_HEADS, s, dh), F32),
            jax.ShapeDtypeStruct((b, KV_GROUPS, n_sel, s), F32),
        ],
        compiler_params=_cparams("parallel", "parallel", "parallel"),
        name="nsa_select",
    )(q, cmp_kv, cmp_kv, bias_c, jnp.asarray(overlap, BF16))


def _nsa_attn_body(q_ref, ks_ref, vs_ref, kw_ref, vw_ref, bt_ref, gate_ref, oc_ref, ch_ref, exp_ref,
                   o_ref, mask_ref, s_ref, near_ref, win_ref, peak_ref, norm_ref, acc_ref):
    T = ATT_TILE
    R = Q_PER_GROUP
    C = FAR_CHUNK
    groups = range(ks_ref.shape[1])
    qi = pl.program_id(2)
    n_far = jnp.maximum(qi - 1, 0)
    n_chunks = (n_far + C // T - 1) // (C // T)
    limit = n_far * T
    qs = [q_ref[0, g * R:(g + 1) * R].reshape(R * T, HEAD_DIM) for g in groups]

    def lane_fold(x, op):
        out = x[:, :T]
        for c in range(1, x.shape[1] // T):
            out = op(out, x[:, c * T:(c + 1) * T])
        return out

    def near_logits(g, k_ref, d, table, masked):
        off = pl.multiple_of(jnp.maximum(qi - d, 0) * T, T)
        s = _dot_nt(qs[g], k_ref[0, g, pl.ds(off, T), :])
        if table is not None:
            s = s + bt_ref[table, g * R:(g + 1) * R].reshape(R * T, T)
        if masked:
            s = (s.reshape(R, T, T) + mask_ref[g, :, pl.ds(off, T)][None]).reshape(R * T, T)
        if d > 0:
            s = jnp.where(qi >= d, s, NEG_INF)
        return s, off

    def logits_pass(g, k_ref, tiles, masked, out_ref):
        offs = []
        peak = None
        for idx, (d, table) in enumerate(tiles):
            s, off = near_logits(g, k_ref, d, table, masked)
            out_ref[g, :, idx * T:(idx + 1) * T] = s
            peak = s if peak is None else jnp.maximum(peak, s)
            offs.append(off)
        return offs, peak

    def values_pass(g, logit_ref, v_ref, offs, m):
        norm = acc = None
        for idx, off in enumerate(offs):
            p = jnp.exp2(logit_ref[g, :, idx * T:(idx + 1) * T] - m)
            pv = _dot(p.astype(BF16), v_ref[0, g, pl.ds(off, T), :])
            norm = p if norm is None else norm + p
            acc = pv if acc is None else acc + pv
        return norm, acc

    win_tiles = [(0, TILE_DIAG), (1, TILE_SUB)]
    win_tiles += [(d, None) for d in range(2, WIN_TILES)] + [(WIN_TILES, TILE_EDGE)]
    near_sel = [(1, TILE_SUB), (0, TILE_DIAG)]
    win_offs, m_win, sel_offs = [], [], []
    for g in groups:
        on_keys = lax.dot_general(ch_ref[0, g].astype(BF16), exp_ref[...], (((0,), (0,)), ((), ())),
                                  preferred_element_type=F32)
        mask_ref[g] = (on_keys - 1.0) * -NEG_INF
        offs, peak = logits_pass(g, kw_ref, win_tiles, False, win_ref)
        win_offs.append(offs)
        m_win.append(jnp.max(peak, axis=-1, keepdims=True))
        offs, peak = logits_pass(g, ks_ref, near_sel, True, near_ref)
        sel_offs.append(offs)
        peak_ref[g] = peak

    def far_logits(c, carry):
        off = pl.multiple_of(c * C, C)
        pos = off + lax.broadcasted_iota(jnp.int32, (T, C), 1)
        for g in groups:
            keymask = jnp.where(pos < limit, mask_ref[g, :, pl.ds(off, C)], NEG_INF)
            s = _dot_nt(qs[g], ks_ref[0, g, pl.ds(off, C), :])
            s = (s.reshape(R, T, C) + keymask[None]).reshape(R * T, C)
            s_ref[g, :, pl.ds(off, C)] = s
            peak_ref[g] = jnp.maximum(peak_ref[g], lane_fold(s, jnp.maximum))
        return carry

    lax.fori_loop(0, n_chunks, far_logits, 0)

    m_sel = []
    for g in groups:
        m_sel.append(jnp.max(peak_ref[g], axis=-1, keepdims=True))
        norm, acc = values_pass(g, near_ref, vs_ref, sel_offs[g], m_sel[g])
        norm_ref[g] = norm
        acc_ref[g] = acc

    def far_values(c, carry):
        off = pl.multiple_of(c * C, C)
        for g in groups:
            p = jnp.exp2(s_ref[g, :, pl.ds(off, C)] - m_sel[g])
            norm_ref[g] += lane_fold(p, jnp.add)
            acc_ref[g] += _dot(p.astype(BF16), vs_ref[0, g, pl.ds(off, C), :])
        return carry

    lax.fori_loop(0, n_chunks, far_values, 0)

    outs = []
    for g in groups:
        gate = gate_ref[0, g]
        gate_of = lambda br: jnp.concatenate(
            [gate[:, br * R + r:br * R + r + 1] for r in range(R)], axis=0)
        norm, acc = values_pass(g, win_ref, vw_ref, win_offs[g], m_win[g])
        w_win = gate_of(2) * (1.0 / jnp.sum(norm, axis=-1, keepdims=True))
        w_sel = gate_of(1) * (1.0 / jnp.sum(norm_ref[g], axis=-1, keepdims=True))
        o = (gate_of(0) * oc_ref[0, g * R:(g + 1) * R].reshape(R * T, HEAD_DIM)
             + w_sel * acc_ref[g] + w_win * acc)
        outs += [o[r * T:(r + 1) * T] for r in range(R)]
    o_ref[0] = jnp.concatenate(outs, axis=-1).astype(BF16)


def _nsa_attn(q, kv, o_cmp, chosen, bias_t, gates):
    b, _, s, dh = q.shape
    T = ATT_TILE
    R = Q_PER_GROUP
    gp = GROUPS_PER_STEP
    n_sel = s // SEL_BLOCK
    far_cols = max(s - 2 * T, FAR_CHUNK)
    expand = np.arange(s)[None, :] // SEL_BLOCK == np.arange(n_sel)[:, None]
    once = dict(pipeline_mode=pl.Buffered(1))
    kv_spec = lambda a: pl.BlockSpec((1, gp, s, dh), lambda i, g, t: (i, a * (KV_GROUPS // gp) + g, 0, 0))
    return pl.pallas_call(
        _nsa_attn_body,
        grid=(b, KV_GROUPS // gp, s // T),
        in_specs=[
            pl.BlockSpec((1, gp * R, T, dh), lambda i, g, t: (i, g, t, 0)),
            kv_spec(2), kv_spec(3), kv_spec(4), kv_spec(5),
            pl.BlockSpec((bias_t.shape[0], gp * R, T, T), lambda i, g, t: (0, g, 0, 0), **once),
            pl.BlockSpec((1, gp, T, 3 * R), lambda i, g, t: (i, g, t, 0)),
            pl.BlockSpec((1, gp * R, T, dh), lambda i, g, t: (i, g, t, 0)),
            pl.BlockSpec((1, gp, n_sel, T), lambda i, g, t: (i, g, 0, t)),
            pl.BlockSpec((n_sel, s), lambda i, g, t: (0, 0), **once),
        ],
        out_specs=pl.BlockSpec((1, T, gp * R * dh), lambda i, g, t: (i, t, g)),
        out_shape=jax.ShapeDtypeStruct((b, s, Q_DIM), BF16),
        scratch_shapes=[
            pltpu.VMEM((gp, T, s), F32),
            pltpu.VMEM((gp, R * T, far_cols), F32),
            pltpu.VMEM((gp, R * T, 2 * T), F32),
            pltpu.VMEM((gp, R * T, (WIN_TILES + 1) * T), F32),
            pltpu.VMEM((gp, R * T, T), F32),
            pltpu.VMEM((gp, R * T, T), F32),
            pltpu.VMEM((gp, R * T, dh), F32),
        ],
        compiler_params=_cparams("parallel", "parallel", "arbitrary"),
        name="nsa_attn",
    )(q, kv, kv, kv, kv, bias_t, gates, o_cmp, chosen, jnp.asarray(expand, BF16))


def _fox_proj_body(h_ref, g_ref, w_ref, bf_ref, q_ref, k_ref, v_ref, cum_ref, carry_ref):
    tm = h_ref.shape[1]

    @pl.when(pl.program_id(1) == 0)
    def _():
        carry_ref[...] = jnp.zeros_like(carry_ref)

    xn = _rms(h_ref[0], g_ref[...]).astype(BF16)
    res = _dot(xn, w_ref[...])
    for hd in range(N_HEADS):
        cols = slice(hd * HEAD_DIM, (hd + 1) * HEAD_DIM)
        q_ref[0, hd] = (res[:, cols] * Q_SCALE).astype(BF16)
        k_ref[0, hd] = res[:, Q_DIM + hd * HEAD_DIM:Q_DIM + (hd + 1) * HEAD_DIM].astype(BF16)
        v_ref[0, hd] = res[:, 2 * Q_DIM + hd * HEAD_DIM:2 * Q_DIM + (hd + 1) * HEAD_DIM].astype(BF16)
    log_f = jax.nn.log_sigmoid(res[:, 3 * Q_DIM:] + bf_ref[...])
    tri = jnp.where(lax.broadcasted_iota(jnp.int32, (tm, tm), 0)
                    >= lax.broadcasted_iota(jnp.int32, (tm, tm), 1), 1.0, 0.0).astype(BF16)
    hi, mid, lo = _split3(log_f)
    cum = _dot(tri, hi) + _dot(tri, mid) + _dot(tri, lo) + carry_ref[...]
    cum_ref[0] = cum
    carry_ref[...] = cum[tm - 1:tm]


def _fox_proj(h3, g, w, b_f, layer, tm=512):
    b, s, d = h3.shape
    n_in = w.shape[-1]
    head_spec = pl.BlockSpec((1, N_HEADS, tm, HEAD_DIM), lambda i, j: (i, 0, j, 0))
    head_shape = jax.ShapeDtypeStruct((b, N_HEADS, s, HEAD_DIM), BF16)
    return pl.pallas_call(
        _fox_proj_body,
        grid=(b, s // tm),
        in_specs=[
            pl.BlockSpec((1, tm, d), lambda i, j: (i, j, 0)),
            pl.BlockSpec((1, d), lambda i, j: (0, 0)),
            pl.BlockSpec((None, d, n_in), lambda i, j: (layer, 0, 0)),
            pl.BlockSpec((None, 1, N_HEADS), lambda i, j: (layer, 0, 0)),
        ],
        out_specs=[head_spec, head_spec, head_spec,
                   pl.BlockSpec((1, tm, N_HEADS), lambda i, j: (i, j, 0))],
        out_shape=[head_shape, head_shape, head_shape,
                   jax.ShapeDtypeStruct((b, s, N_HEADS), F32)],
        scratch_shapes=[pltpu.VMEM((1, N_HEADS), F32)],
        compiler_params=_cparams("parallel", "arbitrary"),
        name="fox_proj",
    )(h3, g, w, b_f)


def _fox_attn_body(q_ref, k_ref, v_ref, nc_ref, o_ref, s_ref, p_ref):
    T = FOX_TILE
    n_q = q_ref.shape[2] // T
    causal = (lax.broadcasted_iota(jnp.int32, (T, T), 0)
              >= lax.broadcasted_iota(jnp.int32, (T, T), 1))
    for qi in range(n_q):
        slot = qi % 2
        rows = slice(qi * T, (qi + 1) * T)
        outs = []
        for hh in range(q_ref.shape[1]):
            q = q_ref[0, hh, rows, :]
            peak = None
            for kt in range(qi + 1):
                cols = slice(kt * T, (kt + 1) * T)
                s = _dot_nt(q, k_ref[0, hh, cols, :]) + nc_ref[0, hh, :, cols]
                if kt == qi:
                    s = jnp.where(causal, s, NEG_INF)
                s_ref[slot, hh, :, cols] = s
                part = jnp.maximum(s[:, :T // 2], s[:, T // 2:])
                peak = part if peak is None else jnp.maximum(peak, part)
            m = jnp.max(peak, axis=-1, keepdims=True)
            norm = None
            for kt in range(qi + 1):
                cols = slice(kt * T, (kt + 1) * T)
                p = jnp.exp2(s_ref[slot, hh, :, cols] - m)
                p_ref[slot, hh, :, cols] = p.astype(BF16)
                part = p[:, :T // 2] + p[:, T // 2:]
                norm = part if norm is None else norm + part
            extent = (qi + 1) * T
            acc = _dot(p_ref[slot, hh, :, :extent], v_ref[0, hh, :extent, :])
            outs.append(acc * (1.0 / jnp.sum(norm, axis=-1, keepdims=True)))
        o_ref[0, rows, :] = jnp.concatenate(outs, axis=-1).astype(BF16)


def _fox_attn(q, k, v, neg_cum, heads_per_step=2):
    b, nh, s, dh = q.shape
    T = FOX_TILE
    hp = heads_per_step
    full = pl.BlockSpec((1, hp, s, dh), lambda i, h: (i, h, 0, 0))
    return pl.pallas_call(
        _fox_attn_body,
        grid=(b, nh // hp),
        in_specs=[full, full, full, pl.BlockSpec((1, hp, 1, s), lambda i, h: (i, h, 0, 0))],
        out_specs=pl.BlockSpec((1, s, hp * dh), lambda i, h: (i, 0, h)),
        out_shape=jax.ShapeDtypeStruct((b, s, nh * dh), BF16),
        scratch_shapes=[
            pltpu.VMEM((2, hp, T, s), F32),
            pltpu.VMEM((2, hp, T, s), BF16),
        ],
        compiler_params=_cparams("parallel", "parallel"),
        name="fox_attn",
    )(q, k, v, neg_cum)


def kernel(x, norm_g, ffn_w_gate, ffn_w_up, ffn_w_down, rel_bias, nsa_w_in, nsa_cmp_pe, nsa_cmp_w1,
           nsa_cmp_b1, nsa_cmp_w2, nsa_w_out, fox_w_in, fox_b_f, fox_w_out):
    b, s, d = x.shape
    depth = norm_g.shape[0]
    n = b * s
    wg, wu, wd = ffn_w_gate.astype(BF16), ffn_w_up.astype(BF16), ffn_w_down.astype(BF16)
    gate_cols = np.arange(N_GATES).reshape(3, KV_GROUPS, Q_PER_GROUP).transpose(1, 0, 2).reshape(-1)
    n_main = Q_DIM + 6 * KV_DIM
    nsa_w_main = nsa_w_in[:, :, :n_main].astype(BF16)
    nsa_w_gate = nsa_w_in[:, :, n_main:][:, :, gate_cols].astype(BF16)
    nsa_w_out_b = nsa_w_out.astype(BF16)
    fox_w_in_b, fox_w_out_b = fox_w_in.astype(BF16), fox_w_out.astype(BF16)
    cmp_w1_b, cmp_w2_b = nsa_cmp_w1.astype(BF16), nsa_cmp_w2.astype(BF16)
    cmp_pe = nsa_cmp_pe.reshape(nsa_cmp_pe.shape[0], 2, 1, CMP_BLOCK * HEAD_DIM)
    cmp_b1 = nsa_cmp_b1[:, :, None, :]
    fox_bf = fox_b_f[:, None, :]
    gains = norm_g[:, :, None, :]

    bias_c, bias_t = _bias_tables(rel_bias, s)

    h = x.reshape(n, d)
    for i in range(depth):
        g = gains[i]
        j = i // 2
        h = _ffn(h, g[0], g[1], wg, wu, wd, i, 0)
        if i % 2 == 0:
            q, kv, gates = _nsa_proj(h.reshape(b, s, d), g[2], nsa_w_main, nsa_w_gate, j)
            cmp_kv = _compress(kv, cmp_pe, cmp_w1_b, cmp_b1, cmp_w2_b, j)
            o_cmp, chosen = _nsa_select(q, cmp_kv, bias_c)
            o = _nsa_attn(q, kv, o_cmp, chosen, bias_t, gates).reshape(n, Q_DIM)
            h = _outproj(o, nsa_w_out_b, h, g[3], j)
        else:
            q, k, v, cum = _fox_proj(h.reshape(b, s, d), g[2], fox_w_in_b, fox_bf, j)
            neg_cum = (-LOG2E * cum).transpose(0, 2, 1)[:, :, None, :]
            o = _fox_attn(q, k, v, neg_cum).reshape(n, Q_DIM)
            h = _outproj(o, fox_w_out_b, h, g[3], j)
        h = _ffn(h, g[4], g[5], wg, wu, wd, i, 1)
    return h.reshape(b, s, d)
```

```python
import math

import numpy as np
import jax
import jax.numpy as jnp
from jax import lax
from jax.experimental import pallas as pl
from jax.experimental.pallas import tpu as pltpu

N_HEADS = 16
HEAD_DIM = 64
KV_GROUPS = 4
Q_PER_GROUP = N_HEADS // KV_GROUPS
CMP_BLOCK = 32
CMP_STRIDE = 16
SEL_BLOCK = 64
SEL_SHIFT = 6
SEL_TOPK = 16
WINDOW = 512
NUM_BUCKETS = 32
MAX_DISTANCE = 128
RMS_EPS = 1e-6
NEG_INF = -1e30
FORCED_SCORE = 1e9
Q_DIM = N_HEADS * HEAD_DIM
KV_DIM = KV_GROUPS * HEAD_DIM
N_GATES = 3 * N_HEADS
LOG2E = math.log2(math.e)
Q_SCALE = HEAD_DIM ** -0.5 * LOG2E

SUBLANES = 8
ATT_TILE = 256
WIN_TILES = WINDOW // ATT_TILE
FAR_CHUNK = 512
GROUPS_PER_STEP = 2
FOX_TILE = 256
FOX_AUG = 2 * HEAD_DIM
VMEM_LIMIT = 56 * 1024 * 1024

BF16 = jnp.bfloat16
F32 = jnp.float32


def _cparams(*sem):
    return pltpu.CompilerParams(dimension_semantics=sem, vmem_limit_bytes=VMEM_LIMIT)


def _rms(x, g):
    return x * lax.rsqrt(jnp.mean(x * x, axis=-1, keepdims=True) + RMS_EPS) * g


def _dot(a, b):
    return jnp.dot(a, b, preferred_element_type=F32)


def _dot_nt(a, b):
    return lax.dot_general(a, b, (((1,), (1,)), ((), ())), preferred_element_type=F32)


def _dot_tn(a, b):
    return lax.dot_general(a, b, (((0,), (0,)), ((), ())), preferred_element_type=F32)


def _split3(x):
    hi = x.astype(BF16)
    r1 = x - hi.astype(F32)
    mid = r1.astype(BF16)
    lo = (r1 - mid.astype(F32)).astype(BF16)
    return hi, mid, lo


def _fold(x, op):
    parts = x.reshape(x.shape[0] // SUBLANES, SUBLANES, x.shape[1])
    return op(parts, axis=0)


def _ffn_body(h_ref, gpre_ref, gpost_ref, wg_ref, wu_ref, wd_ref, o_ref, xn_ref, acc_ref):
    k = pl.program_id(1)

    @pl.when(k == 0)
    def _():
        xn_ref[...] = _rms(h_ref[...], gpre_ref[...]).astype(BF16)
        acc_ref[...] = jnp.zeros_like(acc_ref)

    xn = xn_ref[...]
    g = _dot(xn, wg_ref[...])
    u = _dot(xn, wu_ref[...])
    a = (g * jax.nn.sigmoid(g) * u).astype(BF16)
    acc_ref[...] += _dot(a, wd_ref[...])

    @pl.when(k == pl.num_programs(1) - 1)
    def _():
        o_ref[...] = h_ref[...] + 0.5 * _rms(acc_ref[...], gpost_ref[...])


def _ffn(h, g_pre, g_post, wg, wu, wd, layer, half, tm=1024, tf=1408):
    n, d = h.shape
    f = wg.shape[-1]
    row = lambda i, k: (i, 0)
    vec = lambda i, k: (0, 0)
    return pl.pallas_call(
        _ffn_body,
        grid=(n // tm, f // tf),
        in_specs=[
            pl.BlockSpec((tm, d), row),
            pl.BlockSpec((1, d), vec),
            pl.BlockSpec((1, d), vec),
            pl.BlockSpec((None, None, d, tf), lambda i, k: (layer, half, 0, k)),
            pl.BlockSpec((None, None, d, tf), lambda i, k: (layer, half, 0, k)),
            pl.BlockSpec((None, None, tf, d), lambda i, k: (layer, half, k, 0)),
        ],
        out_specs=pl.BlockSpec((tm, d), row),
        out_shape=jax.ShapeDtypeStruct((n, d), F32),
        scratch_shapes=[pltpu.VMEM((tm, d), BF16), pltpu.VMEM((tm, d), F32)],
        compiler_params=_cparams("parallel", "arbitrary"),
        name="ffn",
    )(h, g_pre, g_post, wg, wu, wd)


def _outproj_body(ot_ref, w_ref, h_ref, g_ref, out_ref):
    y = _dot_tn(ot_ref[0], w_ref[...])
    out_ref[0] = h_ref[0] + _rms(y, g_ref[...])


def _outproj(o_t, w, h3, g, layer, tm=512):
    b, s, d = h3.shape
    kdim = o_t.shape[1]
    return pl.pallas_call(
        _outproj_body,
        grid=(b, s // tm),
        in_specs=[
            pl.BlockSpec((1, kdim, tm), lambda i, j: (i, 0, j)),
            pl.BlockSpec((None, kdim, d), lambda i, j: (layer, 0, 0)),
            pl.BlockSpec((1, tm, d), lambda i, j: (i, j, 0)),
            pl.BlockSpec((1, d), lambda i, j: (0, 0)),
        ],
        out_specs=pl.BlockSpec((1, tm, d), lambda i, j: (i, j, 0)),
        out_shape=jax.ShapeDtypeStruct((b, s, d), F32),
        compiler_params=_cparams("parallel", "parallel"),
        name="outproj",
    )(o_t, w, h3, g)


def _nsa_proj_body(h_ref, g_ref, w_ref, wvt_ref, wgt_ref, q_ref, k_ref, vt_ref, gate_ref):
    xn = _rms(h_ref[0], g_ref[...]).astype(BF16)
    res = _dot(xn, w_ref[...])
    for hd in range(N_HEADS):
        q_ref[0, hd] = (res[:, hd * HEAD_DIM:(hd + 1) * HEAD_DIM] * Q_SCALE).astype(BF16)
    for a in range(4 * KV_GROUPS):
        lo = Q_DIM + a * HEAD_DIM
        k_ref[0, a] = res[:, lo:lo + HEAD_DIM].astype(BF16)
    v_t = _dot_nt(wvt_ref[...], xn)
    for a in range(2 * KV_GROUPS):
        vt_ref[0, a] = v_t[a * HEAD_DIM:(a + 1) * HEAD_DIM].astype(BF16)
    gates = jax.nn.sigmoid(_dot_nt(wgt_ref[...], xn))
    width = 3 * Q_PER_GROUP
    for grp in range(KV_GROUPS):
        gate_ref[0, grp] = gates[grp * width:(grp + 1) * width]


def _nsa_proj(h3, g, w, w_vt, w_gt, layer, tm=512):
    b, s, d = h3.shape
    return pl.pallas_call(
        _nsa_proj_body,
        grid=(b, s // tm),
        in_specs=[
            pl.BlockSpec((1, tm, d), lambda i, j: (i, j, 0)),
            pl.BlockSpec((1, d), lambda i, j: (0, 0)),
            pl.BlockSpec((None, d, w.shape[-1]), lambda i, j: (layer, 0, 0)),
            pl.BlockSpec((None, 2 * KV_DIM, d), lambda i, j: (layer, 0, 0)),
            pl.BlockSpec((None, N_GATES, d), lambda i, j: (layer, 0, 0)),
        ],
        out_specs=[
            pl.BlockSpec((1, N_HEADS, tm, HEAD_DIM), lambda i, j: (i, 0, j, 0)),
            pl.BlockSpec((1, 4 * KV_GROUPS, tm, HEAD_DIM), lambda i, j: (i, 0, j, 0)),
            pl.BlockSpec((1, 2 * KV_GROUPS, HEAD_DIM, tm), lambda i, j: (i, 0, 0, j)),
            pl.BlockSpec((1, KV_GROUPS, 3 * Q_PER_GROUP, tm), lambda i, j: (i, 0, 0, j)),
        ],
        out_shape=[
            jax.ShapeDtypeStruct((b, N_HEADS, s, HEAD_DIM), BF16),
            jax.ShapeDtypeStruct((b, 4 * KV_GROUPS, s, HEAD_DIM), BF16),
            jax.ShapeDtypeStruct((b, 2 * KV_GROUPS, HEAD_DIM, s), BF16),
            jax.ShapeDtypeStruct((b, KV_GROUPS, 3 * Q_PER_GROUP, s), F32),
        ],
        compiler_params=_cparams("parallel", "parallel"),
        name="nsa_proj",
    )(h3, g, w, w_vt, w_gt)


def _compress_body(x_ref, pe_ref, w1_ref, b1_ref, w2_ref, w2t_ref, o_ref, ot_ref):
    n_chunk = x_ref.shape[3]
    half = CMP_STRIDE * HEAD_DIM
    x = x_ref[0, 0].reshape(KV_GROUPS * n_chunk, half)
    top = _dot(x, w1_ref[:half, :])
    bot = _dot(x, w1_ref[half:, :])
    bot_next = pltpu.roll(bot, KV_GROUPS * n_chunk - 1, 0)
    pe = jnp.broadcast_to(pe_ref[...].astype(BF16), (8, 2 * half))
    const = _dot(pe, w1_ref[...])[0:1] + b1_ref[...]
    hid = jax.nn.gelu(top + bot_next + const).astype(BF16)
    out = _dot(hid, w2_ref[...])
    row = lax.broadcasted_iota(jnp.int32, out.shape, 0) & (n_chunk - 1)
    o_ref[0, 0] = jnp.where(row < n_chunk - 1, out, 0.0).reshape(KV_GROUPS, n_chunk, HEAD_DIM).astype(BF16)
    out_t = _dot_nt(w2t_ref[...], hid)
    col = lax.broadcasted_iota(jnp.int32, out_t.shape, 1) & (n_chunk - 1)
    out_t = jnp.where(col < n_chunk - 1, out_t, 0.0).astype(BF16)
    for grp in range(KV_GROUPS):
        ot_ref[0, 0, grp] = out_t[:, grp * n_chunk:(grp + 1) * n_chunk]


def _compress(kk, pe, w1, b1, w2, w2t, layer):
    b, _, s, dh = kk.shape
    n_chunk = s // CMP_STRIDE
    x = kk[:, :2 * KV_GROUPS].reshape(b, 2, KV_GROUPS, n_chunk, CMP_STRIDE * dh)
    hidden = w1.shape[-1]
    return pl.pallas_call(
        _compress_body,
        grid=(b, 2),
        in_specs=[
            pl.BlockSpec((1, 1, KV_GROUPS, n_chunk, CMP_STRIDE * dh), lambda i, a: (i, a, 0, 0, 0)),
            pl.BlockSpec((None, None, 1, CMP_BLOCK * dh), lambda i, a: (layer, a, 0, 0)),
            pl.BlockSpec((None, None, CMP_BLOCK * dh, hidden), lambda i, a: (layer, a, 0, 0)),
            pl.BlockSpec((None, None, 1, hidden), lambda i, a: (layer, a, 0, 0)),
            pl.BlockSpec((None, None, hidden, dh), lambda i, a: (layer, a, 0, 0)),
            pl.BlockSpec((None, None, dh, hidden), lambda i, a: (layer, a, 0, 0)),
        ],
        out_specs=[
            pl.BlockSpec((1, 1, KV_GROUPS, n_chunk, dh), lambda i, a: (i, a, 0, 0, 0)),
            pl.BlockSpec((1, 1, KV_GROUPS, dh, n_chunk), lambda i, a: (i, a, 0, 0, 0)),
        ],
        out_shape=[
            jax.ShapeDtypeStruct((b, 2, KV_GROUPS, n_chunk, dh), BF16),
            jax.ShapeDtypeStruct((b, 2, KV_GROUPS, dh, n_chunk), BF16),
        ],
        compiler_params=_cparams("parallel", "parallel"),
        name="nsa_compress",
    )(x, pe, w1, b1, w2, w2t)


def _t5_bucket_np(rel):
    n = np.maximum(rel, 0)
    max_exact = NUM_BUCKETS // 2
    nf = np.maximum(n, 1).astype(np.float32)
    ratio = np.log(nf / np.float32(max_exact)) / np.float32(math.log(MAX_DISTANCE / max_exact))
    large = max_exact + (ratio * np.float32(NUM_BUCKETS - max_exact)).astype(np.int32)
    large = np.minimum(large, NUM_BUCKETS - 1)
    return np.where(n < max_exact, n, large).astype(np.int32)


def _bucket_maps(s):
    n_chunk = s // CMP_STRIDE
    t = np.arange(s)[None, :]
    blk_end = np.arange(n_chunk)[:, None] * CMP_STRIDE + CMP_BLOCK - 1
    rel_c = t - blk_end
    map_c = np.where(rel_c >= 0, _t5_bucket_np(rel_c), -1).astype(np.int32)
    j = np.arange(ATT_TILE)[:, None]
    i = np.arange(ATT_TILE)[None, :]
    diag = np.where(i - j >= 0, _t5_bucket_np(i - j), -1)
    sub = _t5_bucket_np(ATT_TILE + i - j)
    edge = np.where(j > i, _t5_bucket_np(WINDOW + i - j), -1)
    map_t = np.stack([diag, sub, edge]).astype(np.int32)
    assert _t5_bucket_np(np.arange(ATT_TILE + 1, s + WINDOW)).min() == _FAR_BUCKET
    return map_c, map_t


_FAR_BUCKET = NUM_BUCKETS - 1
TILE_DIAG, TILE_SUB, TILE_EDGE = 0, 1, 2


def _bias_body(rb_ref, mc_ref, mt_ref, bc_ref, bt_ref):
    hd = pl.program_id(0)

    def lookup(bucket, shift):
        acc = jnp.zeros(bucket.shape, F32)
        for bk in range(NUM_BUCKETS):
            acc = jnp.where(bucket == bk, (rb_ref[bk, hd] - shift) * LOG2E, acc)
        return jnp.where(bucket < 0, NEG_INF, acc)

    bc_ref[0] = lookup(mc_ref[...], 0.0)
    for d in range(mt_ref.shape[0]):
        bt_ref[d, 0] = lookup(mt_ref[d], rb_ref[_FAR_BUCKET, hd])


def _bias_tables(rel_bias, s):
    map_c, map_t = _bucket_maps(s)
    n_chunk = map_c.shape[0]
    n_tab = map_t.shape[0]
    T = ATT_TILE
    R = Q_PER_GROUP
    return pl.pallas_call(
        _bias_body,
        grid=(N_HEADS,),
        in_specs=[
            pl.BlockSpec(memory_space=pltpu.SMEM),
            pl.BlockSpec((n_chunk, s), lambda i: (0, 0)),
            pl.BlockSpec((n_tab, T, T), lambda i: (0, 0, 0)),
        ],
        out_specs=[
            pl.BlockSpec((1, n_chunk, s), lambda i: (i, 0, 0)),
            pl.BlockSpec((n_tab, 1, T, T), lambda i: (0, i // R, 0, i % R)),
        ],
        out_shape=[
            jax.ShapeDtypeStruct((N_HEADS, n_chunk, s), F32),
            jax.ShapeDtypeStruct((n_tab, KV_GROUPS, T, R * T), F32),
        ],
        compiler_params=_cparams("parallel"),
        name="t5_bias_tables",
    )(rel_bias, jnp.asarray(map_c), jnp.asarray(map_t))


def _nsa_select_body(q_ref, kc_ref, vct_ref, bc_ref, ovl_ref, oc_ref, ch_ref):
    T = ATT_TILE
    R = Q_PER_GROUP
    n_sel = ovl_ref.shape[0]
    ovl = ovl_ref[...]
    j_blk = lax.broadcasted_iota(jnp.int32, (n_sel, T), 0)
    for u in range(q_ref.shape[2] // T):
        qi = pl.program_id(2) * (q_ref.shape[2] // T) + u
        cols = slice(u * T, (u + 1) * T)
        q = q_ref[0, :, cols, :].reshape(R * T, HEAD_DIM)
        bias = jnp.concatenate([bc_ref[r, :, cols] for r in range(R)], axis=1)
        s_c = _dot_nt(kc_ref[0, 0, 0], q) + bias
        m_c = jnp.max(s_c, axis=0, keepdims=True)
        p_c = jnp.exp2(s_c - m_c)
        p_c = p_c * (1.0 / jnp.sum(p_c, axis=0, keepdims=True))
        t_col = qi * T + (lax.broadcasted_iota(jnp.int32, (1, R * T), 1) & (T - 1))
        p_c = jnp.where(t_col >= CMP_BLOCK - 1, p_c, 0.0)
        o_c = _dot(vct_ref[0, 0, 0], p_c.astype(BF16))
        for r in range(R):
            oc_ref[0, r, :, cols] = o_c[:, r * T:(r + 1) * T]

        p_sum = p_c[:, :T]
        for r in range(1, R):
            p_sum = p_sum + p_c[:, r * T:(r + 1) * T]
        hi, mid, lo = _split3(p_sum)
        imp = _dot(ovl, hi) + _dot(ovl, mid) + _dot(ovl, lo)
        cur = (qi * T + lax.broadcasted_iota(jnp.int32, (n_sel, T), 1)) >> SEL_SHIFT
        forced = (j_blk == 0) | (j_blk == cur) | (j_blk == cur - 1)
        imp = jnp.where(forced, FORCED_SCORE, jnp.where(j_blk <= cur, imp, NEG_INF))
        rank = jnp.zeros((n_sel, T), F32)
        for i in range(n_sel):
            row = imp[i:i + 1, :]
            rank = rank + jnp.where(j_blk > i, jnp.where(row >= imp, 1.0, 0.0),
                                    jnp.where(row > imp, 1.0, 0.0))
        ch_ref[0, 0, :, cols] = jnp.where(rank < min(SEL_TOPK, n_sel), 1.0, 0.0)


def _nsa_select(q, k_cmp, v_cmp_t, bias_c, tiles_per_step=2):
    b, _, s, dh = q.shape
    R = Q_PER_GROUP
    n_chunk = s // CMP_STRIDE
    n_sel = s // SEL_BLOCK
    tq = tiles_per_step * ATT_TILE
    c_start = np.arange(n_chunk)[None, :] * CMP_STRIDE
    j = np.arange(n_sel)[:, None]
    overlap = (c_start < (j + 1) * SEL_BLOCK) & (c_start + CMP_BLOCK > j * SEL_BLOCK)
    overlap[:, n_chunk - 1] = False
    return pl.pallas_call(
        _nsa_select_body,
        grid=(b, KV_GROUPS, s // tq),
        in_specs=[
            pl.BlockSpec((1, R, tq, dh), lambda i, g, t: (i, g, t, 0)),
            pl.BlockSpec((1, 1, 1, n_chunk, dh), lambda i, g, t: (i, 0, g, 0, 0)),
            pl.BlockSpec((1, 1, 1, dh, n_chunk), lambda i, g, t: (i, 1, g, 0, 0)),
            pl.BlockSpec((R, n_chunk, tq), lambda i, g, t: (g, 0, t)),
            pl.BlockSpec((n_sel, n_chunk), lambda i, g, t: (0, 0)),
        ],
        out_specs=[
            pl.BlockSpec((1, R, dh, tq), lambda i, g, t: (i, g, 0, t)),
            pl.BlockSpec((1, 1, n_sel, tq), lambda i, g, t: (i, g, 0, t)),
        ],
        out_shape=[
            jax.ShapeDtypeStruct((b, N_HEADS, dh, s), F32),
            jax.ShapeDtypeStruct((b, KV_GROUPS, n_sel, s), F32),
        ],
        compiler_params=_cparams("parallel", "parallel", "parallel"),
        name="nsa_select",
    )(q, k_cmp, v_cmp_t, bias_c, jnp.asarray(overlap, BF16))


def _nsa_attn_body(q_ref, ks_ref, kw_ref, vst_ref, vwt_ref, bt_ref, gate_ref, oc_ref, ch_ref,
                   o_ref, rows_ref, s_ref, near_ref, win_ref, peak_ref, norm_ref, acc_ref):
    T = ATT_TILE
    R = Q_PER_GROUP
    C = FAR_CHUNK
    N = R * T
    groups = range(ks_ref.shape[1])
    qi = pl.program_id(2)
    n_far = jnp.maximum(qi - 1, 0)
    n_chunks = (n_far + C // T - 1) // (C // T)
    qs = [q_ref[0, g * R:(g + 1) * R].reshape(N, HEAD_DIM) for g in groups]

    def add_block_rows(g, which, s, off):
        first = off // SEL_BLOCK
        pieces = [s[b * SEL_BLOCK:(b + 1) * SEL_BLOCK] + rows_ref[g, which, pl.ds(first + b, 1), :]
                  for b in range(s.shape[0] // SEL_BLOCK)]
        return jnp.concatenate(pieces, axis=0)

    def near_logits(g, k_ref, d, table, masked):
        off = pl.multiple_of(jnp.maximum(qi - d, 0) * T, T)
        s = _dot_nt(k_ref[0, g, pl.ds(off, T), :], qs[g])
        if table is not None:
            s = s + bt_ref[table, g]
        if masked:
            s = add_block_rows(g, 0, s, off)
        if d > 0:
            s = jnp.where(qi >= d, s, NEG_INF)
        return s, off

    def logits_pass(g, k_ref, tiles, masked, out_ref):
        offs = []
        peak = None
        for idx, (d, table) in enumerate(tiles):
            s, off = near_logits(g, k_ref, d, table, masked)
            out_ref[g, idx * T:(idx + 1) * T, :] = s
            part = _fold(s, jnp.max)
            peak = part if peak is None else jnp.maximum(peak, part)
            offs.append(off)
        return offs, peak

    def values_pass(g, logit_ref, vt_ref, offs, m):
        norm = acc = None
        for idx, off in enumerate(offs):
            p = jnp.exp2(logit_ref[g, idx * T:(idx + 1) * T, :] - m)
            pv = _dot(vt_ref[0, g, :, pl.ds(off, T)], p.astype(BF16))
            part = _fold(p, jnp.sum)
            norm = part if norm is None else norm + part
            acc = pv if acc is None else acc + pv
        return norm, acc

    win_tiles = [(0, TILE_DIAG), (1, TILE_SUB)]
    win_tiles += [(d, None) for d in range(2, WIN_TILES)] + [(WIN_TILES, TILE_EDGE)]
    near_sel = [(1, TILE_SUB), (0, TILE_DIAG)]
    win_offs, m_win, sel_offs = [], [], []
    blk = lax.broadcasted_iota(jnp.int32, (ch_ref.shape[2], N), 0)
    for g in groups:
        add = (ch_ref[0, g] - 1.0) * -NEG_INF
        add = jnp.concatenate([add] * R, axis=1)
        rows_ref[g, 0] = add
        rows_ref[g, 1] = jnp.where(blk < n_far * (T // SEL_BLOCK), add, NEG_INF)
        offs, peak = logits_pass(g, kw_ref, win_tiles, False, win_ref)
        win_offs.append(offs)
        m_win.append(jnp.max(peak, axis=0, keepdims=True))
        offs, peak = logits_pass(g, ks_ref, near_sel, True, near_ref)
        sel_offs.append(offs)
        peak_ref[g] = peak

    def far_logits(c, carry):
        off = pl.multiple_of(c * C, C)
        for g in groups:
            s = _dot_nt(ks_ref[0, g, pl.ds(off, C), :], qs[g])
            s = add_block_rows(g, 1, s, off)
            s_ref[g, pl.ds(off, C), :] = s
            peak_ref[g] = jnp.maximum(peak_ref[g], _fold(s, jnp.max))
        return carry

    lax.fori_loop(0, n_chunks, far_logits, 0)

    m_sel = []
    for g in groups:
        m_sel.append(jnp.max(peak_ref[g], axis=0, keepdims=True))
        norm, acc = values_pass(g, near_ref, vst_ref, sel_offs[g], m_sel[g])
        norm_ref[g] = norm
        acc_ref[g] = acc

    def far_values(c, carry):
        off = pl.multiple_of(c * C, C)
        for g in groups:
            p = jnp.exp2(s_ref[g, pl.ds(off, C), :] - m_sel[g])
            norm_ref[g] += _fold(p, jnp.sum)
            acc_ref[g] += _dot(vst_ref[0, g, :, pl.ds(off, C)], p.astype(BF16))
        return carry

    lax.fori_loop(0, n_chunks, far_values, 0)

    for g in groups:
        gate = gate_ref[0, g]
        gate_of = lambda br: jnp.concatenate(
            [gate[br * R + r:br * R + r + 1, :] for r in range(R)], axis=1)
        norm, acc = values_pass(g, win_ref, vwt_ref, win_offs[g], m_win[g])
        w_win = gate_of(2) * (1.0 / jnp.sum(norm, axis=0, keepdims=True))
        w_sel = gate_of(1) * (1.0 / jnp.sum(norm_ref[g], axis=0, keepdims=True))
        o_c = jnp.concatenate([oc_ref[0, g * R + r] for r in range(R)], axis=1)
        o = gate_of(0) * o_c + w_sel * acc_ref[g] + w_win * acc
        for r in range(R):
            hd = g * R + r
            o_ref[0, hd * HEAD_DIM:(hd + 1) * HEAD_DIM, :] = o[:, r * T:(r + 1) * T].astype(BF16)


def _nsa_attn(q, kk, v_t, o_cmp, chosen, bias_t, gates):
    b, _, s, dh = q.shape
    T = ATT_TILE
    R = Q_PER_GROUP
    N = R * T
    gp = GROUPS_PER_STEP
    per = KV_GROUPS // gp
    n_sel = s // SEL_BLOCK
    far_keys = max(s - 2 * T, FAR_CHUNK)
    once = dict(pipeline_mode=pl.Buffered(1))
    k_spec = lambda a: pl.BlockSpec((1, gp, s, dh), lambda i, g, t: (i, a * per + g, 0, 0))
    vt_spec = lambda a: pl.BlockSpec((1, gp, dh, s), lambda i, g, t: (i, a * per + g, 0, 0))
    return pl.pallas_call(
        _nsa_attn_body,
        grid=(b, per, s // T),
        in_specs=[
            pl.BlockSpec((1, gp * R, T, dh), lambda i, g, t: (i, g, t, 0)),
            k_spec(2), k_spec(3), vt_spec(0), vt_spec(1),
            pl.BlockSpec((bias_t.shape[0], gp, T, N), lambda i, g, t: (0, g, 0, 0), **once),
            pl.BlockSpec((1, gp, 3 * R, T), lambda i, g, t: (i, g, 0, t)),
            pl.BlockSpec((1, gp * R, dh, T), lambda i, g, t: (i, g, 0, t)),
            pl.BlockSpec((1, gp, n_sel, T), lambda i, g, t: (i, g, 0, t)),
        ],
        out_specs=pl.BlockSpec((1, gp * R * dh, T), lambda i, g, t: (i, g, t)),
        out_shape=jax.ShapeDtypeStruct((b, Q_DIM, s), BF16),
        scratch_shapes=[
            pltpu.VMEM((gp, 2, n_sel, N), F32),
            pltpu.VMEM((gp, far_keys, N), F32),
            pltpu.VMEM((gp, 2 * T, N), F32),
            pltpu.VMEM((gp, (WIN_TILES + 1) * T, N), F32),
            pltpu.VMEM((gp, SUBLANES, N), F32),
            pltpu.VMEM((gp, SUBLANES, N), F32),
            pltpu.VMEM((gp, dh, N), F32),
        ],
        compiler_params=_cparams("parallel", "parallel", "arbitrary"),
        name="nsa_attn",
    )(q, kk, kk, v_t, v_t, bias_t, gates, o_cmp, chosen)


def _fox_proj_body(h_ref, g_ref, w_ref, wvt_ref, bf_ref, q_ref, k_ref, vt_ref, carry_ref):
    tm = h_ref.shape[1]

    @pl.when(pl.program_id(1) == 0)
    def _():
        carry_ref[...] = jnp.zeros_like(carry_ref)

    xn = _rms(h_ref[0], g_ref[...]).astype(BF16)
    res = _dot(xn, w_ref[...])
    log_f = jax.nn.log_sigmoid(res[:, 2 * Q_DIM:] + bf_ref[...])
    tri = jnp.where(lax.broadcasted_iota(jnp.int32, (tm, tm), 0)
                    >= lax.broadcasted_iota(jnp.int32, (tm, tm), 1), 1.0, 0.0).astype(BF16)
    hi, mid, lo = _split3(log_f)
    cum = _dot(tri, hi) + _dot(tri, mid) + _dot(tri, lo) + carry_ref[...]
    carry_ref[...] = cum[tm - 1:tm]
    terms = [t.astype(F32) for t in _split3(-LOG2E * cum)]

    lane = lax.broadcasted_iota(jnp.int32, (tm, FOX_AUG - HEAD_DIM), 1)
    ones = jnp.where(lane < len(terms), 1.0, 0.0).astype(BF16)
    for hd in range(N_HEADS):
        q_h = (res[:, hd * HEAD_DIM:(hd + 1) * HEAD_DIM] * Q_SCALE).astype(BF16)
        k_h = res[:, Q_DIM + hd * HEAD_DIM:Q_DIM + (hd + 1) * HEAD_DIM].astype(BF16)
        decay = jnp.zeros(lane.shape, F32)
        for idx, term in enumerate(terms):
            decay = jnp.where(lane == idx, term[:, hd:hd + 1], decay)
        q_ref[0, hd] = jnp.concatenate([q_h, ones], axis=-1)
        k_ref[0, hd] = jnp.concatenate([k_h, decay.astype(BF16)], axis=-1)
    v_t = _dot_nt(wvt_ref[...], xn)
    for hd in range(N_HEADS):
        vt_ref[0, hd] = v_t[hd * HEAD_DIM:(hd + 1) * HEAD_DIM].astype(BF16)


def _fox_proj(h3, g, w, w_vt, b_f, layer, tm=512):
    b, s, d = h3.shape
    aug_spec = pl.BlockSpec((1, N_HEADS, tm, FOX_AUG), lambda i, j: (i, 0, j, 0))
    aug_shape = jax.ShapeDtypeStruct((b, N_HEADS, s, FOX_AUG), BF16)
    return pl.pallas_call(
        _fox_proj_body,
        grid=(b, s // tm),
        in_specs=[
            pl.BlockSpec((1, tm, d), lambda i, j: (i, j, 0)),
            pl.BlockSpec((1, d), lambda i, j: (0, 0)),
            pl.BlockSpec((None, d, w.shape[-1]), lambda i, j: (layer, 0, 0)),
            pl.BlockSpec((None, Q_DIM, d), lambda i, j: (layer, 0, 0)),
            pl.BlockSpec((None, 1, N_HEADS), lambda i, j: (layer, 0, 0)),
        ],
        out_specs=[aug_spec, aug_spec,
                   pl.BlockSpec((1, N_HEADS, HEAD_DIM, tm), lambda i, j: (i, 0, 0, j))],
        out_shape=[aug_shape, aug_shape,
                   jax.ShapeDtypeStruct((b, N_HEADS, HEAD_DIM, s), BF16)],
        scratch_shapes=[pltpu.VMEM((1, N_HEADS), F32)],
        compiler_params=_cparams("parallel", "arbitrary"),
        name="fox_proj",
    )(h3, g, w, w_vt, b_f)


def _fox_attn_body(q_ref, k_ref, vt_ref, o_ref, s_ref, p_ref):
    T = FOX_TILE
    n_q = q_ref.shape[2] // T
    causal = (lax.broadcasted_iota(jnp.int32, (T, T), 0)
              <= lax.broadcasted_iota(jnp.int32, (T, T), 1))
    for qi in range(n_q):
        slot = qi % 2
        cols = slice(qi * T, (qi + 1) * T)
        for hh in range(q_ref.shape[1]):
            q = q_ref[0, hh, cols, :]
            peak = None
            for kt in range(qi + 1):
                keys = slice(kt * T, (kt + 1) * T)
                s = _dot_nt(k_ref[0, hh, keys, :], q)
                if kt == qi:
                    s = jnp.where(causal, s, NEG_INF)
                s_ref[slot, hh, keys, :] = s
                part = _fold(s, jnp.max)
                peak = part if peak is None else jnp.maximum(peak, part)
            m = jnp.max(peak, axis=0, keepdims=True)
            norm = None
            for kt in range(qi + 1):
                keys = slice(kt * T, (kt + 1) * T)
                p = jnp.exp2(s_ref[slot, hh, keys, :] - m)
                p_ref[slot, hh, keys, :] = p.astype(BF16)
                part = _fold(p, jnp.sum)
                norm = part if norm is None else norm + part
            extent = (qi + 1) * T
            acc = _dot(vt_ref[0, hh, :, :extent], p_ref[slot, hh, :extent, :])
            o = acc * (1.0 / jnp.sum(norm, axis=0, keepdims=True))
            o_ref[0, hh * HEAD_DIM:(hh + 1) * HEAD_DIM, cols] = o.astype(BF16)


def _fox_attn(q, k, v_t, heads_per_step=2):
    b, nh, s, aug = q.shape
    dh = v_t.shape[2]
    T = FOX_TILE
    hp = heads_per_step
    qk_spec = pl.BlockSpec((1, hp, s, aug), lambda i, h: (i, h, 0, 0))
    return pl.pallas_call(
        _fox_attn_body,
        grid=(b, nh // hp),
        in_specs=[qk_spec, qk_spec, pl.BlockSpec((1, hp, dh, s), lambda i, h: (i, h, 0, 0))],
        out_specs=pl.BlockSpec((1, hp * dh, s), lambda i, h: (i, h, 0)),
        out_shape=jax.ShapeDtypeStruct((b, nh * dh, s), BF16),
        scratch_shapes=[
            pltpu.VMEM((2, hp, s, T), F32),
            pltpu.VMEM((2, hp, s, T), BF16),
        ],
        compiler_params=_cparams("parallel", "parallel"),
        name="fox_attn",
    )(q, k, v_t)


def kernel(x, norm_g, ffn_w_gate, ffn_w_up, ffn_w_down, rel_bias, nsa_w_in, nsa_cmp_pe, nsa_cmp_w1,
           nsa_cmp_b1, nsa_cmp_w2, nsa_w_out, fox_w_in, fox_b_f, fox_w_out):
    b, s, d = x.shape
    depth = norm_g.shape[0]
    n = b * s
    t_last = lambda w: jnp.swapaxes(w, -1, -2)
    wg, wu, wd = ffn_w_gate.astype(BF16), ffn_w_up.astype(BF16), ffn_w_down.astype(BF16)
    c0 = Q_DIM
    col = lambda a: slice(c0 + a * KV_DIM, c0 + (a + 1) * KV_DIM)
    nsa_w_main = jnp.concatenate([nsa_w_in[:, :, :c0 + 3 * KV_DIM], nsa_w_in[:, :, col(4)]], axis=-1).astype(BF16)
    nsa_w_vt = t_last(jnp.concatenate([nsa_w_in[:, :, col(3)], nsa_w_in[:, :, col(5)]], axis=-1)).astype(BF16)
    gate_cols = np.arange(N_GATES).reshape(3, KV_GROUPS, Q_PER_GROUP).transpose(1, 0, 2).reshape(-1)
    nsa_w_gt = t_last(nsa_w_in[:, :, c0 + 6 * KV_DIM:][:, :, gate_cols]).astype(BF16)
    nsa_w_out_b = nsa_w_out.astype(BF16)
    fox_w_main = jnp.concatenate([fox_w_in[:, :, :2 * Q_DIM], fox_w_in[:, :, 3 * Q_DIM:]], axis=-1).astype(BF16)
    fox_w_vt = t_last(fox_w_in[:, :, 2 * Q_DIM:3 * Q_DIM]).astype(BF16)
    fox_w_out_b = fox_w_out.astype(BF16)
    cmp_w1_b, cmp_w2_b = nsa_cmp_w1.astype(BF16), nsa_cmp_w2.astype(BF16)
    cmp_w2t_b = t_last(nsa_cmp_w2).astype(BF16)
    cmp_pe = nsa_cmp_pe.reshape(nsa_cmp_pe.shape[0], 2, 1, CMP_BLOCK * HEAD_DIM)
    cmp_b1 = nsa_cmp_b1[:, :, None, :]
    fox_bf = fox_b_f[:, None, :]
    gains = norm_g[:, :, None, :]

    bias_c, bias_t = _bias_tables(rel_bias, s)

    h = x.reshape(n, d)
    for i in range(depth):
        g = gains[i]
        j = i // 2
        h = _ffn(h, g[0], g[1], wg, wu, wd, i, 0)
        h3 = h.reshape(b, s, d)
        if i % 2 == 0:
            q, kk, v_t, gates = _nsa_proj(h3, g[2], nsa_w_main, nsa_w_vt, nsa_w_gt, j)
            k_cmp, v_cmp_t = _compress(kk, cmp_pe, cmp_w1_b, cmp_b1, cmp_w2_b, cmp_w2t_b, j)
            o_cmp, chosen = _nsa_select(q, k_cmp, v_cmp_t, bias_c)
            o_t = _nsa_attn(q, kk, v_t, o_cmp, chosen, bias_t, gates)
            h3 = _outproj(o_t, nsa_w_out_b, h3, g[3], j)
        else:
            q, k, v_t = _fox_proj(h3, g[2], fox_w_main, fox_w_vt, fox_bf, j)
            o_t = _fox_attn(q, k, v_t)
            h3 = _outproj(o_t, fox_w_out_b, h3, g[3], j)
        h = _ffn(h3.reshape(n, d), g[4], g[5], wg, wu, wd, i, 1)
    return h.reshape(b, s, d)
```

```python
import math

import numpy as np
import jax
import jax.numpy as jnp
from jax import lax
from jax.experimental import pallas as pl
from jax.experimental.pallas import tpu as pltpu

N_HEADS = 16
HEAD_DIM = 64
KV_GROUPS = 4
Q_PER_GROUP = N_HEADS // KV_GROUPS
CMP_BLOCK = 32
CMP_STRIDE = 16
SEL_BLOCK = 64
SEL_SHIFT = 6
SEL_TOPK = 16
WINDOW = 512
NUM_BUCKETS = 32
MAX_DISTANCE = 128
RMS_EPS = 1e-6
NEG_INF = -1e30
FORCED_SCORE = 1e9
Q_DIM = N_HEADS * HEAD_DIM
KV_DIM = KV_GROUPS * HEAD_DIM
N_GATES = 3 * N_HEADS
LOG2E = math.log2(math.e)
Q_SCALE = HEAD_DIM ** -0.5 * LOG2E

SUBLANES = 8
ATT_TILE = 256
WIN_TILES = WINDOW // ATT_TILE
FAR_CHUNK = 512
GROUPS_PER_STEP = 2
FOX_TILE = 256
FOX_SLOTS = 2
AUG_DIM = HEAD_DIM + SUBLANES
FOX_QK_DIM = 2 * HEAD_DIM
VMEM_LIMIT = 56 * 1024 * 1024

BF16 = jnp.bfloat16
F32 = jnp.float32


def _cparams(*sem):
    return pltpu.CompilerParams(dimension_semantics=sem, vmem_limit_bytes=VMEM_LIMIT)


def _rms(x, g):
    return x * lax.rsqrt(jnp.mean(x * x, axis=-1, keepdims=True) + RMS_EPS) * g


def _dot(a, b):
    return jnp.dot(a, b, preferred_element_type=F32)


def _dot_nt(a, b):
    return lax.dot_general(a, b, (((1,), (1,)), ((), ())), preferred_element_type=F32)


def _dot_tn(a, b):
    return lax.dot_general(a, b, (((0,), (0,)), ((), ())), preferred_element_type=F32)


def _split3(x):
    hi = x.astype(BF16)
    r1 = x - hi.astype(F32)
    mid = r1.astype(BF16)
    lo = (r1 - mid.astype(F32)).astype(BF16)
    return hi, mid, lo


def _unit_rows(n_ones, width):
    row = lax.broadcasted_iota(jnp.int32, (SUBLANES, width), 0)
    return jnp.where(row < n_ones, 1.0, 0.0).astype(BF16)


def _fold(x, op):
    parts = x.reshape(x.shape[0] // SUBLANES, SUBLANES, x.shape[1])
    return op(parts, axis=0)


def _ffn_body(h_ref, gpre_ref, gpost_ref, wg_ref, wu_ref, wd_ref, o_ref, xn_ref, acc_ref):
    k = pl.program_id(1)

    @pl.when(k == 0)
    def _():
        xn_ref[...] = _rms(h_ref[...], gpre_ref[...]).astype(BF16)
        acc_ref[...] = jnp.zeros_like(acc_ref)

    xn = xn_ref[...]
    g = _dot(xn, wg_ref[...])
    u = _dot(xn, wu_ref[...])
    a = (g * jax.nn.sigmoid(g) * u).astype(BF16)
    acc_ref[...] += _dot(a, wd_ref[...])

    @pl.when(k == pl.num_programs(1) - 1)
    def _():
        o_ref[...] = h_ref[...] + 0.5 * _rms(acc_ref[...], gpost_ref[...])


def _ffn(h, g_pre, g_post, wg, wu, wd, layer, half, tm=1024, tf=1408):
    n, d = h.shape
    f = wg.shape[-1]
    row = lambda i, k: (i, 0)
    vec = lambda i, k: (0, 0)
    return pl.pallas_call(
        _ffn_body,
        grid=(n // tm, f // tf),
        in_specs=[
            pl.BlockSpec((tm, d), row),
            pl.BlockSpec((1, d), vec),
            pl.BlockSpec((1, d), vec),
            pl.BlockSpec((None, None, d, tf), lambda i, k: (layer, half, 0, k)),
            pl.BlockSpec((None, None, d, tf), lambda i, k: (layer, half, 0, k)),
            pl.BlockSpec((None, None, tf, d), lambda i, k: (layer, half, k, 0)),
        ],
        out_specs=pl.BlockSpec((tm, d), row),
        out_shape=jax.ShapeDtypeStruct((n, d), F32),
        scratch_shapes=[pltpu.VMEM((tm, d), BF16), pltpu.VMEM((tm, d), F32)],
        compiler_params=_cparams("parallel", "arbitrary"),
        name="ffn",
    )(h, g_pre, g_post, wg, wu, wd)


def _outproj_body(ot_ref, w_ref, h_ref, g_ref, out_ref):
    y = _dot_tn(ot_ref[0], w_ref[...])
    out_ref[0] = h_ref[0] + _rms(y, g_ref[...])


def _outproj(o_t, w, h3, g, layer, tm=512):
    b, s, d = h3.shape
    kdim = o_t.shape[1]
    return pl.pallas_call(
        _outproj_body,
        grid=(b, s // tm),
        in_specs=[
            pl.BlockSpec((1, kdim, tm), lambda i, j: (i, 0, j)),
            pl.BlockSpec((None, kdim, d), lambda i, j: (layer, 0, 0)),
            pl.BlockSpec((1, tm, d), lambda i, j: (i, j, 0)),
            pl.BlockSpec((1, d), lambda i, j: (0, 0)),
        ],
        out_specs=pl.BlockSpec((1, tm, d), lambda i, j: (i, j, 0)),
        out_shape=jax.ShapeDtypeStruct((b, s, d), F32),
        compiler_params=_cparams("parallel", "parallel"),
        name="outproj",
    )(o_t, w, h3, g)


def _nsa_proj_body(h_ref, g_ref, w_ref, wt_ref, qt_ref, k_ref, vt_ref, gate_ref):
    xn = _rms(h_ref[0], g_ref[...]).astype(BF16)
    res = _dot(xn, w_ref[...])
    for a in range(4 * KV_GROUPS):
        k_ref[0, a] = res[:, a * HEAD_DIM:(a + 1) * HEAD_DIM].astype(BF16)
    res_t = _dot_nt(wt_ref[...], xn)
    for hd in range(N_HEADS):
        qt_ref[0, hd] = (res_t[hd * HEAD_DIM:(hd + 1) * HEAD_DIM] * Q_SCALE).astype(BF16)
    extra = _unit_rows(1, res_t.shape[1])
    for a in range(2 * KV_GROUPS):
        lo = Q_DIM + a * HEAD_DIM
        vt_ref[0, a] = jnp.concatenate([res_t[lo:lo + HEAD_DIM].astype(BF16), extra], axis=0)
    gates = jax.nn.sigmoid(res_t[Q_DIM + 2 * KV_DIM:])
    width = 3 * Q_PER_GROUP
    for grp in range(KV_GROUPS):
        gate_ref[0, grp] = gates[grp * width:(grp + 1) * width]


def _nsa_proj(h3, g, w, w_t, layer, tm=512):
    b, s, d = h3.shape
    return pl.pallas_call(
        _nsa_proj_body,
        grid=(b, s // tm),
        in_specs=[
            pl.BlockSpec((1, tm, d), lambda i, j: (i, j, 0)),
            pl.BlockSpec((1, d), lambda i, j: (0, 0)),
            pl.BlockSpec((None, d, w.shape[-1]), lambda i, j: (layer, 0, 0)),
            pl.BlockSpec((None, w_t.shape[1], d), lambda i, j: (layer, 0, 0)),
        ],
        out_specs=[
            pl.BlockSpec((1, N_HEADS, HEAD_DIM, tm), lambda i, j: (i, 0, 0, j)),
            pl.BlockSpec((1, 4 * KV_GROUPS, tm, HEAD_DIM), lambda i, j: (i, 0, j, 0)),
            pl.BlockSpec((1, 2 * KV_GROUPS, AUG_DIM, tm), lambda i, j: (i, 0, 0, j)),
            pl.BlockSpec((1, KV_GROUPS, 3 * Q_PER_GROUP, tm), lambda i, j: (i, 0, 0, j)),
        ],
        out_shape=[
            jax.ShapeDtypeStruct((b, N_HEADS, HEAD_DIM, s), BF16),
            jax.ShapeDtypeStruct((b, 4 * KV_GROUPS, s, HEAD_DIM), BF16),
            jax.ShapeDtypeStruct((b, 2 * KV_GROUPS, AUG_DIM, s), BF16),
            jax.ShapeDtypeStruct((b, KV_GROUPS, 3 * Q_PER_GROUP, s), F32),
        ],
        compiler_params=_cparams("parallel", "parallel"),
        name="nsa_proj",
    )(h3, g, w, w_t)


def _compress_body(x_ref, pe_ref, w1_ref, b1_ref, w2_ref, w2t_ref, o_ref, ot_ref):
    n_chunk = x_ref.shape[3]
    half = CMP_STRIDE * HEAD_DIM
    x = x_ref[0, 0].reshape(KV_GROUPS * n_chunk, half)
    top = _dot(x, w1_ref[:half, :])
    bot = _dot(x, w1_ref[half:, :])
    bot_next = pltpu.roll(bot, KV_GROUPS * n_chunk - 1, 0)
    pe = jnp.broadcast_to(pe_ref[...].astype(BF16), (8, 2 * half))
    const = _dot(pe, w1_ref[...])[0:1] + b1_ref[...]
    hid = jax.nn.gelu(top + bot_next + const).astype(BF16)
    out = _dot(hid, w2_ref[...])
    row = lax.broadcasted_iota(jnp.int32, out.shape, 0) & (n_chunk - 1)
    o_ref[0, 0] = jnp.where(row < n_chunk - 1, out, 0.0).reshape(KV_GROUPS, n_chunk, HEAD_DIM).astype(BF16)
    out_t = _dot_nt(w2t_ref[...], hid)
    col = lax.broadcasted_iota(jnp.int32, out_t.shape, 1) & (n_chunk - 1)
    out_t = jnp.where(col < n_chunk - 1, out_t, 0.0).astype(BF16)
    for grp in range(KV_GROUPS):
        ot_ref[0, 0, grp] = out_t[:, grp * n_chunk:(grp + 1) * n_chunk]


def _compress(kk, pe, w1, b1, w2, w2t, layer):
    b, _, s, dh = kk.shape
    n_chunk = s // CMP_STRIDE
    x = kk[:, :2 * KV_GROUPS].reshape(b, 2, KV_GROUPS, n_chunk, CMP_STRIDE * dh)
    hidden = w1.shape[-1]
    return pl.pallas_call(
        _compress_body,
        grid=(b, 2),
        in_specs=[
            pl.BlockSpec((1, 1, KV_GROUPS, n_chunk, CMP_STRIDE * dh), lambda i, a: (i, a, 0, 0, 0)),
            pl.BlockSpec((None, None, 1, CMP_BLOCK * dh), lambda i, a: (layer, a, 0, 0)),
            pl.BlockSpec((None, None, CMP_BLOCK * dh, hidden), lambda i, a: (layer, a, 0, 0)),
            pl.BlockSpec((None, None, 1, hidden), lambda i, a: (layer, a, 0, 0)),
            pl.BlockSpec((None, None, hidden, dh), lambda i, a: (layer, a, 0, 0)),
            pl.BlockSpec((None, None, dh, hidden), lambda i, a: (layer, a, 0, 0)),
        ],
        out_specs=[
            pl.BlockSpec((1, 1, KV_GROUPS, n_chunk, dh), lambda i, a: (i, a, 0, 0, 0)),
            pl.BlockSpec((1, 1, KV_GROUPS, dh, n_chunk), lambda i, a: (i, a, 0, 0, 0)),
        ],
        out_shape=[
            jax.ShapeDtypeStruct((b, 2, KV_GROUPS, n_chunk, dh), BF16),
            jax.ShapeDtypeStruct((b, 2, KV_GROUPS, dh, n_chunk), BF16),
        ],
        compiler_params=_cparams("parallel", "parallel"),
        name="nsa_compress",
    )(x, pe, w1, b1, w2, w2t)


def _t5_bucket_np(rel):
    n = np.maximum(rel, 0)
    max_exact = NUM_BUCKETS // 2
    nf = np.maximum(n, 1).astype(np.float32)
    ratio = np.log(nf / np.float32(max_exact)) / np.float32(math.log(MAX_DISTANCE / max_exact))
    large = max_exact + (ratio * np.float32(NUM_BUCKETS - max_exact)).astype(np.int32)
    large = np.minimum(large, NUM_BUCKETS - 1)
    return np.where(n < max_exact, n, large).astype(np.int32)


def _bucket_maps(s):
    n_chunk = s // CMP_STRIDE
    t = np.arange(s)[None, :]
    blk_end = np.arange(n_chunk)[:, None] * CMP_STRIDE + CMP_BLOCK - 1
    rel_c = t - blk_end
    map_c = np.where(rel_c >= 0, _t5_bucket_np(rel_c), -1).astype(np.int32)
    j = np.arange(ATT_TILE)[:, None]
    i = np.arange(ATT_TILE)[None, :]
    diag = np.where(i - j >= 0, _t5_bucket_np(i - j), -1)
    sub = _t5_bucket_np(ATT_TILE + i - j)
    edge = np.where(j > i, _t5_bucket_np(WINDOW + i - j), -1)
    map_t = np.stack([diag, sub, edge]).astype(np.int32)
    assert _t5_bucket_np(np.arange(ATT_TILE + 1, s + WINDOW)).min() == _FAR_BUCKET
    return map_c, map_t


_FAR_BUCKET = NUM_BUCKETS - 1
TILE_DIAG, TILE_SUB, TILE_EDGE = 0, 1, 2


def _bias_body(rb_ref, mc_ref, mt_ref, bc_ref, bt_ref):
    hd = pl.program_id(0)

    def lookup(bucket, shift):
        acc = jnp.zeros(bucket.shape, F32)
        for bk in range(NUM_BUCKETS):
            acc = jnp.where(bucket == bk, (rb_ref[bk, hd] - shift) * LOG2E, acc)
        return jnp.where(bucket < 0, NEG_INF, acc)

    bc_ref[0] = lookup(mc_ref[...], 0.0)
    for d in range(mt_ref.shape[0]):
        bt_ref[d, 0] = lookup(mt_ref[d], rb_ref[_FAR_BUCKET, hd])


def _bias_tables(rel_bias, s):
    map_c, map_t = _bucket_maps(s)
    n_chunk = map_c.shape[0]
    n_tab = map_t.shape[0]
    T = ATT_TILE
    R = Q_PER_GROUP
    return pl.pallas_call(
        _bias_body,
        grid=(N_HEADS,),
        in_specs=[
            pl.BlockSpec(memory_space=pltpu.SMEM),
            pl.BlockSpec((n_chunk, s), lambda i: (0, 0)),
            pl.BlockSpec((n_tab, T, T), lambda i: (0, 0, 0)),
        ],
        out_specs=[
            pl.BlockSpec((1, n_chunk, s), lambda i: (i, 0, 0)),
            pl.BlockSpec((n_tab, 1, T, T), lambda i: (0, i // R, 0, i % R)),
        ],
        out_shape=[
            jax.ShapeDtypeStruct((N_HEADS, n_chunk, s), F32),
            jax.ShapeDtypeStruct((n_tab, KV_GROUPS, T, R * T), F32),
        ],
        compiler_params=_cparams("parallel"),
        name="t5_bias_tables",
    )(rel_bias, jnp.asarray(map_c), jnp.asarray(map_t))


def _nsa_select_body(q_ref, kc_ref, vct_ref, bc_ref, ovl_ref, oc_ref, ch_ref):
    T = ATT_TILE
    R = Q_PER_GROUP
    n_sel = ovl_ref.shape[0]
    ovl = ovl_ref[...]
    j_blk = lax.broadcasted_iota(jnp.int32, (n_sel, T), 0)
    for u in range(q_ref.shape[3] // T):
        qi = pl.program_id(2) * (q_ref.shape[3] // T) + u
        cols = slice(u * T, (u + 1) * T)
        q_t = jnp.concatenate([q_ref[0, r, :, cols] for r in range(R)], axis=1)
        bias = jnp.concatenate([bc_ref[r, :, cols] for r in range(R)], axis=1)
        s_c = _dot(kc_ref[0, 0, 0], q_t) + bias
        m_c = jnp.max(s_c, axis=0, keepdims=True)
        p_c = jnp.exp2(s_c - m_c)
        p_c = p_c * (1.0 / jnp.sum(p_c, axis=0, keepdims=True))
        t_col = qi * T + (lax.broadcasted_iota(jnp.int32, (1, R * T), 1) & (T - 1))
        p_c = jnp.where(t_col >= CMP_BLOCK - 1, p_c, 0.0)
        o_c = _dot(vct_ref[0, 0, 0], p_c.astype(BF16))
        for r in range(R):
            oc_ref[0, r, :, cols] = o_c[:, r * T:(r + 1) * T]

        p_sum = p_c[:, :T]
        for r in range(1, R):
            p_sum = p_sum + p_c[:, r * T:(r + 1) * T]
        hi, mid, lo = _split3(p_sum)
        imp = _dot(ovl, hi) + _dot(ovl, mid) + _dot(ovl, lo)
        cur = (qi * T + lax.broadcasted_iota(jnp.int32, (n_sel, T), 1)) >> SEL_SHIFT
        forced = (j_blk == 0) | (j_blk == cur) | (j_blk == cur - 1)
        imp = jnp.where(forced, FORCED_SCORE, jnp.where(j_blk <= cur, imp, NEG_INF))
        rank = jnp.zeros((n_sel, T), F32)
        for i in range(n_sel):
            row = imp[i:i + 1, :]
            rank = rank + jnp.where(j_blk > i, jnp.where(row >= imp, 1.0, 0.0),
                                    jnp.where(row > imp, 1.0, 0.0))
        ch_ref[0, 0, :, cols] = jnp.where(rank < min(SEL_TOPK, n_sel), 1.0, 0.0)


def _nsa_select(q_t, k_cmp, v_cmp_t, bias_c, tiles_per_step=2):
    b, _, dh, s = q_t.shape
    R = Q_PER_GROUP
    n_chunk = s // CMP_STRIDE
    n_sel = s // SEL_BLOCK
    tq = tiles_per_step * ATT_TILE
    c_start = np.arange(n_chunk)[None, :] * CMP_STRIDE
    j = np.arange(n_sel)[:, None]
    overlap = (c_start < (j + 1) * SEL_BLOCK) & (c_start + CMP_BLOCK > j * SEL_BLOCK)
    overlap[:, n_chunk - 1] = False
    return pl.pallas_call(
        _nsa_select_body,
        grid=(b, KV_GROUPS, s // tq),
        in_specs=[
            pl.BlockSpec((1, R, dh, tq), lambda i, g, t: (i, g, 0, t)),
            pl.BlockSpec((1, 1, 1, n_chunk, dh), lambda i, g, t: (i, 0, g, 0, 0)),
            pl.BlockSpec((1, 1, 1, dh, n_chunk), lambda i, g, t: (i, 1, g, 0, 0)),
            pl.BlockSpec((R, n_chunk, tq), lambda i, g, t: (g, 0, t)),
            pl.BlockSpec((n_sel, n_chunk), lambda i, g, t: (0, 0)),
        ],
        out_specs=[
            pl.BlockSpec((1, R, dh, tq), lambda i, g, t: (i, g, 0, t)),
            pl.BlockSpec((1, 1, n_sel, tq), lambda i, g, t: (i, g, 0, t)),
        ],
        out_shape=[
            jax.ShapeDtypeStruct((b, N_HEADS, dh, s), F32),
            jax.ShapeDtypeStruct((b, KV_GROUPS, n_sel, s), F32),
        ],
        compiler_params=_cparams("parallel", "parallel", "parallel"),
        name="nsa_select",
    )(q_t, k_cmp, v_cmp_t, bias_c, jnp.asarray(overlap, BF16))


def _nsa_attn_body(q_ref, ks_ref, kw_ref, vst_ref, vwt_ref, bt_ref, gate_ref, oc_ref, ch_ref,
                   o_ref, rows_ref, s_ref, near_ref, win_ref, peak_ref, acc_ref):
    T = ATT_TILE
    R = Q_PER_GROUP
    C = FAR_CHUNK
    N = R * T
    groups = range(ks_ref.shape[1])
    qi = pl.program_id(2)
    n_far = jnp.maximum(qi - 1, 0)
    n_chunks = (n_far + C // T - 1) // (C // T)
    qs = [jnp.concatenate([q_ref[0, g * R + r] for r in range(R)], axis=1) for g in groups]

    def add_block_rows(g, which, s, off):
        first = off // SEL_BLOCK
        pieces = [s[b * SEL_BLOCK:(b + 1) * SEL_BLOCK] + rows_ref[g, which, pl.ds(first + b, 1), :]
                  for b in range(s.shape[0] // SEL_BLOCK)]
        return jnp.concatenate(pieces, axis=0)

    def near_logits(g, k_ref, d, table, masked):
        off = pl.multiple_of(jnp.maximum(qi - d, 0) * T, T)
        s = _dot(k_ref[0, g, pl.ds(off, T), :], qs[g])
        if table is not None:
            s = s + bt_ref[table, g]
        if masked:
            s = add_block_rows(g, 0, s, off)
        if d > 0:
            s = jnp.where(qi >= d, s, NEG_INF)
        return s, off

    def logits_pass(g, k_ref, tiles, masked, out_ref):
        offs = []
        peak = None
        for idx, (d, table) in enumerate(tiles):
            s, off = near_logits(g, k_ref, d, table, masked)
            out_ref[g, idx * T:(idx + 1) * T, :] = s
            part = _fold(s, jnp.max)
            peak = part if peak is None else jnp.maximum(peak, part)
            offs.append(off)
        return offs, peak

    def values_pass(g, logit_ref, vt_ref, offs, m):
        acc = None
        for idx, off in enumerate(offs):
            p = jnp.exp2((logit_ref[g, idx * T:(idx + 1) * T, :] - m).astype(BF16))
            pv = _dot(vt_ref[0, g, :, pl.ds(off, T)], p)
            acc = pv if acc is None else acc + pv
        return acc

    win_tiles = [(0, TILE_DIAG), (1, TILE_SUB)]
    win_tiles += [(d, None) for d in range(2, WIN_TILES)] + [(WIN_TILES, TILE_EDGE)]
    near_sel = [(1, TILE_SUB), (0, TILE_DIAG)]
    win_offs, m_win, sel_offs = [], [], []
    blk = lax.broadcasted_iota(jnp.int32, (ch_ref.shape[2], N), 0)
    for g in groups:
        add = (ch_ref[0, g] - 1.0) * -NEG_INF
        add = jnp.concatenate([add] * R, axis=1)
        rows_ref[g, 0] = add
        rows_ref[g, 1] = jnp.where(blk < n_far * (T // SEL_BLOCK), add, NEG_INF)
        offs, peak = logits_pass(g, kw_ref, win_tiles, False, win_ref)
        win_offs.append(offs)
        m_win.append(jnp.max(peak, axis=0, keepdims=True))
        offs, peak = logits_pass(g, ks_ref, near_sel, True, near_ref)
        sel_offs.append(offs)
        peak_ref[g] = peak

    def far_logits(c, carry):
        off = pl.multiple_of(c * C, C)
        for g in groups:
            s = _dot(ks_ref[0, g, pl.ds(off, C), :], qs[g])
            s = add_block_rows(g, 1, s, off)
            s_ref[g, pl.ds(off, C), :] = s
            peak_ref[g] = jnp.maximum(peak_ref[g], _fold(s, jnp.max))
        return carry

    lax.fori_loop(0, n_chunks, far_logits, 0)

    m_sel = []
    for g in groups:
        m_sel.append(jnp.max(peak_ref[g], axis=0, keepdims=True))
        acc_ref[g] = values_pass(g, near_ref, vst_ref, sel_offs[g], m_sel[g])

    def far_values(c, carry):
        off = pl.multiple_of(c * C, C)
        for g in groups:
            p = jnp.exp2((s_ref[g, pl.ds(off, C), :] - m_sel[g]).astype(BF16))
            acc_ref[g] += _dot(vst_ref[0, g, :, pl.ds(off, C)], p)
        return carry

    lax.fori_loop(0, n_chunks, far_values, 0)

    for g in groups:
        gate = gate_ref[0, g]
        gate_of = lambda br: jnp.concatenate(
            [gate[br * R + r:br * R + r + 1, :] for r in range(R)], axis=1)
        win = values_pass(g, win_ref, vwt_ref, win_offs[g], m_win[g])
        sel = acc_ref[g]
        w_win = gate_of(2) * (1.0 / win[HEAD_DIM:HEAD_DIM + 1])
        w_sel = gate_of(1) * (1.0 / sel[HEAD_DIM:HEAD_DIM + 1])
        o_c = jnp.concatenate([oc_ref[0, g * R + r] for r in range(R)], axis=1)
        o = gate_of(0) * o_c + w_sel * sel[:HEAD_DIM] + w_win * win[:HEAD_DIM]
        for r in range(R):
            hd = g * R + r
            o_ref[0, hd * HEAD_DIM:(hd + 1) * HEAD_DIM, :] = o[:, r * T:(r + 1) * T].astype(BF16)


def _nsa_attn(q_t, kk, v_t, o_cmp, chosen, bias_t, gates):
    b, _, dh, s = q_t.shape
    T = ATT_TILE
    R = Q_PER_GROUP
    N = R * T
    gp = GROUPS_PER_STEP
    per = KV_GROUPS // gp
    n_sel = s // SEL_BLOCK
    far_keys = max(s - 2 * T, FAR_CHUNK)
    once = dict(pipeline_mode=pl.Buffered(1))
    k_spec = lambda a: pl.BlockSpec((1, gp, s, dh), lambda i, g, t: (i, a * per + g, 0, 0))
    vt_spec = lambda a: pl.BlockSpec((1, gp, AUG_DIM, s), lambda i, g, t: (i, a * per + g, 0, 0))
    return pl.pallas_call(
        _nsa_attn_body,
        grid=(b, per, s // T),
        in_specs=[
            pl.BlockSpec((1, gp * R, dh, T), lambda i, g, t: (i, g, 0, t)),
            k_spec(2), k_spec(3), vt_spec(0), vt_spec(1),
            pl.BlockSpec((bias_t.shape[0], gp, T, N), lambda i, g, t: (0, g, 0, 0), **once),
            pl.BlockSpec((1, gp, 3 * R, T), lambda i, g, t: (i, g, 0, t)),
            pl.BlockSpec((1, gp * R, dh, T), lambda i, g, t: (i, g, 0, t)),
            pl.BlockSpec((1, gp, n_sel, T), lambda i, g, t: (i, g, 0, t)),
        ],
        out_specs=pl.BlockSpec((1, gp * R * dh, T), lambda i, g, t: (i, g, t)),
        out_shape=jax.ShapeDtypeStruct((b, Q_DIM, s), BF16),
        scratch_shapes=[
            pltpu.VMEM((gp, 2, n_sel, N), F32),
            pltpu.VMEM((gp, far_keys, N), F32),
            pltpu.VMEM((gp, 2 * T, N), F32),
            pltpu.VMEM((gp, (WIN_TILES + 1) * T, N), F32),
            pltpu.VMEM((gp, SUBLANES, N), F32),
            pltpu.VMEM((gp, AUG_DIM, N), F32),
        ],
        compiler_params=_cparams("parallel", "parallel", "arbitrary"),
        name="nsa_attn",
    )(q_t, kk, kk, v_t, v_t, bias_t, gates, o_cmp, chosen)


def _fox_proj_body(h_ref, g_ref, wt_ref, bf_ref, q_ref, k_ref, vt_ref, carry_ref):
    tm = h_ref.shape[1]

    @pl.when(pl.program_id(1) == 0)
    def _():
        carry_ref[...] = jnp.zeros_like(carry_ref)

    xn = _rms(h_ref[0], g_ref[...]).astype(BF16)
    res_t = _dot_nt(wt_ref[...], xn)
    log_f = jax.nn.log_sigmoid(res_t[3 * Q_DIM:] + bf_ref[...])
    upper = jnp.where(lax.broadcasted_iota(jnp.int32, (tm, tm), 0)
                      <= lax.broadcasted_iota(jnp.int32, (tm, tm), 1), 1.0, 0.0).astype(BF16)
    hi, mid, lo = _split3(log_f)
    cum = _dot(hi, upper) + _dot(mid, upper) + _dot(lo, upper) + carry_ref[...]
    carry_ref[...] = cum[:, tm - 1:tm]
    terms = [t.astype(F32) for t in _split3(-LOG2E * cum)]
    pad = FOX_QK_DIM - HEAD_DIM
    k_zeros = jnp.zeros((pad - len(terms), tm), F32)
    q_extra = jnp.concatenate([_unit_rows(len(terms), tm), jnp.zeros((pad - SUBLANES, tm), BF16)], axis=0)
    v_extra = _unit_rows(1, tm)
    for hd in range(N_HEADS):
        rows = slice(hd * HEAD_DIM, (hd + 1) * HEAD_DIM)
        q_h = (res_t[rows] * Q_SCALE).astype(BF16)
        k_h = res_t[Q_DIM + hd * HEAD_DIM:Q_DIM + (hd + 1) * HEAD_DIM]
        v_h = res_t[2 * Q_DIM + hd * HEAD_DIM:2 * Q_DIM + (hd + 1) * HEAD_DIM].astype(BF16)
        k_t = jnp.concatenate([k_h] + [t[hd:hd + 1] for t in terms] + [k_zeros], axis=0)
        q_ref[0, hd] = jnp.concatenate([q_h, q_extra], axis=0)
        k_ref[0, hd] = k_t.T.astype(BF16)
        vt_ref[0, hd] = jnp.concatenate([v_h, v_extra], axis=0)


def _fox_proj(h3, g, w_t, b_f, layer, tm=512):
    b, s, d = h3.shape
    t_spec = lambda rows: pl.BlockSpec((1, N_HEADS, rows, tm), lambda i, j: (i, 0, 0, j))
    t_shape = lambda rows: jax.ShapeDtypeStruct((b, N_HEADS, rows, s), BF16)
    return pl.pallas_call(
        _fox_proj_body,
        grid=(b, s // tm),
        in_specs=[
            pl.BlockSpec((1, tm, d), lambda i, j: (i, j, 0)),
            pl.BlockSpec((1, d), lambda i, j: (0, 0)),
            pl.BlockSpec((None, w_t.shape[1], d), lambda i, j: (layer, 0, 0)),
            pl.BlockSpec((None, N_HEADS, 1), lambda i, j: (layer, 0, 0)),
        ],
        out_specs=[t_spec(FOX_QK_DIM),
                   pl.BlockSpec((1, N_HEADS, tm, FOX_QK_DIM), lambda i, j: (i, 0, j, 0)),
                   t_spec(AUG_DIM)],
        out_shape=[t_shape(FOX_QK_DIM),
                   jax.ShapeDtypeStruct((b, N_HEADS, s, FOX_QK_DIM), BF16),
                   t_shape(AUG_DIM)],
        scratch_shapes=[pltpu.VMEM((N_HEADS, 1), F32)],
        compiler_params=_cparams("parallel", "arbitrary"),
        name="fox_proj",
    )(h3, g, w_t, b_f)


def _fox_attn_body(q_ref, k_ref, vt_ref, o_ref, *scratch):
    T = FOX_TILE
    n_q = q_ref.shape[3] // T
    n_heads = q_ref.shape[1]
    s_refs, p_refs = scratch[:len(scratch) // 2], scratch[len(scratch) // 2:]
    causal = (lax.broadcasted_iota(jnp.int32, (T, T), 0)
              <= lax.broadcasted_iota(jnp.int32, (T, T), 1))
    for qi in range(n_q):
        cols = slice(qi * T, (qi + 1) * T)
        for hh in range(n_heads):
            s_ref = s_refs[(qi % FOX_SLOTS) * n_heads + hh]
            p_ref = p_refs[(qi % FOX_SLOTS) * n_heads + hh]
            q_t = q_ref[0, hh, :, cols]
            peak = None
            for kt in range(qi + 1):
                keys = slice(kt * T, (kt + 1) * T)
                s = _dot(k_ref[0, hh, keys, :], q_t)
                if kt == qi:
                    s = jnp.where(causal, s, NEG_INF)
                s_ref[keys, :] = s
                part = _fold(s, jnp.max)
                peak = part if peak is None else jnp.maximum(peak, part)
            m = jnp.max(peak, axis=0, keepdims=True)
            for kt in range(qi + 1):
                keys = slice(kt * T, (kt + 1) * T)
                p_ref[keys, :] = jnp.exp2((s_ref[keys, :] - m).astype(BF16))
            extent = (qi + 1) * T
            acc = _dot(vt_ref[0, hh, :, :extent], p_ref[:extent, :])
            o = acc[:HEAD_DIM] * (1.0 / acc[HEAD_DIM:HEAD_DIM + 1])
            o_ref[0, hh * HEAD_DIM:(hh + 1) * HEAD_DIM, cols] = o.astype(BF16)


def _fox_attn(q_t, k, v_t, heads_per_step=2):
    b, nh, kdim, s = q_t.shape
    dh = HEAD_DIM
    T = FOX_TILE
    hp = heads_per_step
    t_spec = lambda rows: pl.BlockSpec((1, hp, rows, s), lambda i, h: (i, h, 0, 0))
    return pl.pallas_call(
        _fox_attn_body,
        grid=(b, nh // hp),
        in_specs=[t_spec(kdim), pl.BlockSpec((1, hp, s, kdim), lambda i, h: (i, h, 0, 0)), t_spec(AUG_DIM)],
        out_specs=pl.BlockSpec((1, hp * dh, s), lambda i, h: (i, h, 0)),
        out_shape=jax.ShapeDtypeStruct((b, nh * dh, s), BF16),
        scratch_shapes=([pltpu.VMEM((s, T), F32)] * (FOX_SLOTS * hp)
                        + [pltpu.VMEM((s, T), BF16)] * (FOX_SLOTS * hp)),
        compiler_params=_cparams("parallel", "parallel"),
        name="fox_attn",
    )(q_t, k, v_t)


def kernel(x, norm_g, ffn_w_gate, ffn_w_up, ffn_w_down, rel_bias, nsa_w_in, nsa_cmp_pe, nsa_cmp_w1,
           nsa_cmp_b1, nsa_cmp_w2, nsa_w_out, fox_w_in, fox_b_f, fox_w_out):
    b, s, d = x.shape
    depth = norm_g.shape[0]
    n = b * s
    t_last = lambda w: jnp.swapaxes(w, -1, -2)
    wg, wu, wd = ffn_w_gate.astype(BF16), ffn_w_up.astype(BF16), ffn_w_down.astype(BF16)
    c0 = Q_DIM
    col = lambda a: nsa_w_in[:, :, c0 + a * KV_DIM:c0 + (a + 1) * KV_DIM]
    gate_cols = np.arange(N_GATES).reshape(3, KV_GROUPS, Q_PER_GROUP).transpose(1, 0, 2).reshape(-1)
    w_gates = nsa_w_in[:, :, c0 + 6 * KV_DIM:][:, :, gate_cols]
    nsa_w_k = jnp.concatenate([col(0), col(1), col(2), col(4)], axis=-1).astype(BF16)
    nsa_w_t = t_last(jnp.concatenate([nsa_w_in[:, :, :c0], col(3), col(5), w_gates], axis=-1)).astype(BF16)
    nsa_w_out_b = nsa_w_out.astype(BF16)
    fox_w_t = t_last(fox_w_in).astype(BF16)
    fox_w_out_b = fox_w_out.astype(BF16)
    cmp_w1_b, cmp_w2_b = nsa_cmp_w1.astype(BF16), nsa_cmp_w2.astype(BF16)
    cmp_w2t_b = t_last(nsa_cmp_w2).astype(BF16)
    cmp_pe = nsa_cmp_pe.reshape(nsa_cmp_pe.shape[0], 2, 1, CMP_BLOCK * HEAD_DIM)
    cmp_b1 = nsa_cmp_b1[:, :, None, :]
    fox_bf = fox_b_f[:, :, None]
    gains = norm_g[:, :, None, :]

    bias_c, bias_t = _bias_tables(rel_bias, s)

    h = x.reshape(n, d)
    for i in range(depth):
        g = gains[i]
        j = i // 2
        h = _ffn(h, g[0], g[1], wg, wu, wd, i, 0)
        h3 = h.reshape(b, s, d)
        if i % 2 == 0:
            q_t, kk, v_t, gates = _nsa_proj(h3, g[2], nsa_w_k, nsa_w_t, j)
            k_cmp, v_cmp_t = _compress(kk, cmp_pe, cmp_w1_b, cmp_b1, cmp_w2_b, cmp_w2t_b, j)
            o_cmp, chosen = _nsa_select(q_t, k_cmp, v_cmp_t, bias_c)
            o_t = _nsa_attn(q_t, kk, v_t, o_cmp, chosen, bias_t, gates)
            h3 = _outproj(o_t, nsa_w_out_b, h3, g[3], j)
        else:
            q_t, k, v_t = _fox_proj(h3, g[2], fox_w_t, fox_bf, j)
            o_t = _fox_attn(q_t, k, v_t)
            h3 = _outproj(o_t, fox_w_out_b, h3, g[3], j)
        h = _ffn(h3.reshape(n, d), g[4], g[5], wg, wu, wd, i, 1)
    return h.reshape(b, s, d)
```

```python
import math

import numpy as np
import jax
import jax.numpy as jnp
from jax import lax
from jax.experimental import pallas as pl
from jax.experimental.pallas import tpu as pltpu

N_HEADS = 16
HEAD_DIM = 64
KV_GROUPS = 4
Q_PER_GROUP = N_HEADS // KV_GROUPS
CMP_BLOCK = 32
CMP_STRIDE = 16
SEL_BLOCK = 64
SEL_SHIFT = 6
SEL_TOPK = 16
WINDOW = 512
NUM_BUCKETS = 32
MAX_DISTANCE = 128
RMS_EPS = 1e-6
NEG_INF = -1e30
FORCED_SCORE = 1e9
Q_DIM = N_HEADS * HEAD_DIM
KV_DIM = KV_GROUPS * HEAD_DIM
N_GATES = 3 * N_HEADS
LOG2E = math.log2(math.e)
Q_SCALE = HEAD_DIM ** -0.5 * LOG2E

SUBLANES = 8
ATT_TILE = 256
WIN_TILES = WINDOW // ATT_TILE
FAR_CHUNK = 512
GROUPS_PER_STEP = 2
FOX_TILE = 256
FOX_SLOTS = 2
AUG_DIM = HEAD_DIM + SUBLANES
FOX_QK_DIM = 2 * HEAD_DIM
VMEM_LIMIT = 56 * 1024 * 1024

BF16 = jnp.bfloat16
F32 = jnp.float32


def _cparams(*sem):
    return pltpu.CompilerParams(dimension_semantics=sem, vmem_limit_bytes=VMEM_LIMIT)


def _rms(x, g):
    return x * lax.rsqrt(jnp.mean(x * x, axis=-1, keepdims=True) + RMS_EPS) * g


def _dot(a, b):
    return jnp.dot(a, b, preferred_element_type=F32)


def _dot_nt(a, b):
    return lax.dot_general(a, b, (((1,), (1,)), ((), ())), preferred_element_type=F32)


def _dot_tn(a, b):
    return lax.dot_general(a, b, (((0,), (0,)), ((), ())), preferred_element_type=F32)


def _split3(x):
    hi = x.astype(BF16)
    r1 = x - hi.astype(F32)
    mid = r1.astype(BF16)
    lo = (r1 - mid.astype(F32)).astype(BF16)
    return hi, mid, lo


def _unit_rows(n_ones, width):
    row = lax.broadcasted_iota(jnp.int32, (SUBLANES, width), 0)
    return jnp.where(row < n_ones, 1.0, 0.0).astype(BF16)


def _fold(x, op):
    parts = x.reshape(x.shape[0] // SUBLANES, SUBLANES, x.shape[1])
    return op(parts, axis=0)


def _ffn_body(h_ref, gpre_ref, gpost_ref, wg_ref, wu_ref, wd_ref, o_ref, xn_ref, acc_ref):
    k = pl.program_id(1)

    @pl.when(k == 0)
    def _():
        xn_ref[...] = _rms(h_ref[...], gpre_ref[...]).astype(BF16)
        acc_ref[...] = jnp.zeros_like(acc_ref)

    xn = xn_ref[...]
    g = _dot(xn, wg_ref[...])
    u = _dot(xn, wu_ref[...])
    a = (g * jax.nn.sigmoid(g) * u).astype(BF16)
    acc_ref[...] += _dot(a, wd_ref[...])

    @pl.when(k == pl.num_programs(1) - 1)
    def _():
        o_ref[...] = h_ref[...] + 0.5 * _rms(acc_ref[...], gpost_ref[...])


def _ffn(h, g_pre, g_post, wg, wu, wd, layer, half, tm=1024, tf=1408):
    n, d = h.shape
    f = wg.shape[-1]
    row = lambda i, k: (i, 0)
    vec = lambda i, k: (0, 0)
    return pl.pallas_call(
        _ffn_body,
        grid=(n // tm, f // tf),
        in_specs=[
            pl.BlockSpec((tm, d), row),
            pl.BlockSpec((1, d), vec),
            pl.BlockSpec((1, d), vec),
            pl.BlockSpec((None, None, d, tf), lambda i, k: (layer, half, 0, k)),
            pl.BlockSpec((None, None, d, tf), lambda i, k: (layer, half, 0, k)),
            pl.BlockSpec((None, None, tf, d), lambda i, k: (layer, half, k, 0)),
        ],
        out_specs=pl.BlockSpec((tm, d), row),
        out_shape=jax.ShapeDtypeStruct((n, d), F32),
        scratch_shapes=[pltpu.VMEM((tm, d), BF16), pltpu.VMEM((tm, d), F32)],
        compiler_params=_cparams("parallel", "arbitrary"),
        name="ffn",
    )(h, g_pre, g_post, wg, wu, wd)


def _outproj_body(ot_ref, w_ref, h_ref, g_ref, out_ref):
    y = _dot_tn(ot_ref[0], w_ref[...])
    out_ref[0] = h_ref[0] + _rms(y, g_ref[...])


def _outproj(o_t, w, h3, g, layer, tm=512):
    b, s, d = h3.shape
    kdim = o_t.shape[1]
    return pl.pallas_call(
        _outproj_body,
        grid=(b, s // tm),
        in_specs=[
            pl.BlockSpec((1, kdim, tm), lambda i, j: (i, 0, j)),
            pl.BlockSpec((None, kdim, d), lambda i, j: (layer, 0, 0)),
            pl.BlockSpec((1, tm, d), lambda i, j: (i, j, 0)),
            pl.BlockSpec((1, d), lambda i, j: (0, 0)),
        ],
        out_specs=pl.BlockSpec((1, tm, d), lambda i, j: (i, j, 0)),
        out_shape=jax.ShapeDtypeStruct((b, s, d), F32),
        compiler_params=_cparams("parallel", "parallel"),
        name="outproj",
    )(o_t, w, h3, g)


def _nsa_proj_body(h_ref, g_ref, w_ref, wt_ref, qt_ref, k_ref, vt_ref, gate_ref):
    xn = _rms(h_ref[0], g_ref[...]).astype(BF16)
    res = _dot(xn, w_ref[...])
    for a in range(4 * KV_GROUPS):
        k_ref[0, a] = res[:, a * HEAD_DIM:(a + 1) * HEAD_DIM].astype(BF16)
    res_t = _dot_nt(wt_ref[...], xn)
    for hd in range(N_HEADS):
        qt_ref[0, hd] = (res_t[hd * HEAD_DIM:(hd + 1) * HEAD_DIM] * Q_SCALE).astype(BF16)
    extra = _unit_rows(1, res_t.shape[1])
    for a in range(2 * KV_GROUPS):
        lo = Q_DIM + a * HEAD_DIM
        vt_ref[0, a] = jnp.concatenate([res_t[lo:lo + HEAD_DIM].astype(BF16), extra], axis=0)
    gates = jax.nn.sigmoid(res_t[Q_DIM + 2 * KV_DIM:])
    width = 3 * Q_PER_GROUP
    for grp in range(KV_GROUPS):
        gate_ref[0, grp] = gates[grp * width:(grp + 1) * width]


def _nsa_proj(h3, g, w, w_t, layer, tm=512):
    b, s, d = h3.shape
    return pl.pallas_call(
        _nsa_proj_body,
        grid=(b, s // tm),
        in_specs=[
            pl.BlockSpec((1, tm, d), lambda i, j: (i, j, 0)),
            pl.BlockSpec((1, d), lambda i, j: (0, 0)),
            pl.BlockSpec((None, d, w.shape[-1]), lambda i, j: (layer, 0, 0)),
            pl.BlockSpec((None, w_t.shape[1], d), lambda i, j: (layer, 0, 0)),
        ],
        out_specs=[
            pl.BlockSpec((1, N_HEADS, HEAD_DIM, tm), lambda i, j: (i, 0, 0, j)),
            pl.BlockSpec((1, 4 * KV_GROUPS, tm, HEAD_DIM), lambda i, j: (i, 0, j, 0)),
            pl.BlockSpec((1, 2 * KV_GROUPS, AUG_DIM, tm), lambda i, j: (i, 0, 0, j)),
            pl.BlockSpec((1, KV_GROUPS, 3 * Q_PER_GROUP, tm), lambda i, j: (i, 0, 0, j)),
        ],
        out_shape=[
            jax.ShapeDtypeStruct((b, N_HEADS, HEAD_DIM, s), BF16),
            jax.ShapeDtypeStruct((b, 4 * KV_GROUPS, s, HEAD_DIM), BF16),
            jax.ShapeDtypeStruct((b, 2 * KV_GROUPS, AUG_DIM, s), BF16),
            jax.ShapeDtypeStruct((b, KV_GROUPS, 3 * Q_PER_GROUP, s), F32),
        ],
        compiler_params=_cparams("parallel", "parallel"),
        name="nsa_proj",
    )(h3, g, w, w_t)


def _compress_body(x_ref, pe_ref, w1_ref, b1_ref, w2_ref, w2t_ref, o_ref, ot_ref):
    n_chunk = x_ref.shape[3]
    half = CMP_STRIDE * HEAD_DIM
    x = x_ref[0, 0].reshape(KV_GROUPS * n_chunk, half)
    top = _dot(x, w1_ref[:half, :])
    bot = _dot(x, w1_ref[half:, :])
    bot_next = pltpu.roll(bot, KV_GROUPS * n_chunk - 1, 0)
    pe = jnp.broadcast_to(pe_ref[...].astype(BF16), (8, 2 * half))
    const = _dot(pe, w1_ref[...])[0:1] + b1_ref[...]
    hid = jax.nn.gelu(top + bot_next + const).astype(BF16)
    out = _dot(hid, w2_ref[...])
    row = lax.broadcasted_iota(jnp.int32, out.shape, 0) & (n_chunk - 1)
    o_ref[0, 0] = jnp.where(row < n_chunk - 1, out, 0.0).reshape(KV_GROUPS, n_chunk, HEAD_DIM).astype(BF16)
    out_t = _dot_nt(w2t_ref[...], hid)
    col = lax.broadcasted_iota(jnp.int32, out_t.shape, 1) & (n_chunk - 1)
    out_t = jnp.where(col < n_chunk - 1, out_t, 0.0).astype(BF16)
    for grp in range(KV_GROUPS):
        ot_ref[0, 0, grp] = out_t[:, grp * n_chunk:(grp + 1) * n_chunk]


def _compress(kk, pe, w1, b1, w2, w2t, layer):
    b, _, s, dh = kk.shape
    n_chunk = s // CMP_STRIDE
    x = kk[:, :2 * KV_GROUPS].reshape(b, 2, KV_GROUPS, n_chunk, CMP_STRIDE * dh)
    hidden = w1.shape[-1]
    return pl.pallas_call(
        _compress_body,
        grid=(b, 2),
        in_specs=[
            pl.BlockSpec((1, 1, KV_GROUPS, n_chunk, CMP_STRIDE * dh), lambda i, a: (i, a, 0, 0, 0)),
            pl.BlockSpec((None, None, 1, CMP_BLOCK * dh), lambda i, a: (layer, a, 0, 0)),
            pl.BlockSpec((None, None, CMP_BLOCK * dh, hidden), lambda i, a: (layer, a, 0, 0)),
            pl.BlockSpec((None, None, 1, hidden), lambda i, a: (layer, a, 0, 0)),
            pl.BlockSpec((None, None, hidden, dh), lambda i, a: (layer, a, 0, 0)),
            pl.BlockSpec((None, None, dh, hidden), lambda i, a: (layer, a, 0, 0)),
        ],
        out_specs=[
            pl.BlockSpec((1, 1, KV_GROUPS, n_chunk, dh), lambda i, a: (i, a, 0, 0, 0)),
            pl.BlockSpec((1, 1, KV_GROUPS, dh, n_chunk), lambda i, a: (i, a, 0, 0, 0)),
        ],
        out_shape=[
            jax.ShapeDtypeStruct((b, 2, KV_GROUPS, n_chunk, dh), BF16),
            jax.ShapeDtypeStruct((b, 2, KV_GROUPS, dh, n_chunk), BF16),
        ],
        compiler_params=_cparams("parallel", "parallel"),
        name="nsa_compress",
    )(x, pe, w1, b1, w2, w2t)


def _t5_bucket_np(rel):
    n = np.maximum(rel, 0)
    max_exact = NUM_BUCKETS // 2
    nf = np.maximum(n, 1).astype(np.float32)
    ratio = np.log(nf / np.float32(max_exact)) / np.float32(math.log(MAX_DISTANCE / max_exact))
    large = max_exact + (ratio * np.float32(NUM_BUCKETS - max_exact)).astype(np.int32)
    large = np.minimum(large, NUM_BUCKETS - 1)
    return np.where(n < max_exact, n, large).astype(np.int32)


def _bucket_maps(s):
    n_chunk = s // CMP_STRIDE
    t = np.arange(s)[None, :]
    blk_end = np.arange(n_chunk)[:, None] * CMP_STRIDE + CMP_BLOCK - 1
    rel_c = t - blk_end
    map_c = np.where(rel_c >= 0, _t5_bucket_np(rel_c), -1).astype(np.int32)
    j = np.arange(ATT_TILE)[:, None]
    i = np.arange(ATT_TILE)[None, :]
    diag = np.where(i - j >= 0, _t5_bucket_np(i - j), -1)
    sub = _t5_bucket_np(ATT_TILE + i - j)
    edge = np.where(j > i, _t5_bucket_np(WINDOW + i - j), -1)
    map_t = np.stack([diag, sub, edge]).astype(np.int32)
    assert _t5_bucket_np(np.arange(ATT_TILE + 1, s + WINDOW)).min() == _FAR_BUCKET
    return map_c, map_t


_FAR_BUCKET = NUM_BUCKETS - 1
TILE_DIAG, TILE_SUB, TILE_EDGE = 0, 1, 2


def _bias_body(rb_ref, mc_ref, mt_ref, bc_ref, bt_ref):
    hd = pl.program_id(0)

    def lookup(bucket, shift):
        acc = jnp.zeros(bucket.shape, F32)
        for bk in range(NUM_BUCKETS):
            acc = jnp.where(bucket == bk, (rb_ref[bk, hd] - shift) * LOG2E, acc)
        return jnp.where(bucket < 0, NEG_INF, acc)

    bc_ref[0] = lookup(mc_ref[...], 0.0)
    for d in range(mt_ref.shape[0]):
        bt_ref[d, 0] = lookup(mt_ref[d], rb_ref[_FAR_BUCKET, hd])


def _bias_tables(rel_bias, s):
    map_c, map_t = _bucket_maps(s)
    n_chunk = map_c.shape[0]
    n_tab = map_t.shape[0]
    T = ATT_TILE
    R = Q_PER_GROUP
    return pl.pallas_call(
        _bias_body,
        grid=(N_HEADS,),
        in_specs=[
            pl.BlockSpec(memory_space=pltpu.SMEM),
            pl.BlockSpec((n_chunk, s), lambda i: (0, 0)),
            pl.BlockSpec((n_tab, T, T), lambda i: (0, 0, 0)),
        ],
        out_specs=[
            pl.BlockSpec((1, n_chunk, s), lambda i: (i, 0, 0)),
            pl.BlockSpec((n_tab, 1, T, T), lambda i: (0, i // R, 0, i % R)),
        ],
        out_shape=[
            jax.ShapeDtypeStruct((N_HEADS, n_chunk, s), F32),
            jax.ShapeDtypeStruct((n_tab, KV_GROUPS, T, R * T), F32),
        ],
        compiler_params=_cparams("parallel"),
        name="t5_bias_tables",
    )(rel_bias, jnp.asarray(map_c), jnp.asarray(map_t))


def _nsa_select_body(q_ref, kc_ref, vct_ref, bc_ref, ovl_ref, oc_ref, ch_ref):
    T = ATT_TILE
    R = Q_PER_GROUP
    n_sel = ovl_ref.shape[0]
    ovl = ovl_ref[...]
    j_blk = lax.broadcasted_iota(jnp.int32, (n_sel, T), 0)
    j_slab = lax.broadcasted_iota(jnp.int32, (SUBLANES, T), 0)
    for u in range(q_ref.shape[3] // T):
        qi = pl.program_id(1) * (q_ref.shape[3] // T) + u
        cols = slice(u * T, (u + 1) * T)
        q_t = jnp.concatenate([q_ref[0, r, :, cols] for r in range(R)], axis=1)
        bias = jnp.concatenate([bc_ref[r, :, cols] for r in range(R)], axis=1)
        s_c = _dot(kc_ref[0, 0, 0], q_t) + bias
        m_c = jnp.max(s_c, axis=0, keepdims=True)
        p_c = jnp.exp2(s_c - m_c)
        p_c = p_c * (1.0 / jnp.sum(p_c, axis=0, keepdims=True))
        t_col = qi * T + (lax.broadcasted_iota(jnp.int32, (1, R * T), 1) & (T - 1))
        p_c = jnp.where(t_col >= CMP_BLOCK - 1, p_c, 0.0)
        o_c = _dot(vct_ref[0, 0, 0], p_c.astype(BF16))
        for r in range(R):
            oc_ref[0, r, :, cols] = o_c[:, r * T:(r + 1) * T]

        p_sum = p_c[:, :T]
        for r in range(1, R):
            p_sum = p_sum + p_c[:, r * T:(r + 1) * T]
        hi, mid, lo = _split3(p_sum)
        imp = _dot(ovl, hi) + _dot(ovl, mid) + _dot(ovl, lo)
        cur = (qi * T + lax.broadcasted_iota(jnp.int32, (n_sel, T), 1)) >> SEL_SHIFT
        forced = (j_blk == 0) | (j_blk == cur) | (j_blk == cur - 1)
        imp = jnp.where(forced, FORCED_SCORE, jnp.where(j_blk <= cur, imp, NEG_INF))
        slabs = [imp[lo:lo + SUBLANES] for lo in range(0, n_sel, SUBLANES)]
        ranks = [jnp.zeros((SUBLANES, T), F32) for _ in slabs]
        for i in range(n_sel):
            row = imp[i:i + 1, :]
            for k, slab in enumerate(slabs):
                lo = k * SUBLANES
                if lo > i:
                    hit = jnp.where(row >= slab, 1.0, 0.0)
                elif lo + SUBLANES - 1 <= i:
                    hit = jnp.where(row > slab, 1.0, 0.0)
                else:
                    hit = jnp.where(j_slab > i - lo, jnp.where(row >= slab, 1.0, 0.0),
                                    jnp.where(row > slab, 1.0, 0.0))
                ranks[k] = ranks[k] + hit
        rank = jnp.concatenate(ranks, axis=0)
        ch_ref[0, 0, :, cols] = jnp.where(rank < min(SEL_TOPK, n_sel), 1.0, 0.0)


def _nsa_select(q_t, k_cmp, v_cmp_t, bias_c, tiles_per_step=2):
    b, _, dh, s = q_t.shape
    R = Q_PER_GROUP
    n_chunk = s // CMP_STRIDE
    n_sel = s // SEL_BLOCK
    tq = tiles_per_step * ATT_TILE
    c_start = np.arange(n_chunk)[None, :] * CMP_STRIDE
    j = np.arange(n_sel)[:, None]
    overlap = (c_start < (j + 1) * SEL_BLOCK) & (c_start + CMP_BLOCK > j * SEL_BLOCK)
    overlap[:, n_chunk - 1] = False
    return pl.pallas_call(
        _nsa_select_body,
        grid=(KV_GROUPS, s // tq, b),
        in_specs=[
            pl.BlockSpec((1, R, dh, tq), lambda g, t, i: (i, g, 0, t)),
            pl.BlockSpec((1, 1, 1, n_chunk, dh), lambda g, t, i: (i, 0, g, 0, 0)),
            pl.BlockSpec((1, 1, 1, dh, n_chunk), lambda g, t, i: (i, 1, g, 0, 0)),
            pl.BlockSpec((R, n_chunk, tq), lambda g, t, i: (g, 0, t)),
            pl.BlockSpec((n_sel, n_chunk), lambda g, t, i: (0, 0)),
        ],
        out_specs=[
            pl.BlockSpec((1, R, dh, tq), lambda g, t, i: (i, g, 0, t)),
            pl.BlockSpec((1, 1, n_sel, tq), lambda g, t, i: (i, g, 0, t)),
        ],
        out_shape=[
            jax.ShapeDtypeStruct((b, N_HEADS, dh, s), F32),
            jax.ShapeDtypeStruct((b, KV_GROUPS, n_sel, s), F32),
        ],
        compiler_params=_cparams("parallel", "parallel", "parallel"),
        name="nsa_select",
    )(q_t, k_cmp, v_cmp_t, bias_c, jnp.asarray(overlap, BF16))


def _nsa_attn_body(q_ref, ks_ref, kw_ref, vst_ref, vwt_ref, bt_ref, gate_ref, oc_ref, ch_ref,
                   o_ref, rows_ref, s_ref, near_ref, win_ref, peak_ref, acc_ref):
    T = ATT_TILE
    R = Q_PER_GROUP
    C = FAR_CHUNK
    N = R * T
    groups = range(ks_ref.shape[1])
    qi = pl.program_id(2)
    n_far = jnp.maximum(qi - 1, 0)
    n_chunks = (n_far + C // T - 1) // (C // T)
    qs = [jnp.concatenate([q_ref[0, g * R + r] for r in range(R)], axis=1) for g in groups]

    def add_block_rows(g, which, s, off):
        first = off // SEL_BLOCK
        pieces = [s[b * SEL_BLOCK:(b + 1) * SEL_BLOCK] + rows_ref[g, which, pl.ds(first + b, 1), :]
                  for b in range(s.shape[0] // SEL_BLOCK)]
        return jnp.concatenate(pieces, axis=0)

    def near_logits(g, k_ref, d, table, masked):
        off = pl.multiple_of(jnp.maximum(qi - d, 0) * T, T)
        s = _dot(k_ref[0, g, pl.ds(off, T), :], qs[g])
        if table is not None:
            s = s + bt_ref[table, g]
        if masked:
            s = add_block_rows(g, 0, s, off)
        if d > 0:
            s = jnp.where(qi >= d, s, NEG_INF)
        return s, off

    def logits_pass(g, k_ref, tiles, masked, out_ref):
        offs = []
        peak = None
        for idx, (d, table) in enumerate(tiles):
            s, off = near_logits(g, k_ref, d, table, masked)
            out_ref[g, idx * T:(idx + 1) * T, :] = s
            part = _fold(s, jnp.max)
            peak = part if peak is None else jnp.maximum(peak, part)
            offs.append(off)
        return offs, peak

    def values_pass(g, logit_ref, vt_ref, offs, m):
        acc = None
        for idx, off in enumerate(offs):
            p = jnp.exp2((logit_ref[g, idx * T:(idx + 1) * T, :] - m).astype(BF16))
            pv = _dot(vt_ref[0, g, :, pl.ds(off, T)], p)
            acc = pv if acc is None else acc + pv
        return acc

    win_tiles = [(0, TILE_DIAG), (1, TILE_SUB)]
    win_tiles += [(d, None) for d in range(2, WIN_TILES)] + [(WIN_TILES, TILE_EDGE)]
    near_sel = [(1, TILE_SUB), (0, TILE_DIAG)]
    win_offs, m_win, sel_offs = [], [], []
    blk = lax.broadcasted_iota(jnp.int32, (ch_ref.shape[2], N), 0)
    for g in groups:
        add = (ch_ref[0, g] - 1.0) * -NEG_INF
        add = jnp.concatenate([add] * R, axis=1)
        rows_ref[g, 0] = add
        rows_ref[g, 1] = jnp.where(blk < n_far * (T // SEL_BLOCK), add, NEG_INF)
        offs, peak = logits_pass(g, kw_ref, win_tiles, False, win_ref)
        win_offs.append(offs)
        m_win.append(jnp.max(peak, axis=0, keepdims=True))
        offs, peak = logits_pass(g, ks_ref, near_sel, True, near_ref)
        sel_offs.append(offs)
        peak_ref[g] = peak

    def far_logits(c, carry):
        off = pl.multiple_of(c * C, C)
        for g in groups:
            s = _dot(ks_ref[0, g, pl.ds(off, C), :], qs[g])
            s = add_block_rows(g, 1, s, off)
            s_ref[g, pl.ds(off, C), :] = s
            peak_ref[g] = jnp.maximum(peak_ref[g], _fold(s, jnp.max))
        return carry

    lax.fori_loop(0, n_chunks, far_logits, 0)

    m_sel = []
    for g in groups:
        m_sel.append(jnp.max(peak_ref[g], axis=0, keepdims=True))
        acc_ref[g] = values_pass(g, near_ref, vst_ref, sel_offs[g], m_sel[g])

    def far_values(c, carry):
        off = pl.multiple_of(c * C, C)
        for g in groups:
            p = jnp.exp2((s_ref[g, pl.ds(off, C), :] - m_sel[g]).astype(BF16))
            acc_ref[g] += _dot(vst_ref[0, g, :, pl.ds(off, C)], p)
        return carry

    lax.fori_loop(0, n_chunks, far_values, 0)

    for g in groups:
        gate = gate_ref[0, g]
        gate_of = lambda br: jnp.concatenate(
            [gate[br * R + r:br * R + r + 1, :] for r in range(R)], axis=1)
        win = values_pass(g, win_ref, vwt_ref, win_offs[g], m_win[g])
        sel = acc_ref[g]
        w_win = gate_of(2) * (1.0 / win[HEAD_DIM:HEAD_DIM + 1])
        w_sel = gate_of(1) * (1.0 / sel[HEAD_DIM:HEAD_DIM + 1])
        o_c = jnp.concatenate([oc_ref[0, g * R + r] for r in range(R)], axis=1)
        o = gate_of(0) * o_c + w_sel * sel[:HEAD_DIM] + w_win * win[:HEAD_DIM]
        for r in range(R):
            hd = g * R + r
            o_ref[0, hd * HEAD_DIM:(hd + 1) * HEAD_DIM, :] = o[:, r * T:(r + 1) * T].astype(BF16)


def _nsa_attn(q_t, kk, v_t, o_cmp, chosen, bias_t, gates):
    b, _, dh, s = q_t.shape
    T = ATT_TILE
    R = Q_PER_GROUP
    N = R * T
    gp = GROUPS_PER_STEP
    per = KV_GROUPS // gp
    n_sel = s // SEL_BLOCK
    far_keys = max(s - 2 * T, FAR_CHUNK)
    once = dict(pipeline_mode=pl.Buffered(1))
    k_spec = lambda a: pl.BlockSpec((1, gp, s, dh), lambda g, i, t: (i, a * per + g, 0, 0))
    vt_spec = lambda a: pl.BlockSpec((1, gp, AUG_DIM, s), lambda g, i, t: (i, a * per + g, 0, 0))
    return pl.pallas_call(
        _nsa_attn_body,
        grid=(per, b, s // T),
        in_specs=[
            pl.BlockSpec((1, gp * R, dh, T), lambda g, i, t: (i, g, 0, t)),
            k_spec(2), k_spec(3), vt_spec(0), vt_spec(1),
            pl.BlockSpec((bias_t.shape[0], gp, T, N), lambda g, i, t: (0, g, 0, 0), **once),
            pl.BlockSpec((1, gp, 3 * R, T), lambda g, i, t: (i, g, 0, t)),
            pl.BlockSpec((1, gp * R, dh, T), lambda g, i, t: (i, g, 0, t)),
            pl.BlockSpec((1, gp, n_sel, T), lambda g, i, t: (i, g, 0, t)),
        ],
        out_specs=pl.BlockSpec((1, gp * R * dh, T), lambda g, i, t: (i, g, t)),
        out_shape=jax.ShapeDtypeStruct((b, Q_DIM, s), BF16),
        scratch_shapes=[
            pltpu.VMEM((gp, 2, n_sel, N), F32),
            pltpu.VMEM((gp, far_keys, N), F32),
            pltpu.VMEM((gp, 2 * T, N), F32),
            pltpu.VMEM((gp, (WIN_TILES + 1) * T, N), F32),
            pltpu.VMEM((gp, SUBLANES, N), F32),
            pltpu.VMEM((gp, AUG_DIM, N), F32),
        ],
        compiler_params=_cparams("parallel", "parallel", "arbitrary"),
        name="nsa_attn",
    )(q_t, kk, kk, v_t, v_t, bias_t, gates, o_cmp, chosen)


def _fox_proj_body(h_ref, g_ref, wt_ref, bf_ref, q_ref, k_ref, vt_ref, carry_ref):
    tm = h_ref.shape[1]

    @pl.when(pl.program_id(1) == 0)
    def _():
        carry_ref[...] = jnp.zeros_like(carry_ref)

    xn = _rms(h_ref[0], g_ref[...]).astype(BF16)
    res_t = _dot_nt(wt_ref[...], xn)
    log_f = jax.nn.log_sigmoid(res_t[3 * Q_DIM:] + bf_ref[...])
    upper = jnp.where(lax.broadcasted_iota(jnp.int32, (tm, tm), 0)
                      <= lax.broadcasted_iota(jnp.int32, (tm, tm), 1), 1.0, 0.0).astype(BF16)
    hi, mid, lo = _split3(log_f)
    cum = _dot(hi, upper) + _dot(mid, upper) + _dot(lo, upper) + carry_ref[...]
    carry_ref[...] = cum[:, tm - 1:tm]
    terms = [t.astype(F32) for t in _split3(-LOG2E * cum)]
    pad = FOX_QK_DIM - HEAD_DIM
    k_zeros = jnp.zeros((pad - len(terms), tm), F32)
    q_extra = jnp.concatenate([_unit_rows(len(terms), tm), jnp.zeros((pad - SUBLANES, tm), BF16)], axis=0)
    v_extra = _unit_rows(1, tm)
    for hd in range(N_HEADS):
        rows = slice(hd * HEAD_DIM, (hd + 1) * HEAD_DIM)
        q_h = (res_t[rows] * Q_SCALE).astype(BF16)
        k_h = res_t[Q_DIM + hd * HEAD_DIM:Q_DIM + (hd + 1) * HEAD_DIM]
        v_h = res_t[2 * Q_DIM + hd * HEAD_DIM:2 * Q_DIM + (hd + 1) * HEAD_DIM].astype(BF16)
        k_t = jnp.concatenate([k_h] + [t[hd:hd + 1] for t in terms] + [k_zeros], axis=0)
        q_ref[0, hd] = jnp.concatenate([q_h, q_extra], axis=0)
        k_ref[0, hd] = k_t.T.astype(BF16)
        vt_ref[0, hd] = jnp.concatenate([v_h, v_extra], axis=0)


def _fox_proj(h3, g, w_t, b_f, layer, tm=512):
    b, s, d = h3.shape
    t_spec = lambda rows: pl.BlockSpec((1, N_HEADS, rows, tm), lambda i, j: (i, 0, 0, j))
    t_shape = lambda rows: jax.ShapeDtypeStruct((b, N_HEADS, rows, s), BF16)
    return pl.pallas_call(
        _fox_proj_body,
        grid=(b, s // tm),
        in_specs=[
            pl.BlockSpec((1, tm, d), lambda i, j: (i, j, 0)),
            pl.BlockSpec((1, d), lambda i, j: (0, 0)),
            pl.BlockSpec((None, w_t.shape[1], d), lambda i, j: (layer, 0, 0)),
            pl.BlockSpec((None, N_HEADS, 1), lambda i, j: (layer, 0, 0)),
        ],
        out_specs=[t_spec(FOX_QK_DIM),
                   pl.BlockSpec((1, N_HEADS, tm, FOX_QK_DIM), lambda i, j: (i, 0, j, 0)),
                   t_spec(AUG_DIM)],
        out_shape=[t_shape(FOX_QK_DIM),
                   jax.ShapeDtypeStruct((b, N_HEADS, s, FOX_QK_DIM), BF16),
                   t_shape(AUG_DIM)],
        scratch_shapes=[pltpu.VMEM((N_HEADS, 1), F32)],
        compiler_params=_cparams("parallel", "arbitrary"),
        name="fox_proj",
    )(h3, g, w_t, b_f)


def _fox_attn_body(q_ref, k_ref, vt_ref, o_ref, *scratch):
    T = FOX_TILE
    n_q = q_ref.shape[3] // T
    n_heads = q_ref.shape[1]
    s_refs, p_refs = scratch[:len(scratch) // 2], scratch[len(scratch) // 2:]
    causal = (lax.broadcasted_iota(jnp.int32, (T, T), 0)
              <= lax.broadcasted_iota(jnp.int32, (T, T), 1))
    for qi in range(n_q):
        cols = slice(qi * T, (qi + 1) * T)
        for hh in range(n_heads):
            s_ref = s_refs[(qi % FOX_SLOTS) * n_heads + hh]
            p_ref = p_refs[(qi % FOX_SLOTS) * n_heads + hh]
            q_t = q_ref[0, hh, :, cols]
            peak = None
            for kt in range(qi + 1):
                keys = slice(kt * T, (kt + 1) * T)
                s = _dot(k_ref[0, hh, keys, :], q_t)
                if kt == qi:
                    s = jnp.where(causal, s, NEG_INF)
                s_ref[keys, :] = s
                part = _fold(s, jnp.max)
                peak = part if peak is None else jnp.maximum(peak, part)
            m = jnp.max(peak, axis=0, keepdims=True)
            for kt in range(qi + 1):
                keys = slice(kt * T, (kt + 1) * T)
                p_ref[keys, :] = jnp.exp2((s_ref[keys, :] - m).astype(BF16))
            extent = (qi + 1) * T
            acc = _dot(vt_ref[0, hh, :, :extent], p_ref[:extent, :])
            o = acc[:HEAD_DIM] * (1.0 / acc[HEAD_DIM:HEAD_DIM + 1])
            o_ref[0, hh * HEAD_DIM:(hh + 1) * HEAD_DIM, cols] = o.astype(BF16)


def _fox_attn(q_t, k, v_t, heads_per_step=2):
    b, nh, kdim, s = q_t.shape
    dh = HEAD_DIM
    T = FOX_TILE
    hp = heads_per_step
    t_spec = lambda rows: pl.BlockSpec((1, hp, rows, s), lambda i, h: (i, h, 0, 0))
    return pl.pallas_call(
        _fox_attn_body,
        grid=(b, nh // hp),
        in_specs=[t_spec(kdim), pl.BlockSpec((1, hp, s, kdim), lambda i, h: (i, h, 0, 0)), t_spec(AUG_DIM)],
        out_specs=pl.BlockSpec((1, hp * dh, s), lambda i, h: (i, h, 0)),
        out_shape=jax.ShapeDtypeStruct((b, nh * dh, s), BF16),
        scratch_shapes=([pltpu.VMEM((s, T), F32)] * (FOX_SLOTS * hp)
                        + [pltpu.VMEM((s, T), BF16)] * (FOX_SLOTS * hp)),
        compiler_params=_cparams("parallel", "parallel"),
        name="fox_attn",
    )(q_t, k, v_t)


def kernel(x, norm_g, ffn_w_gate, ffn_w_up, ffn_w_down, rel_bias, nsa_w_in, nsa_cmp_pe, nsa_cmp_w1,
           nsa_cmp_b1, nsa_cmp_w2, nsa_w_out, fox_w_in, fox_b_f, fox_w_out):
    b, s, d = x.shape
    depth = norm_g.shape[0]
    n = b * s
    t_last = lambda w: jnp.swapaxes(w, -1, -2)
    wg, wu, wd = ffn_w_gate.astype(BF16), ffn_w_up.astype(BF16), ffn_w_down.astype(BF16)
    c0 = Q_DIM
    col = lambda a: nsa_w_in[:, :, c0 + a * KV_DIM:c0 + (a + 1) * KV_DIM]
    gate_cols = np.arange(N_GATES).reshape(3, KV_GROUPS, Q_PER_GROUP).transpose(1, 0, 2).reshape(-1)
    w_gates = nsa_w_in[:, :, c0 + 6 * KV_DIM:][:, :, gate_cols]
    nsa_w_k = jnp.concatenate([col(0), col(1), col(2), col(4)], axis=-1).astype(BF16)
    nsa_w_t = t_last(jnp.concatenate([nsa_w_in[:, :, :c0], col(3), col(5), w_gates], axis=-1)).astype(BF16)
    nsa_w_out_b = nsa_w_out.astype(BF16)
    fox_w_t = t_last(fox_w_in).astype(BF16)
    fox_w_out_b = fox_w_out.astype(BF16)
    cmp_w1_b, cmp_w2_b = nsa_cmp_w1.astype(BF16), nsa_cmp_w2.astype(BF16)
    cmp_w2t_b = t_last(nsa_cmp_w2).astype(BF16)
    cmp_pe = nsa_cmp_pe.reshape(nsa_cmp_pe.shape[0], 2, 1, CMP_BLOCK * HEAD_DIM)
    cmp_b1 = nsa_cmp_b1[:, :, None, :]
    fox_bf = fox_b_f[:, :, None]
    gains = norm_g[:, :, None, :]

    bias_c, bias_t = _bias_tables(rel_bias, s)

    h = x.reshape(n, d)
    for i in range(depth):
        g = gains[i]
        j = i // 2
        h = _ffn(h, g[0], g[1], wg, wu, wd, i, 0)
        h3 = h.reshape(b, s, d)
        if i % 2 == 0:
            q_t, kk, v_t, gates = _nsa_proj(h3, g[2], nsa_w_k, nsa_w_t, j)
            k_cmp, v_cmp_t = _compress(kk, cmp_pe, cmp_w1_b, cmp_b1, cmp_w2_b, cmp_w2t_b, j)
            o_cmp, chosen = _nsa_select(q_t, k_cmp, v_cmp_t, bias_c)
            o_t = _nsa_attn(q_t, kk, v_t, o_cmp, chosen, bias_t, gates)
            h3 = _outproj(o_t, nsa_w_out_b, h3, g[3], j)
        else:
            q_t, k, v_t = _fox_proj(h3, g[2], fox_w_t, fox_bf, j)
            o_t = _fox_attn(q_t, k, v_t)
            h3 = _outproj(o_t, fox_w_out_b, h3, g[3], j)
        h = _ffn(h3.reshape(n, d), g[4], g[5], wg, wu, wd, i, 1)
    return h.reshape(b, s, d)
```

```python
import math

import numpy as np
import jax
import jax.numpy as jnp
from jax import lax
from jax.experimental import pallas as pl
from jax.experimental.pallas import tpu as pltpu

N_HEADS = 16
HEAD_DIM = 64
KV_GROUPS = 4
Q_PER_GROUP = N_HEADS // KV_GROUPS
CMP_BLOCK = 32
CMP_STRIDE = 16
SEL_BLOCK = 64
SEL_SHIFT = 6
SEL_TOPK = 16
WINDOW = 512
NUM_BUCKETS = 32
MAX_DISTANCE = 128
RMS_EPS = 1e-6
NEG_INF = -1e30
FORCED_SCORE = 1e9
Q_DIM = N_HEADS * HEAD_DIM
KV_DIM = KV_GROUPS * HEAD_DIM
N_GATES = 3 * N_HEADS
LOG2E = math.log2(math.e)
Q_SCALE = HEAD_DIM ** -0.5 * LOG2E

SUBLANES = 8
ATT_TILE = 256
WIN_TILES = WINDOW // ATT_TILE
FAR_CHUNK = 512
GROUPS_PER_STEP = 2
FOX_TILE = 256
FOX_SLOTS = 2
AUG_DIM = HEAD_DIM + SUBLANES
FOX_QK_DIM = 2 * HEAD_DIM
VMEM_LIMIT = 56 * 1024 * 1024

BF16 = jnp.bfloat16
F32 = jnp.float32


def _cparams(*sem):
    return pltpu.CompilerParams(dimension_semantics=sem, vmem_limit_bytes=VMEM_LIMIT)


def _rms(x, g):
    return x * lax.rsqrt(jnp.mean(x * x, axis=-1, keepdims=True) + RMS_EPS) * g


def _dot(a, b):
    return jnp.dot(a, b, preferred_element_type=F32)


def _dot_nt(a, b):
    return lax.dot_general(a, b, (((1,), (1,)), ((), ())), preferred_element_type=F32)


def _dot_tn(a, b):
    return lax.dot_general(a, b, (((0,), (0,)), ((), ())), preferred_element_type=F32)


def _split3(x):
    hi = x.astype(BF16)
    r1 = x - hi.astype(F32)
    mid = r1.astype(BF16)
    lo = (r1 - mid.astype(F32)).astype(BF16)
    return hi, mid, lo


def _unit_rows(n_ones, width):
    row = lax.broadcasted_iota(jnp.int32, (SUBLANES, width), 0)
    return jnp.where(row < n_ones, 1.0, 0.0).astype(BF16)


def _fold(x, op):
    parts = x.reshape(x.shape[0] // SUBLANES, SUBLANES, x.shape[1])
    return op(parts, axis=0)


def _ffn_body(h_ref, gpre_ref, gpost_ref, wg_ref, wu_ref, wd_ref, o_ref):
    h = h_ref[...]
    xn = _rms(h, gpre_ref[...]).astype(BF16)
    g = _dot(xn, wg_ref[...])
    u = _dot(xn, wu_ref[...])
    a = (g * jax.nn.sigmoid(g) * u).astype(BF16)
    o_ref[...] = h + 0.5 * _rms(_dot(a, wd_ref[...]), gpost_ref[...])


def _ffn(h, g_pre, g_post, wg, wu, wd, layer, half, tm=512):
    n, d = h.shape
    f = wg.shape[-1]
    once = dict(pipeline_mode=pl.Buffered(1))
    return pl.pallas_call(
        _ffn_body,
        grid=(n // tm,),
        in_specs=[
            pl.BlockSpec((tm, d), lambda i: (i, 0)),
            pl.BlockSpec((1, d), lambda i: (0, 0)),
            pl.BlockSpec((1, d), lambda i: (0, 0)),
            pl.BlockSpec((None, None, d, f), lambda i: (layer, half, 0, 0), **once),
            pl.BlockSpec((None, None, d, f), lambda i: (layer, half, 0, 0), **once),
            pl.BlockSpec((None, None, f, d), lambda i: (layer, half, 0, 0), **once),
        ],
        out_specs=pl.BlockSpec((tm, d), lambda i: (i, 0)),
        out_shape=jax.ShapeDtypeStruct((n, d), F32),
        compiler_params=_cparams("parallel"),
        name="ffn",
    )(h, g_pre, g_post, wg, wu, wd)


def _outproj_body(ot_ref, w_ref, h_ref, g_ref, out_ref):
    y = _dot_tn(ot_ref[0], w_ref[...])
    out_ref[0] = h_ref[0] + _rms(y, g_ref[...])


def _outproj(o_t, w, h3, g, layer, tm=512):
    b, s, d = h3.shape
    kdim = o_t.shape[1]
    return pl.pallas_call(
        _outproj_body,
        grid=(b, s // tm),
        in_specs=[
            pl.BlockSpec((1, kdim, tm), lambda i, j: (i, 0, j)),
            pl.BlockSpec((None, kdim, d), lambda i, j: (layer, 0, 0)),
            pl.BlockSpec((1, tm, d), lambda i, j: (i, j, 0)),
            pl.BlockSpec((1, d), lambda i, j: (0, 0)),
        ],
        out_specs=pl.BlockSpec((1, tm, d), lambda i, j: (i, j, 0)),
        out_shape=jax.ShapeDtypeStruct((b, s, d), F32),
        compiler_params=_cparams("parallel", "parallel"),
        name="outproj",
    )(o_t, w, h3, g)


def _nsa_proj_body(h_ref, g_ref, w_ref, wt_ref, qt_ref, k_ref, vt_ref, gate_ref):
    xn = _rms(h_ref[0], g_ref[...]).astype(BF16)
    res = _dot(xn, w_ref[...])
    for a in range(4 * KV_GROUPS):
        k_ref[0, a] = res[:, a * HEAD_DIM:(a + 1) * HEAD_DIM].astype(BF16)
    res_t = _dot_nt(wt_ref[...], xn)
    for hd in range(N_HEADS):
        qt_ref[0, hd] = (res_t[hd * HEAD_DIM:(hd + 1) * HEAD_DIM] * Q_SCALE).astype(BF16)
    extra = _unit_rows(1, res_t.shape[1])
    for a in range(2 * KV_GROUPS):
        lo = Q_DIM + a * HEAD_DIM
        vt_ref[0, a] = jnp.concatenate([res_t[lo:lo + HEAD_DIM].astype(BF16), extra], axis=0)
    gates = jax.nn.sigmoid(res_t[Q_DIM + 2 * KV_DIM:])
    width = 3 * Q_PER_GROUP
    for grp in range(KV_GROUPS):
        gate_ref[0, grp] = gates[grp * width:(grp + 1) * width]


def _nsa_proj(h3, g, w, w_t, layer, tm=512):
    b, s, d = h3.shape
    return pl.pallas_call(
        _nsa_proj_body,
        grid=(b, s // tm),
        in_specs=[
            pl.BlockSpec((1, tm, d), lambda i, j: (i, j, 0)),
            pl.BlockSpec((1, d), lambda i, j: (0, 0)),
            pl.BlockSpec((None, d, w.shape[-1]), lambda i, j: (layer, 0, 0)),
            pl.BlockSpec((None, w_t.shape[1], d), lambda i, j: (layer, 0, 0)),
        ],
        out_specs=[
            pl.BlockSpec((1, N_HEADS, HEAD_DIM, tm), lambda i, j: (i, 0, 0, j)),
            pl.BlockSpec((1, 4 * KV_GROUPS, tm, HEAD_DIM), lambda i, j: (i, 0, j, 0)),
            pl.BlockSpec((1, 2 * KV_GROUPS, AUG_DIM, tm), lambda i, j: (i, 0, 0, j)),
            pl.BlockSpec((1, KV_GROUPS, 3 * Q_PER_GROUP, tm), lambda i, j: (i, 0, 0, j)),
        ],
        out_shape=[
            jax.ShapeDtypeStruct((b, N_HEADS, HEAD_DIM, s), BF16),
            jax.ShapeDtypeStruct((b, 4 * KV_GROUPS, s, HEAD_DIM), BF16),
            jax.ShapeDtypeStruct((b, 2 * KV_GROUPS, AUG_DIM, s), BF16),
            jax.ShapeDtypeStruct((b, KV_GROUPS, 3 * Q_PER_GROUP, s), F32),
        ],
        compiler_params=_cparams("parallel", "parallel"),
        name="nsa_proj",
    )(h3, g, w, w_t)


def _compress_body(x_ref, pe_ref, w1_ref, b1_ref, w2_ref, w2t_ref, o_ref, ot_ref):
    n_chunk = x_ref.shape[3]
    half = CMP_STRIDE * HEAD_DIM
    x = x_ref[0, 0].reshape(KV_GROUPS * n_chunk, half)
    top = _dot(x, w1_ref[:half, :])
    bot = _dot(x, w1_ref[half:, :])
    bot_next = pltpu.roll(bot, KV_GROUPS * n_chunk - 1, 0)
    pe = jnp.broadcast_to(pe_ref[...].astype(BF16), (8, 2 * half))
    const = _dot(pe, w1_ref[...])[0:1] + b1_ref[...]
    hid = jax.nn.gelu(top + bot_next + const).astype(BF16)
    out = _dot(hid, w2_ref[...])
    row = lax.broadcasted_iota(jnp.int32, out.shape, 0) & (n_chunk - 1)
    o_ref[0, 0] = jnp.where(row < n_chunk - 1, out, 0.0).reshape(KV_GROUPS, n_chunk, HEAD_DIM).astype(BF16)
    out_t = _dot_nt(w2t_ref[...], hid)
    col = lax.broadcasted_iota(jnp.int32, out_t.shape, 1) & (n_chunk - 1)
    out_t = jnp.where(col < n_chunk - 1, out_t, 0.0).astype(BF16)
    for grp in range(KV_GROUPS):
        ot_ref[0, 0, grp] = out_t[:, grp * n_chunk:(grp + 1) * n_chunk]


def _compress(kk, pe, w1, b1, w2, w2t, layer):
    b, _, s, dh = kk.shape
    n_chunk = s // CMP_STRIDE
    x = kk[:, :2 * KV_GROUPS].reshape(b, 2, KV_GROUPS, n_chunk, CMP_STRIDE * dh)
    hidden = w1.shape[-1]
    return pl.pallas_call(
        _compress_body,
        grid=(b, 2),
        in_specs=[
            pl.BlockSpec((1, 1, KV_GROUPS, n_chunk, CMP_STRIDE * dh), lambda i, a: (i, a, 0, 0, 0)),
            pl.BlockSpec((None, None, 1, CMP_BLOCK * dh), lambda i, a: (layer, a, 0, 0)),
            pl.BlockSpec((None, None, CMP_BLOCK * dh, hidden), lambda i, a: (layer, a, 0, 0)),
            pl.BlockSpec((None, None, 1, hidden), lambda i, a: (layer, a, 0, 0)),
            pl.BlockSpec((None, None, hidden, dh), lambda i, a: (layer, a, 0, 0)),
            pl.BlockSpec((None, None, dh, hidden), lambda i, a: (layer, a, 0, 0)),
        ],
        out_specs=[
            pl.BlockSpec((1, 1, KV_GROUPS, n_chunk, dh), lambda i, a: (i, a, 0, 0, 0)),
            pl.BlockSpec((1, 1, KV_GROUPS, dh, n_chunk), lambda i, a: (i, a, 0, 0, 0)),
        ],
        out_shape=[
            jax.ShapeDtypeStruct((b, 2, KV_GROUPS, n_chunk, dh), BF16),
            jax.ShapeDtypeStruct((b, 2, KV_GROUPS, dh, n_chunk), BF16),
        ],
        compiler_params=_cparams("parallel", "parallel"),
        name="nsa_compress",
    )(x, pe, w1, b1, w2, w2t)


def _t5_bucket_np(rel):
    n = np.maximum(rel, 0)
    max_exact = NUM_BUCKETS // 2
    nf = np.maximum(n, 1).astype(np.float32)
    ratio = np.log(nf / np.float32(max_exact)) / np.float32(math.log(MAX_DISTANCE / max_exact))
    large = max_exact + (ratio * np.float32(NUM_BUCKETS - max_exact)).astype(np.int32)
    large = np.minimum(large, NUM_BUCKETS - 1)
    return np.where(n < max_exact, n, large).astype(np.int32)


def _bucket_maps(s):
    n_chunk = s // CMP_STRIDE
    t = np.arange(s)[None, :]
    blk_end = np.arange(n_chunk)[:, None] * CMP_STRIDE + CMP_BLOCK - 1
    rel_c = t - blk_end
    map_c = np.where(rel_c >= 0, _t5_bucket_np(rel_c), -1).astype(np.int32)
    j = np.arange(ATT_TILE)[:, None]
    i = np.arange(ATT_TILE)[None, :]
    diag = np.where(i - j >= 0, _t5_bucket_np(i - j), -1)
    sub = _t5_bucket_np(ATT_TILE + i - j)
    edge = np.where(j > i, _t5_bucket_np(WINDOW + i - j), -1)
    map_t = np.stack([diag, sub, edge]).astype(np.int32)
    assert _t5_bucket_np(np.arange(ATT_TILE + 1, s + WINDOW)).min() == _FAR_BUCKET
    return map_c, map_t


_FAR_BUCKET = NUM_BUCKETS - 1
TILE_DIAG, TILE_SUB, TILE_EDGE = 0, 1, 2


def _bias_body(rb_ref, mc_ref, mt_ref, bc_ref, bt_ref):
    hd = pl.program_id(0)

    def lookup(bucket, shift):
        acc = jnp.zeros(bucket.shape, F32)
        for bk in range(NUM_BUCKETS):
            acc = jnp.where(bucket == bk, (rb_ref[bk, hd] - shift) * LOG2E, acc)
        return jnp.where(bucket < 0, NEG_INF, acc)

    bc_ref[0] = lookup(mc_ref[...], 0.0)
    for d in range(mt_ref.shape[0]):
        bt_ref[d, 0] = lookup(mt_ref[d], rb_ref[_FAR_BUCKET, hd])


def _bias_tables(rel_bias, s):
    map_c, map_t = _bucket_maps(s)
    n_chunk = map_c.shape[0]
    n_tab = map_t.shape[0]
    T = ATT_TILE
    R = Q_PER_GROUP
    return pl.pallas_call(
        _bias_body,
        grid=(N_HEADS,),
        in_specs=[
            pl.BlockSpec(memory_space=pltpu.SMEM),
            pl.BlockSpec((n_chunk, s), lambda i: (0, 0)),
            pl.BlockSpec((n_tab, T, T), lambda i: (0, 0, 0)),
        ],
        out_specs=[
            pl.BlockSpec((1, n_chunk, s), lambda i: (i, 0, 0)),
            pl.BlockSpec((n_tab, 1, T, T), lambda i: (0, i // R, 0, i % R)),
        ],
        out_shape=[
            jax.ShapeDtypeStruct((N_HEADS, n_chunk, s), F32),
            jax.ShapeDtypeStruct((n_tab, KV_GROUPS, T, R * T), F32),
        ],
        compiler_params=_cparams("parallel"),
        name="t5_bias_tables",
    )(rel_bias, jnp.asarray(map_c), jnp.asarray(map_t))


def _nsa_select_body(q_ref, kc_ref, vct_ref, bc_ref, ovl_ref, oc_ref, ch_ref):
    T = ATT_TILE
    R = Q_PER_GROUP
    n_sel = ovl_ref.shape[0]
    ovl = ovl_ref[...]
    j_blk = lax.broadcasted_iota(jnp.int32, (n_sel, T), 0)
    j_slab = lax.broadcasted_iota(jnp.int32, (SUBLANES, T), 0)
    for u in range(q_ref.shape[3] // T):
        qi = pl.program_id(1) * (q_ref.shape[3] // T) + u
        cols = slice(u * T, (u + 1) * T)
        q_t = jnp.concatenate([q_ref[0, r, :, cols] for r in range(R)], axis=1)
        bias = jnp.concatenate([bc_ref[r, :, cols] for r in range(R)], axis=1)
        s_c = _dot(kc_ref[0, 0, 0], q_t) + bias
        m_c = jnp.max(s_c, axis=0, keepdims=True)
        p_c = jnp.exp2(s_c - m_c)
        p_c = p_c * (1.0 / jnp.sum(p_c, axis=0, keepdims=True))
        t_col = qi * T + (lax.broadcasted_iota(jnp.int32, (1, R * T), 1) & (T - 1))
        p_c = jnp.where(t_col >= CMP_BLOCK - 1, p_c, 0.0)
        o_c = _dot(vct_ref[0, 0, 0], p_c.astype(BF16))
        for r in range(R):
            oc_ref[0, r, :, cols] = o_c[:, r * T:(r + 1) * T]

        p_sum = p_c[:, :T]
        for r in range(1, R):
            p_sum = p_sum + p_c[:, r * T:(r + 1) * T]
        hi, mid, lo = _split3(p_sum)
        imp = _dot(ovl, hi) + _dot(ovl, mid) + _dot(ovl, lo)
        cur = (qi * T + lax.broadcasted_iota(jnp.int32, (n_sel, T), 1)) >> SEL_SHIFT
        forced = (j_blk == 0) | (j_blk == cur) | (j_blk == cur - 1)
        imp = jnp.where(forced, FORCED_SCORE, jnp.where(j_blk <= cur, imp, NEG_INF))
        slabs = [imp[lo:lo + SUBLANES] for lo in range(0, n_sel, SUBLANES)]
        ranks = [jnp.zeros((SUBLANES, T), F32) for _ in slabs]
        for i in range(n_sel):
            row = imp[i:i + 1, :]
            for k, slab in enumerate(slabs):
                lo = k * SUBLANES
                if lo > i:
                    hit = jnp.where(row >= slab, 1.0, 0.0)
                elif lo + SUBLANES - 1 <= i:
                    hit = jnp.where(row > slab, 1.0, 0.0)
                else:
                    hit = jnp.where(j_slab > i - lo, jnp.where(row >= slab, 1.0, 0.0),
                                    jnp.where(row > slab, 1.0, 0.0))
                ranks[k] = ranks[k] + hit
        rank = jnp.concatenate(ranks, axis=0)
        ch_ref[0, 0, :, cols] = jnp.where(rank < min(SEL_TOPK, n_sel), 1.0, 0.0)


def _nsa_select(q_t, k_cmp, v_cmp_t, bias_c, tiles_per_step=2):
    b, _, dh, s = q_t.shape
    R = Q_PER_GROUP
    n_chunk = s // CMP_STRIDE
    n_sel = s // SEL_BLOCK
    tq = tiles_per_step * ATT_TILE
    c_start = np.arange(n_chunk)[None, :] * CMP_STRIDE
    j = np.arange(n_sel)[:, None]
    overlap = (c_start < (j + 1) * SEL_BLOCK) & (c_start + CMP_BLOCK > j * SEL_BLOCK)
    overlap[:, n_chunk - 1] = False
    return pl.pallas_call(
        _nsa_select_body,
        grid=(KV_GROUPS, s // tq, b),
        in_specs=[
            pl.BlockSpec((1, R, dh, tq), lambda g, t, i: (i, g, 0, t)),
            pl.BlockSpec((1, 1, 1, n_chunk, dh), lambda g, t, i: (i, 0, g, 0, 0)),
            pl.BlockSpec((1, 1, 1, dh, n_chunk), lambda g, t, i: (i, 1, g, 0, 0)),
            pl.BlockSpec((R, n_chunk, tq), lambda g, t, i: (g, 0, t)),
            pl.BlockSpec((n_sel, n_chunk), lambda g, t, i: (0, 0)),
        ],
        out_specs=[
            pl.BlockSpec((1, R, dh, tq), lambda g, t, i: (i, g, 0, t)),
            pl.BlockSpec((1, 1, n_sel, tq), lambda g, t, i: (i, g, 0, t)),
        ],
        out_shape=[
            jax.ShapeDtypeStruct((b, N_HEADS, dh, s), F32),
            jax.ShapeDtypeStruct((b, KV_GROUPS, n_sel, s), F32),
        ],
        compiler_params=_cparams("parallel", "parallel", "parallel"),
        name="nsa_select",
    )(q_t, k_cmp, v_cmp_t, bias_c, jnp.asarray(overlap, BF16))


def _nsa_attn_body(q_ref, ks_ref, kw_ref, vst_ref, vwt_ref, bt_ref, gate_ref, oc_ref, ch_ref,
                   o_ref, rows_ref, s_ref, near_ref, win_ref, peak_ref, acc_ref):
    T = ATT_TILE
    R = Q_PER_GROUP
    C = FAR_CHUNK
    N = R * T
    groups = range(ks_ref.shape[1])
    qi = pl.program_id(2)
    n_far = jnp.maximum(qi - 1, 0)
    n_chunks = (n_far + C // T - 1) // (C // T)
    qs = [jnp.concatenate([q_ref[0, g * R + r] for r in range(R)], axis=1) for g in groups]

    def add_block_rows(g, which, s, off):
        first = off // SEL_BLOCK
        pieces = [s[b * SEL_BLOCK:(b + 1) * SEL_BLOCK] + rows_ref[g, which, pl.ds(first + b, 1), :]
                  for b in range(s.shape[0] // SEL_BLOCK)]
        return jnp.concatenate(pieces, axis=0)

    def near_logits(g, k_ref, d, table, masked):
        off = pl.multiple_of(jnp.maximum(qi - d, 0) * T, T)
        s = _dot(k_ref[0, g, pl.ds(off, T), :], qs[g])
        if table is not None:
            s = s + bt_ref[table, g]
        if masked:
            s = add_block_rows(g, 0, s, off)
        if d > 0:
            s = jnp.where(qi >= d, s, NEG_INF)
        return s, off

    def logits_pass(g, k_ref, tiles, masked, out_ref):
        offs = []
        peak = None
        for idx, (d, table) in enumerate(tiles):
            s, off = near_logits(g, k_ref, d, table, masked)
            out_ref[g, idx * T:(idx + 1) * T, :] = s
            part = _fold(s, jnp.max)
            peak = part if peak is None else jnp.maximum(peak, part)
            offs.append(off)
        return offs, peak

    def values_pass(g, logit_ref, vt_ref, offs, m):
        acc = None
        for idx, off in enumerate(offs):
            p = jnp.exp2((logit_ref[g, idx * T:(idx + 1) * T, :] - m).astype(BF16))
            pv = _dot(vt_ref[0, g, :, pl.ds(off, T)], p)
            acc = pv if acc is None else acc + pv
        return acc

    win_tiles = [(0, TILE_DIAG), (1, TILE_SUB)]
    win_tiles += [(d, None) for d in range(2, WIN_TILES)] + [(WIN_TILES, TILE_EDGE)]
    near_sel = [(1, TILE_SUB), (0, TILE_DIAG)]
    win_offs, m_win, sel_offs = [], [], []
    blk = lax.broadcasted_iota(jnp.int32, (ch_ref.shape[2], N), 0)
    for g in groups:
        add = (ch_ref[0, g] - 1.0) * -NEG_INF
        add = jnp.concatenate([add] * R, axis=1)
        rows_ref[g, 0] = add
        rows_ref[g, 1] = jnp.where(blk < n_far * (T // SEL_BLOCK), add, NEG_INF)
        offs, peak = logits_pass(g, kw_ref, win_tiles, False, win_ref)
        win_offs.append(offs)
        m_win.append(jnp.max(peak, axis=0, keepdims=True))
        offs, peak = logits_pass(g, ks_ref, near_sel, True, near_ref)
        sel_offs.append(offs)
        peak_ref[g] = peak

    def far_logits(c, carry):
        off = pl.multiple_of(c * C, C)
        for g in groups:
            s = _dot(ks_ref[0, g, pl.ds(off, C), :], qs[g])
            s = add_block_rows(g, 1, s, off)
            s_ref[g, pl.ds(off, C), :] = s
            peak_ref[g] = jnp.maximum(peak_ref[g], _fold(s, jnp.max))
        return carry

    lax.fori_loop(0, n_chunks, far_logits, 0)

    m_sel = []
    for g in groups:
        m_sel.append(jnp.max(peak_ref[g], axis=0, keepdims=True))
        acc_ref[g] = values_pass(g, near_ref, vst_ref, sel_offs[g], m_sel[g])

    def far_values(c, carry):
        off = pl.multiple_of(c * C, C)
        for g in groups:
            p = jnp.exp2((s_ref[g, pl.ds(off, C), :] - m_sel[g]).astype(BF16))
            acc_ref[g] += _dot(vst_ref[0, g, :, pl.ds(off, C)], p)
        return carry

    lax.fori_loop(0, n_chunks, far_values, 0)

    for g in groups:
        gate = gate_ref[0, g]
        gate_of = lambda br: jnp.concatenate(
            [gate[br * R + r:br * R + r + 1, :] for r in range(R)], axis=1)
        win = values_pass(g, win_ref, vwt_ref, win_offs[g], m_win[g])
        sel = acc_ref[g]
        w_win = gate_of(2) * (1.0 / win[HEAD_DIM:HEAD_DIM + 1])
        w_sel = gate_of(1) * (1.0 / sel[HEAD_DIM:HEAD_DIM + 1])
        o_c = jnp.concatenate([oc_ref[0, g * R + r] for r in range(R)], axis=1)
        o = gate_of(0) * o_c + w_sel * sel[:HEAD_DIM] + w_win * win[:HEAD_DIM]
        for r in range(R):
            hd = g * R + r
            o_ref[0, hd * HEAD_DIM:(hd + 1) * HEAD_DIM, :] = o[:, r * T:(r + 1) * T].astype(BF16)


def _nsa_attn(q_t, kk, v_t, o_cmp, chosen, bias_t, gates):
    b, _, dh, s = q_t.shape
    T = ATT_TILE
    R = Q_PER_GROUP
    N = R * T
    gp = GROUPS_PER_STEP
    per = KV_GROUPS // gp
    n_sel = s // SEL_BLOCK
    far_keys = max(s - 2 * T, FAR_CHUNK)
    once = dict(pipeline_mode=pl.Buffered(1))
    k_spec = lambda a: pl.BlockSpec((1, gp, s, dh), lambda g, i, t: (i, a * per + g, 0, 0))
    vt_spec = lambda a: pl.BlockSpec((1, gp, AUG_DIM, s), lambda g, i, t: (i, a * per + g, 0, 0))
    return pl.pallas_call(
        _nsa_attn_body,
        grid=(per, b, s // T),
        in_specs=[
            pl.BlockSpec((1, gp * R, dh, T), lambda g, i, t: (i, g, 0, t)),
            k_spec(2), k_spec(3), vt_spec(0), vt_spec(1),
            pl.BlockSpec((bias_t.shape[0], gp, T, N), lambda g, i, t: (0, g, 0, 0), **once),
            pl.BlockSpec((1, gp, 3 * R, T), lambda g, i, t: (i, g, 0, t)),
            pl.BlockSpec((1, gp * R, dh, T), lambda g, i, t: (i, g, 0, t)),
            pl.BlockSpec((1, gp, n_sel, T), lambda g, i, t: (i, g, 0, t)),
        ],
        out_specs=pl.BlockSpec((1, gp * R * dh, T), lambda g, i, t: (i, g, t)),
        out_shape=jax.ShapeDtypeStruct((b, Q_DIM, s), BF16),
        scratch_shapes=[
            pltpu.VMEM((gp, 2, n_sel, N), F32),
            pltpu.VMEM((gp, far_keys, N), F32),
            pltpu.VMEM((gp, 2 * T, N), F32),
            pltpu.VMEM((gp, (WIN_TILES + 1) * T, N), F32),
            pltpu.VMEM((gp, SUBLANES, N), F32),
            pltpu.VMEM((gp, AUG_DIM, N), F32),
        ],
        compiler_params=_cparams("parallel", "parallel", "arbitrary"),
        name="nsa_attn",
    )(q_t, kk, kk, v_t, v_t, bias_t, gates, o_cmp, chosen)


def _fox_proj_body(h_ref, g_ref, wt_ref, bf_ref, q_ref, k_ref, vt_ref, carry_ref):
    tm = h_ref.shape[1]

    @pl.when(pl.program_id(1) == 0)
    def _():
        carry_ref[...] = jnp.zeros_like(carry_ref)

    xn = _rms(h_ref[0], g_ref[...]).astype(BF16)
    res_t = _dot_nt(wt_ref[...], xn)
    log_f = jax.nn.log_sigmoid(res_t[3 * Q_DIM:] + bf_ref[...])
    upper = jnp.where(lax.broadcasted_iota(jnp.int32, (tm, tm), 0)
                      <= lax.broadcasted_iota(jnp.int32, (tm, tm), 1), 1.0, 0.0).astype(BF16)
    hi, mid, lo = _split3(log_f)
    cum = _dot(hi, upper) + _dot(mid, upper) + _dot(lo, upper) + carry_ref[...]
    carry_ref[...] = cum[:, tm - 1:tm]
    terms = [t.astype(F32) for t in _split3(-LOG2E * cum)]
    pad = FOX_QK_DIM - HEAD_DIM
    k_zeros = jnp.zeros((pad - len(terms), tm), F32)
    q_extra = jnp.concatenate([_unit_rows(len(terms), tm), jnp.zeros((pad - SUBLANES, tm), BF16)], axis=0)
    v_extra = _unit_rows(1, tm)
    for hd in range(N_HEADS):
        rows = slice(hd * HEAD_DIM, (hd + 1) * HEAD_DIM)
        q_h = (res_t[rows] * Q_SCALE).astype(BF16)
        k_h = res_t[Q_DIM + hd * HEAD_DIM:Q_DIM + (hd + 1) * HEAD_DIM]
        v_h = res_t[2 * Q_DIM + hd * HEAD_DIM:2 * Q_DIM + (hd + 1) * HEAD_DIM].astype(BF16)
        k_t = jnp.concatenate([k_h] + [t[hd:hd + 1] for t in terms] + [k_zeros], axis=0)
        q_ref[0, hd] = jnp.concatenate([q_h, q_extra], axis=0)
        k_ref[0, hd] = k_t.T.astype(BF16)
        vt_ref[0, hd] = jnp.concatenate([v_h, v_extra], axis=0)


def _fox_proj(h3, g, w_t, b_f, layer, tm=512):
    b, s, d = h3.shape
    t_spec = lambda rows: pl.BlockSpec((1, N_HEADS, rows, tm), lambda i, j: (i, 0, 0, j))
    t_shape = lambda rows: jax.ShapeDtypeStruct((b, N_HEADS, rows, s), BF16)
    return pl.pallas_call(
        _fox_proj_body,
        grid=(b, s // tm),
        in_specs=[
            pl.BlockSpec((1, tm, d), lambda i, j: (i, j, 0)),
            pl.BlockSpec((1, d), lambda i, j: (0, 0)),
            pl.BlockSpec((None, w_t.shape[1], d), lambda i, j: (layer, 0, 0)),
            pl.BlockSpec((None, N_HEADS, 1), lambda i, j: (layer, 0, 0)),
        ],
        out_specs=[t_spec(FOX_QK_DIM),
                   pl.BlockSpec((1, N_HEADS, tm, FOX_QK_DIM), lambda i, j: (i, 0, j, 0)),
                   t_spec(AUG_DIM)],
        out_shape=[t_shape(FOX_QK_DIM),
                   jax.ShapeDtypeStruct((b, N_HEADS, s, FOX_QK_DIM), BF16),
                   t_shape(AUG_DIM)],
        scratch_shapes=[pltpu.VMEM((N_HEADS, 1), F32)],
        compiler_params=_cparams("parallel", "arbitrary"),
        name="fox_proj",
    )(h3, g, w_t, b_f)


def _fox_attn_body(q_ref, k_ref, vt_ref, o_ref, *scratch):
    T = FOX_TILE
    n_q = q_ref.shape[3] // T
    n_heads = q_ref.shape[1]
    s_refs, p_refs = scratch[:len(scratch) // 2], scratch[len(scratch) // 2:]
    causal = (lax.broadcasted_iota(jnp.int32, (T, T), 0)
              <= lax.broadcasted_iota(jnp.int32, (T, T), 1))
    for qi in range(n_q):
        cols = slice(qi * T, (qi + 1) * T)
        for hh in range(n_heads):
            s_ref = s_refs[(qi % FOX_SLOTS) * n_heads + hh]
            p_ref = p_refs[(qi % FOX_SLOTS) * n_heads + hh]
            q_t = q_ref[0, hh, :, cols]
            peak = None
            for kt in range(qi + 1):
                keys = slice(kt * T, (kt + 1) * T)
                s = _dot(k_ref[0, hh, keys, :], q_t)
                if kt == qi:
                    s = jnp.where(causal, s, NEG_INF)
                s_ref[keys, :] = s
                part = _fold(s, jnp.max)
                peak = part if peak is None else jnp.maximum(peak, part)
            m = jnp.max(peak, axis=0, keepdims=True)
            for kt in range(qi + 1):
                keys = slice(kt * T, (kt + 1) * T)
                p_ref[keys, :] = jnp.exp2((s_ref[keys, :] - m).astype(BF16))
            extent = (qi + 1) * T
            acc = _dot(vt_ref[0, hh, :, :extent], p_ref[:extent, :])
            o = acc[:HEAD_DIM] * (1.0 / acc[HEAD_DIM:HEAD_DIM + 1])
            o_ref[0, hh * HEAD_DIM:(hh + 1) * HEAD_DIM, cols] = o.astype(BF16)


def _fox_attn(q_t, k, v_t, heads_per_step=2):
    b, nh, kdim, s = q_t.shape
    dh = HEAD_DIM
    T = FOX_TILE
    hp = heads_per_step
    t_spec = lambda rows: pl.BlockSpec((1, hp, rows, s), lambda i, h: (i, h, 0, 0))
    return pl.pallas_call(
        _fox_attn_body,
        grid=(b, nh // hp),
        in_specs=[t_spec(kdim), pl.BlockSpec((1, hp, s, kdim), lambda i, h: (i, h, 0, 0)), t_spec(AUG_DIM)],
        out_specs=pl.BlockSpec((1, hp * dh, s), lambda i, h: (i, h, 0)),
        out_shape=jax.ShapeDtypeStruct((b, nh * dh, s), BF16),
        scratch_shapes=([pltpu.VMEM((s, T), F32)] * (FOX_SLOTS * hp)
                        + [pltpu.VMEM((s, T), BF16)] * (FOX_SLOTS * hp)),
        compiler_params=_cparams("parallel", "parallel"),
        name="fox_attn",
    )(q_t, k, v_t)


def kernel(x, norm_g, ffn_w_gate, ffn_w_up, ffn_w_down, rel_bias, nsa_w_in, nsa_cmp_pe, nsa_cmp_w1,
           nsa_cmp_b1, nsa_cmp_w2, nsa_w_out, fox_w_in, fox_b_f, fox_w_out):
    b, s, d = x.shape
    depth = norm_g.shape[0]
    n = b * s
    t_last = lambda w: jnp.swapaxes(w, -1, -2)
    wg, wu, wd = ffn_w_gate.astype(BF16), ffn_w_up.astype(BF16), ffn_w_down.astype(BF16)
    c0 = Q_DIM
    col = lambda a: nsa_w_in[:, :, c0 + a * KV_DIM:c0 + (a + 1) * KV_DIM]
    gate_cols = np.arange(N_GATES).reshape(3, KV_GROUPS, Q_PER_GROUP).transpose(1, 0, 2).reshape(-1)
    w_gates = nsa_w_in[:, :, c0 + 6 * KV_DIM:][:, :, gate_cols]
    nsa_w_k = jnp.concatenate([col(0), col(1), col(2), col(4)], axis=-1).astype(BF16)
    nsa_w_t = t_last(jnp.concatenate([nsa_w_in[:, :, :c0], col(3), col(5), w_gates], axis=-1)).astype(BF16)
    nsa_w_out_b = nsa_w_out.astype(BF16)
    fox_w_t = t_last(fox_w_in).astype(BF16)
    fox_w_out_b = fox_w_out.astype(BF16)
    cmp_w1_b, cmp_w2_b = nsa_cmp_w1.astype(BF16), nsa_cmp_w2.astype(BF16)
    cmp_w2t_b = t_last(nsa_cmp_w2).astype(BF16)
    cmp_pe = nsa_cmp_pe.reshape(nsa_cmp_pe.shape[0], 2, 1, CMP_BLOCK * HEAD_DIM)
    cmp_b1 = nsa_cmp_b1[:, :, None, :]
    fox_bf = fox_b_f[:, :, None]
    gains = norm_g[:, :, None, :]

    bias_c, bias_t = _bias_tables(rel_bias, s)

    h = x.reshape(n, d)
    for i in range(depth):
        g = gains[i]
        j = i // 2
        h = _ffn(h, g[0], g[1], wg, wu, wd, i, 0)
        h3 = h.reshape(b, s, d)
        if i % 2 == 0:
            q_t, kk, v_t, gates = _nsa_proj(h3, g[2], nsa_w_k, nsa_w_t, j)
            k_cmp, v_cmp_t = _compress(kk, cmp_pe, cmp_w1_b, cmp_b1, cmp_w2_b, cmp_w2t_b, j)
            o_cmp, chosen = _nsa_select(q_t, k_cmp, v_cmp_t, bias_c)
            o_t = _nsa_attn(q_t, kk, v_t, o_cmp, chosen, bias_t, gates)
            h3 = _outproj(o_t, nsa_w_out_b, h3, g[3], j)
        else:
            q_t, k, v_t = _fox_proj(h3, g[2], fox_w_t, fox_bf, j)
            o_t = _fox_attn(q_t, k, v_t)
            h3 = _outproj(o_t, fox_w_out_b, h3, g[3], j)
        h = _ffn(h3.reshape(n, d), g[4], g[5], wg, wu, wd, i, 1)
    return h.reshape(b, s, d)
```

```python
import math

import numpy as np
import jax
import jax.numpy as jnp
from jax import lax
from jax.experimental import pallas as pl
from jax.experimental.pallas import tpu as pltpu

N_HEADS = 16
HEAD_DIM = 64
KV_GROUPS = 4
Q_PER_GROUP = N_HEADS // KV_GROUPS
CMP_BLOCK = 32
CMP_STRIDE = 16
SEL_BLOCK = 64
SEL_SHIFT = 6
SEL_TOPK = 16
WINDOW = 512
NUM_BUCKETS = 32
MAX_DISTANCE = 128
RMS_EPS = 1e-6
NEG_INF = -1e30
FORCED_SCORE = 1e9
Q_DIM = N_HEADS * HEAD_DIM
KV_DIM = KV_GROUPS * HEAD_DIM
N_GATES = 3 * N_HEADS
LOG2E = math.log2(math.e)
Q_SCALE = HEAD_DIM ** -0.5 * LOG2E

SUBLANES = 8
ATT_TILE = 256
WIN_TILES = WINDOW // ATT_TILE
FAR_CHUNK = 512
GROUPS_PER_STEP = 2
FOX_TILE = 256
FOX_SLOTS = 2
AUG_DIM = HEAD_DIM + SUBLANES
FOX_QK_DIM = 2 * HEAD_DIM
VMEM_LIMIT = 56 * 1024 * 1024

BF16 = jnp.bfloat16
F32 = jnp.float32


def _cparams(*sem):
    return pltpu.CompilerParams(dimension_semantics=sem, vmem_limit_bytes=VMEM_LIMIT)


def _rms(x, g):
    return x * lax.rsqrt(jnp.mean(x * x, axis=-1, keepdims=True) + RMS_EPS) * g


def _dot(a, b):
    return jnp.dot(a, b, preferred_element_type=F32)


def _dot_nt(a, b):
    return lax.dot_general(a, b, (((1,), (1,)), ((), ())), preferred_element_type=F32)


def _dot_tn(a, b):
    return lax.dot_general(a, b, (((0,), (0,)), ((), ())), preferred_element_type=F32)


def _split3(x):
    hi = x.astype(BF16)
    r1 = x - hi.astype(F32)
    mid = r1.astype(BF16)
    lo = (r1 - mid.astype(F32)).astype(BF16)
    return hi, mid, lo


def _unit_rows(n_ones, width):
    row = lax.broadcasted_iota(jnp.int32, (SUBLANES, width), 0)
    return jnp.where(row < n_ones, 1.0, 0.0).astype(BF16)


def _fold(x, op):
    parts = x.reshape(x.shape[0] // SUBLANES, SUBLANES, x.shape[1])
    return op(parts, axis=0)


def _swiglu_halfstep(h, gpre_ref, gpost_ref, wg_ref, wu_ref, wd_ref):
    xn = _rms(h, gpre_ref[...]).astype(BF16)
    g = _dot(xn, wg_ref[...])
    u = _dot(xn, wu_ref[...])
    a = (g * jax.nn.sigmoid(g) * u).astype(BF16)
    return h + 0.5 * _rms(_dot(a, wd_ref[...]), gpost_ref[...])


def _ffn_body(h_ref, gpre_ref, gpost_ref, wg_ref, wu_ref, wd_ref, o_ref):
    o_ref[...] = _swiglu_halfstep(h_ref[...], gpre_ref, gpost_ref, wg_ref, wu_ref, wd_ref)


def _mix_ffn_body(h_ref, ot_ref, wo_ref, gmix_ref, gpre_ref, gpost_ref, wg_ref, wu_ref, wd_ref, o_ref):
    y = _dot_tn(ot_ref[0], wo_ref[...])
    h = h_ref[...] + _rms(y, gmix_ref[...])
    o_ref[...] = _swiglu_halfstep(h, gpre_ref, gpost_ref, wg_ref, wu_ref, wd_ref)


def _ffn(h, g_pre, g_post, wg, wu, wd, layer, half, mixer=None, tm=512):
    n, d = h.shape
    f = wg.shape[-1]
    once = dict(pipeline_mode=pl.Buffered(1))
    row = pl.BlockSpec((tm, d), lambda i: (i, 0))
    vec = pl.BlockSpec((1, d), lambda i: (0, 0))
    ffn_specs = [
        vec, vec,
        pl.BlockSpec((None, None, d, f), lambda i: (layer, half, 0, 0), **once),
        pl.BlockSpec((None, None, d, f), lambda i: (layer, half, 0, 0), **once),
        pl.BlockSpec((None, None, f, d), lambda i: (layer, half, 0, 0), **once),
    ]
    ffn_args = (g_pre, g_post, wg, wu, wd)
    if mixer is None:
        body, specs, args = _ffn_body, [row] + ffn_specs, (h,) + ffn_args
    else:
        o_t, w_out, mix_layer, g_mix = mixer
        kdim, s = o_t.shape[1:]
        per_seq = s // tm
        mix_specs = [
            pl.BlockSpec((1, kdim, tm), lambda i: (i // per_seq, 0, i % per_seq)),
            pl.BlockSpec((None, kdim, d), lambda i: (mix_layer, 0, 0), **once),
            vec,
        ]
        body, specs, args = _mix_ffn_body, [row] + mix_specs + ffn_specs, (h, o_t, w_out, g_mix) + ffn_args
    return pl.pallas_call(
        body,
        grid=(n // tm,),
        in_specs=specs,
        out_specs=row,
        out_shape=jax.ShapeDtypeStruct((n, d), F32),
        compiler_params=_cparams("parallel"),
        name="ffn" if mixer is None else "mix_ffn",
    )(*args)


def _nsa_proj_body(h_ref, g_ref, w_ref, wt_ref, qt_ref, k_ref, vt_ref, gate_ref):
    xn = _rms(h_ref[0], g_ref[...]).astype(BF16)
    res = _dot(xn, w_ref[...])
    for a in range(4 * KV_GROUPS):
        k_ref[0, a] = res[:, a * HEAD_DIM:(a + 1) * HEAD_DIM].astype(BF16)
    res_t = _dot_nt(wt_ref[...], xn)
    for hd in range(N_HEADS):
        qt_ref[0, hd] = (res_t[hd * HEAD_DIM:(hd + 1) * HEAD_DIM] * Q_SCALE).astype(BF16)
    extra = _unit_rows(1, res_t.shape[1])
    for a in range(2 * KV_GROUPS):
        lo = Q_DIM + a * HEAD_DIM
        vt_ref[0, a] = jnp.concatenate([res_t[lo:lo + HEAD_DIM].astype(BF16), extra], axis=0)
    gates = jax.nn.sigmoid(res_t[Q_DIM + 2 * KV_DIM:])
    width = 3 * Q_PER_GROUP
    for grp in range(KV_GROUPS):
        gate_ref[0, grp] = gates[grp * width:(grp + 1) * width]


def _nsa_proj(h3, g, w, w_t, layer, tm=512):
    b, s, d = h3.shape
    return pl.pallas_call(
        _nsa_proj_body,
        grid=(b, s // tm),
        in_specs=[
            pl.BlockSpec((1, tm, d), lambda i, j: (i, j, 0)),
            pl.BlockSpec((1, d), lambda i, j: (0, 0)),
            pl.BlockSpec((None, d, w.shape[-1]), lambda i, j: (layer, 0, 0)),
            pl.BlockSpec((None, w_t.shape[1], d), lambda i, j: (layer, 0, 0)),
        ],
        out_specs=[
            pl.BlockSpec((1, N_HEADS, HEAD_DIM, tm), lambda i, j: (i, 0, 0, j)),
            pl.BlockSpec((1, 4 * KV_GROUPS, tm, HEAD_DIM), lambda i, j: (i, 0, j, 0)),
            pl.BlockSpec((1, 2 * KV_GROUPS, AUG_DIM, tm), lambda i, j: (i, 0, 0, j)),
            pl.BlockSpec((1, KV_GROUPS, 3 * Q_PER_GROUP, tm), lambda i, j: (i, 0, 0, j)),
        ],
        out_shape=[
            jax.ShapeDtypeStruct((b, N_HEADS, HEAD_DIM, s), BF16),
            jax.ShapeDtypeStruct((b, 4 * KV_GROUPS, s, HEAD_DIM), BF16),
            jax.ShapeDtypeStruct((b, 2 * KV_GROUPS, AUG_DIM, s), BF16),
            jax.ShapeDtypeStruct((b, KV_GROUPS, 3 * Q_PER_GROUP, s), F32),
        ],
        compiler_params=_cparams("parallel", "parallel"),
        name="nsa_proj",
    )(h3, g, w, w_t)


def _compress_body(x_ref, pe_ref, w1_ref, b1_ref, w2_ref, w2t_ref, o_ref, ot_ref):
    n_chunk = x_ref.shape[3]
    half = CMP_STRIDE * HEAD_DIM
    x = x_ref[0, 0].reshape(KV_GROUPS * n_chunk, half)
    top = _dot(x, w1_ref[:half, :])
    bot = _dot(x, w1_ref[half:, :])
    bot_next = pltpu.roll(bot, KV_GROUPS * n_chunk - 1, 0)
    pe = jnp.broadcast_to(pe_ref[...].astype(BF16), (8, 2 * half))
    const = _dot(pe, w1_ref[...])[0:1] + b1_ref[...]
    hid = jax.nn.gelu(top + bot_next + const).astype(BF16)
    out = _dot(hid, w2_ref[...])
    row = lax.broadcasted_iota(jnp.int32, out.shape, 0) & (n_chunk - 1)
    o_ref[0, 0] = jnp.where(row < n_chunk - 1, out, 0.0).reshape(KV_GROUPS, n_chunk, HEAD_DIM).astype(BF16)
    out_t = _dot_nt(w2t_ref[...], hid)
    col = lax.broadcasted_iota(jnp.int32, out_t.shape, 1) & (n_chunk - 1)
    out_t = jnp.where(col < n_chunk - 1, out_t, 0.0).astype(BF16)
    for grp in range(KV_GROUPS):
        ot_ref[0, 0, grp] = out_t[:, grp * n_chunk:(grp + 1) * n_chunk]


def _compress(kk, pe, w1, b1, w2, w2t, layer):
    b, _, s, dh = kk.shape
    n_chunk = s // CMP_STRIDE
    x = kk[:, :2 * KV_GROUPS].reshape(b, 2, KV_GROUPS, n_chunk, CMP_STRIDE * dh)
    hidden = w1.shape[-1]
    return pl.pallas_call(
        _compress_body,
        grid=(b, 2),
        in_specs=[
            pl.BlockSpec((1, 1, KV_GROUPS, n_chunk, CMP_STRIDE * dh), lambda i, a: (i, a, 0, 0, 0)),
            pl.BlockSpec((None, None, 1, CMP_BLOCK * dh), lambda i, a: (layer, a, 0, 0)),
            pl.BlockSpec((None, None, CMP_BLOCK * dh, hidden), lambda i, a: (layer, a, 0, 0)),
            pl.BlockSpec((None, None, 1, hidden), lambda i, a: (layer, a, 0, 0)),
            pl.BlockSpec((None, None, hidden, dh), lambda i, a: (layer, a, 0, 0)),
            pl.BlockSpec((None, None, dh, hidden), lambda i, a: (layer, a, 0, 0)),
        ],
        out_specs=[
            pl.BlockSpec((1, 1, KV_GROUPS, n_chunk, dh), lambda i, a: (i, a, 0, 0, 0)),
            pl.BlockSpec((1, 1, KV_GROUPS, dh, n_chunk), lambda i, a: (i, a, 0, 0, 0)),
        ],
        out_shape=[
            jax.ShapeDtypeStruct((b, 2, KV_GROUPS, n_chunk, dh), BF16),
            jax.ShapeDtypeStruct((b, 2, KV_GROUPS, dh, n_chunk), BF16),
        ],
        compiler_params=_cparams("parallel", "parallel"),
        name="nsa_compress",
    )(x, pe, w1, b1, w2, w2t)


def _t5_bucket_np(rel):
    n = np.maximum(rel, 0)
    max_exact = NUM_BUCKETS // 2
    nf = np.maximum(n, 1).astype(np.float32)
    ratio = np.log(nf / np.float32(max_exact)) / np.float32(math.log(MAX_DISTANCE / max_exact))
    large = max_exact + (ratio * np.float32(NUM_BUCKETS - max_exact)).astype(np.int32)
    large = np.minimum(large, NUM_BUCKETS - 1)
    return np.where(n < max_exact, n, large).astype(np.int32)


def _bucket_maps(s):
    n_chunk = s // CMP_STRIDE
    t = np.arange(s)[None, :]
    blk_end = np.arange(n_chunk)[:, None] * CMP_STRIDE + CMP_BLOCK - 1
    rel_c = t - blk_end
    map_c = np.where(rel_c >= 0, _t5_bucket_np(rel_c), -1).astype(np.int32)
    j = np.arange(ATT_TILE)[:, None]
    i = np.arange(ATT_TILE)[None, :]
    diag = np.where(i - j >= 0, _t5_bucket_np(i - j), -1)
    sub = _t5_bucket_np(ATT_TILE + i - j)
    edge = np.where(j > i, _t5_bucket_np(WINDOW + i - j), -1)
    map_t = np.stack([diag, sub, edge]).astype(np.int32)
    assert _t5_bucket_np(np.arange(ATT_TILE + 1, s + WINDOW)).min() == _FAR_BUCKET
    return map_c, map_t


_FAR_BUCKET = NUM_BUCKETS - 1
TILE_DIAG, TILE_SUB, TILE_EDGE = 0, 1, 2


def _bias_body(rb_ref, mc_ref, mt_ref, bc_ref, bt_ref):
    hd = pl.program_id(0)

    def lookup(bucket, shift):
        acc = jnp.zeros(bucket.shape, F32)
        for bk in range(NUM_BUCKETS):
            acc = jnp.where(bucket == bk, (rb_ref[bk, hd] - shift) * LOG2E, acc)
        return jnp.where(bucket < 0, NEG_INF, acc)

    bc_ref[0] = lookup(mc_ref[...], 0.0)
    for d in range(mt_ref.shape[0]):
        bt_ref[d, 0] = lookup(mt_ref[d], rb_ref[_FAR_BUCKET, hd])


def _bias_tables(rel_bias, s):
    map_c, map_t = _bucket_maps(s)
    n_chunk = map_c.shape[0]
    n_tab = map_t.shape[0]
    T = ATT_TILE
    R = Q_PER_GROUP
    return pl.pallas_call(
        _bias_body,
        grid=(N_HEADS,),
        in_specs=[
            pl.BlockSpec(memory_space=pltpu.SMEM),
            pl.BlockSpec((n_chunk, s), lambda i: (0, 0)),
            pl.BlockSpec((n_tab, T, T), lambda i: (0, 0, 0)),
        ],
        out_specs=[
            pl.BlockSpec((1, n_chunk, s), lambda i: (i, 0, 0)),
            pl.BlockSpec((n_tab, 1, T, T), lambda i: (0, i // R, 0, i % R)),
        ],
        out_shape=[
            jax.ShapeDtypeStruct((N_HEADS, n_chunk, s), F32),
            jax.ShapeDtypeStruct((n_tab, KV_GROUPS, T, R * T), F32),
        ],
        compiler_params=_cparams("parallel"),
        name="t5_bias_tables",
    )(rel_bias, jnp.asarray(map_c), jnp.asarray(map_t))


def _nsa_select_body(q_ref, kc_ref, vct_ref, bc_ref, ovl_ref, oc_ref, ch_ref):
    T = ATT_TILE
    R = Q_PER_GROUP
    n_sel = ovl_ref.shape[0]
    ovl = ovl_ref[...]
    j_blk = lax.broadcasted_iota(jnp.int32, (n_sel, T), 0)
    j_slab = lax.broadcasted_iota(jnp.int32, (SUBLANES, T), 0)
    for u in range(q_ref.shape[3] // T):
        qi = pl.program_id(1) * (q_ref.shape[3] // T) + u
        cols = slice(u * T, (u + 1) * T)
        q_t = jnp.concatenate([q_ref[0, r, :, cols] for r in range(R)], axis=1)
        bias = jnp.concatenate([bc_ref[r, :, cols] for r in range(R)], axis=1)
        s_c = _dot(kc_ref[0, 0, 0], q_t) + bias
        m_c = jnp.max(s_c, axis=0, keepdims=True)
        p_c = jnp.exp2(s_c - m_c)
        p_c = p_c * (1.0 / jnp.sum(p_c, axis=0, keepdims=True))
        t_col = qi * T + (lax.broadcasted_iota(jnp.int32, (1, R * T), 1) & (T - 1))
        p_c = jnp.where(t_col >= CMP_BLOCK - 1, p_c, 0.0)
        o_c = _dot(vct_ref[0, 0, 0], p_c.astype(BF16))
        for r in range(R):
            oc_ref[0, r, :, cols] = o_c[:, r * T:(r + 1) * T]

        p_sum = p_c[:, :T]
        for r in range(1, R):
            p_sum = p_sum + p_c[:, r * T:(r + 1) * T]
        hi, mid, lo = _split3(p_sum)
        imp = _dot(ovl, hi) + _dot(ovl, mid) + _dot(ovl, lo)
        cur = (qi * T + lax.broadcasted_iota(jnp.int32, (n_sel, T), 1)) >> SEL_SHIFT
        forced = (j_blk == 0) | (j_blk == cur) | (j_blk == cur - 1)
        imp = jnp.where(forced, FORCED_SCORE, jnp.where(j_blk <= cur, imp, NEG_INF))
        slabs = [imp[lo:lo + SUBLANES] for lo in range(0, n_sel, SUBLANES)]
        ranks = [jnp.zeros((SUBLANES, T), F32) for _ in slabs]
        for i in range(n_sel):
            row = imp[i:i + 1, :]
            for k, slab in enumerate(slabs):
                lo = k * SUBLANES
                if lo > i:
                    hit = jnp.where(row >= slab, 1.0, 0.0)
                elif lo + SUBLANES - 1 <= i:
                    hit = jnp.where(row > slab, 1.0, 0.0)
                else:
                    hit = jnp.where(j_slab > i - lo, jnp.where(row >= slab, 1.0, 0.0),
                                    jnp.where(row > slab, 1.0, 0.0))
                ranks[k] = ranks[k] + hit
        rank = jnp.concatenate(ranks, axis=0)
        ch_ref[0, 0, :, cols] = jnp.where(rank < min(SEL_TOPK, n_sel), 1.0, 0.0)


def _nsa_select(q_t, k_cmp, v_cmp_t, bias_c, tiles_per_step=2):
    b, _, dh, s = q_t.shape
    R = Q_PER_GROUP
    n_chunk = s // CMP_STRIDE
    n_sel = s // SEL_BLOCK
    tq = tiles_per_step * ATT_TILE
    c_start = np.arange(n_chunk)[None, :] * CMP_STRIDE
    j = np.arange(n_sel)[:, None]
    overlap = (c_start < (j + 1) * SEL_BLOCK) & (c_start + CMP_BLOCK > j * SEL_BLOCK)
    overlap[:, n_chunk - 1] = False
    return pl.pallas_call(
        _nsa_select_body,
        grid=(KV_GROUPS, s // tq, b),
        in_specs=[
            pl.BlockSpec((1, R, dh, tq), lambda g, t, i: (i, g, 0, t)),
            pl.BlockSpec((1, 1, 1, n_chunk, dh), lambda g, t, i: (i, 0, g, 0, 0)),
            pl.BlockSpec((1, 1, 1, dh, n_chunk), lambda g, t, i: (i, 1, g, 0, 0)),
            pl.BlockSpec((R, n_chunk, tq), lambda g, t, i: (g, 0, t)),
            pl.BlockSpec((n_sel, n_chunk), lambda g, t, i: (0, 0)),
        ],
        out_specs=[
            pl.BlockSpec((1, R, dh, tq), lambda g, t, i: (i, g, 0, t)),
            pl.BlockSpec((1, 1, n_sel, tq), lambda g, t, i: (i, g, 0, t)),
        ],
        out_shape=[
            jax.ShapeDtypeStruct((b, N_HEADS, dh, s), F32),
            jax.ShapeDtypeStruct((b, KV_GROUPS, n_sel, s), F32),
        ],
        compiler_params=_cparams("parallel", "parallel", "parallel"),
        name="nsa_select",
    )(q_t, k_cmp, v_cmp_t, bias_c, jnp.asarray(overlap, BF16))


def _nsa_attn_body(q_ref, ks_ref, kw_ref, vst_ref, vwt_ref, bt_ref, gate_ref, oc_ref, ch_ref,
                   o_ref, rows_ref, s_ref, near_ref, win_ref, peak_ref, acc_ref):
    T = ATT_TILE
    R = Q_PER_GROUP
    C = FAR_CHUNK
    N = R * T
    groups = range(ks_ref.shape[1])
    qi = pl.program_id(2)
    n_far = jnp.maximum(qi - 1, 0)
    n_chunks = (n_far + C // T - 1) // (C // T)
    qs = [jnp.concatenate([q_ref[0, g * R + r] for r in range(R)], axis=1) for g in groups]

    def add_block_rows(g, which, s, off):
        first = off // SEL_BLOCK
        pieces = [s[b * SEL_BLOCK:(b + 1) * SEL_BLOCK] + rows_ref[g, which, pl.ds(first + b, 1), :]
                  for b in range(s.shape[0] // SEL_BLOCK)]
        return jnp.concatenate(pieces, axis=0)

    def near_logits(g, k_ref, d, table, masked):
        off = pl.multiple_of(jnp.maximum(qi - d, 0) * T, T)
        s = _dot(k_ref[0, g, pl.ds(off, T), :], qs[g])
        if table is not None:
            s = s + bt_ref[table, g]
        if masked:
            s = add_block_rows(g, 0, s, off)
        if d > 0:
            s = jnp.where(qi >= d, s, NEG_INF)
        return s, off

    def logits_pass(g, k_ref, tiles, masked, out_ref):
        offs = []
        peak = None
        for idx, (d, table) in enumerate(tiles):
            s, off = near_logits(g, k_ref, d, table, masked)
            out_ref[g, idx * T:(idx + 1) * T, :] = s
            part = _fold(s, jnp.max)
            peak = part if peak is None else jnp.maximum(peak, part)
            offs.append(off)
        return offs, peak

    def values_pass(g, logit_ref, vt_ref, offs, m):
        acc = None
        for idx, off in enumerate(offs):
            p = jnp.exp2((logit_ref[g, idx * T:(idx + 1) * T, :] - m).astype(BF16))
            pv = _dot(vt_ref[0, g, :, pl.ds(off, T)], p)
            acc = pv if acc is None else acc + pv
        return acc

    win_tiles = [(0, TILE_DIAG), (1, TILE_SUB)]
    win_tiles += [(d, None) for d in range(2, WIN_TILES)] + [(WIN_TILES, TILE_EDGE)]
    near_sel = [(1, TILE_SUB), (0, TILE_DIAG)]
    win_offs, m_win, sel_offs = [], [], []
    blk = lax.broadcasted_iota(jnp.int32, (ch_ref.shape[2], N), 0)
    for g in groups:
        add = (ch_ref[0, g] - 1.0) * -NEG_INF
        add = jnp.concatenate([add] * R, axis=1)
        rows_ref[g, 0] = add
        rows_ref[g, 1] = jnp.where(blk < n_far * (T // SEL_BLOCK), add, NEG_INF)
        offs, peak = logits_pass(g, kw_ref, win_tiles, False, win_ref)
        win_offs.append(offs)
        m_win.append(jnp.max(peak, axis=0, keepdims=True))
        offs, peak = logits_pass(g, ks_ref, near_sel, True, near_ref)
        sel_offs.append(offs)
        peak_ref[g] = peak

    def far_logits(c, carry):
        off = pl.multiple_of(c * C, C)
        for g in groups:
            s = _dot(ks_ref[0, g, pl.ds(off, C), :], qs[g])
            s = add_block_rows(g, 1, s, off)
            s_ref[g, pl.ds(off, C), :] = s
            peak_ref[g] = jnp.maximum(peak_ref[g], _fold(s, jnp.max))
        return carry

    lax.fori_loop(0, n_chunks, far_logits, 0)

    m_sel = []
    for g in groups:
        m_sel.append(jnp.max(peak_ref[g], axis=0, keepdims=True))
        acc_ref[g] = values_pass(g, near_ref, vst_ref, sel_offs[g], m_sel[g])

    def far_values(c, carry):
        off = pl.multiple_of(c * C, C)
        for g in groups:
            p = jnp.exp2((s_ref[g, pl.ds(off, C), :] - m_sel[g]).astype(BF16))
            acc_ref[g] += _dot(vst_ref[0, g, :, pl.ds(off, C)], p)
        return carry

    lax.fori_loop(0, n_chunks, far_values, 0)

    for g in groups:
        gate = gate_ref[0, g]
        gate_of = lambda br: jnp.concatenate(
            [gate[br * R + r:br * R + r + 1, :] for r in range(R)], axis=1)
        win = values_pass(g, win_ref, vwt_ref, win_offs[g], m_win[g])
        sel = acc_ref[g]
        w_win = gate_of(2) * (1.0 / win[HEAD_DIM:HEAD_DIM + 1])
        w_sel = gate_of(1) * (1.0 / sel[HEAD_DIM:HEAD_DIM + 1])
        o_c = jnp.concatenate([oc_ref[0, g * R + r] for r in range(R)], axis=1)
        o = gate_of(0) * o_c + w_sel * sel[:HEAD_DIM] + w_win * win[:HEAD_DIM]
        for r in range(R):
            hd = g * R + r
            o_ref[0, hd * HEAD_DIM:(hd + 1) * HEAD_DIM, :] = o[:, r * T:(r + 1) * T].astype(BF16)


def _nsa_attn(q_t, kk, v_t, o_cmp, chosen, bias_t, gates):
    b, _, dh, s = q_t.shape
    T = ATT_TILE
    R = Q_PER_GROUP
    N = R * T
    gp = GROUPS_PER_STEP
    per = KV_GROUPS // gp
    n_sel = s // SEL_BLOCK
    far_keys = max(s - 2 * T, FAR_CHUNK)
    once = dict(pipeline_mode=pl.Buffered(1))
    k_spec = lambda a: pl.BlockSpec((1, gp, s, dh), lambda g, i, t: (i, a * per + g, 0, 0))
    vt_spec = lambda a: pl.BlockSpec((1, gp, AUG_DIM, s), lambda g, i, t: (i, a * per + g, 0, 0))
    return pl.pallas_call(
        _nsa_attn_body,
        grid=(per, b, s // T),
        in_specs=[
            pl.BlockSpec((1, gp * R, dh, T), lambda g, i, t: (i, g, 0, t)),
            k_spec(2), k_spec(3), vt_spec(0), vt_spec(1),
            pl.BlockSpec((bias_t.shape[0], gp, T, N), lambda g, i, t: (0, g, 0, 0), **once),
            pl.BlockSpec((1, gp, 3 * R, T), lambda g, i, t: (i, g, 0, t)),
            pl.BlockSpec((1, gp * R, dh, T), lambda g, i, t: (i, g, 0, t)),
            pl.BlockSpec((1, gp, n_sel, T), lambda g, i, t: (i, g, 0, t)),
        ],
        out_specs=pl.BlockSpec((1, gp * R * dh, T), lambda g, i, t: (i, g, t)),
        out_shape=jax.ShapeDtypeStruct((b, Q_DIM, s), BF16),
        scratch_shapes=[
            pltpu.VMEM((gp, 2, n_sel, N), F32),
            pltpu.VMEM((gp, far_keys, N), F32),
            pltpu.VMEM((gp, 2 * T, N), F32),
            pltpu.VMEM((gp, (WIN_TILES + 1) * T, N), F32),
            pltpu.VMEM((gp, SUBLANES, N), F32),
            pltpu.VMEM((gp, AUG_DIM, N), F32),
        ],
        compiler_params=_cparams("parallel", "parallel", "arbitrary"),
        name="nsa_attn",
    )(q_t, kk, kk, v_t, v_t, bias_t, gates, o_cmp, chosen)


def _fox_proj_body(h_ref, g_ref, wt_ref, bf_ref, q_ref, k_ref, vt_ref, carry_ref):
    tm = h_ref.shape[1]

    @pl.when(pl.program_id(1) == 0)
    def _():
        carry_ref[...] = jnp.zeros_like(carry_ref)

    xn = _rms(h_ref[0], g_ref[...]).astype(BF16)
    res_t = _dot_nt(wt_ref[...], xn)
    log_f = jax.nn.log_sigmoid(res_t[3 * Q_DIM:] + bf_ref[...])
    upper = jnp.where(lax.broadcasted_iota(jnp.int32, (tm, tm), 0)
                      <= lax.broadcasted_iota(jnp.int32, (tm, tm), 1), 1.0, 0.0).astype(BF16)
    hi, mid, lo = _split3(log_f)
    cum = _dot(hi, upper) + _dot(mid, upper) + _dot(lo, upper) + carry_ref[...]
    carry_ref[...] = cum[:, tm - 1:tm]
    terms = [t.astype(F32) for t in _split3(-LOG2E * cum)]
    pad = FOX_QK_DIM - HEAD_DIM
    k_zeros = jnp.zeros((pad - len(terms), tm), F32)
    q_extra = jnp.concatenate([_unit_rows(len(terms), tm), jnp.zeros((pad - SUBLANES, tm), BF16)], axis=0)
    v_extra = _unit_rows(1, tm)
    for hd in range(N_HEADS):
        rows = slice(hd * HEAD_DIM, (hd + 1) * HEAD_DIM)
        q_h = (res_t[rows] * Q_SCALE).astype(BF16)
        k_h = res_t[Q_DIM + hd * HEAD_DIM:Q_DIM + (hd + 1) * HEAD_DIM]
        v_h = res_t[2 * Q_DIM + hd * HEAD_DIM:2 * Q_DIM + (hd + 1) * HEAD_DIM].astype(BF16)
        k_t = jnp.concatenate([k_h] + [t[hd:hd + 1] for t in terms] + [k_zeros], axis=0)
        q_ref[0, hd] = jnp.concatenate([q_h, q_extra], axis=0)
        k_ref[0, hd] = k_t.T.astype(BF16)
        vt_ref[0, hd] = jnp.concatenate([v_h, v_extra], axis=0)


def _fox_proj(h3, g, w_t, b_f, layer, tm=512):
    b, s, d = h3.shape
    t_spec = lambda rows: pl.BlockSpec((1, N_HEADS, rows, tm), lambda i, j: (i, 0, 0, j))
    t_shape = lambda rows: jax.ShapeDtypeStruct((b, N_HEADS, rows, s), BF16)
    return pl.pallas_call(
        _fox_proj_body,
        grid=(b, s // tm),
        in_specs=[
            pl.BlockSpec((1, tm, d), lambda i, j: (i, j, 0)),
            pl.BlockSpec((1, d), lambda i, j: (0, 0)),
            pl.BlockSpec((None, w_t.shape[1], d), lambda i, j: (layer, 0, 0)),
            pl.BlockSpec((None, N_HEADS, 1), lambda i, j: (layer, 0, 0)),
        ],
        out_specs=[t_spec(FOX_QK_DIM),
                   pl.BlockSpec((1, N_HEADS, tm, FOX_QK_DIM), lambda i, j: (i, 0, j, 0)),
                   t_spec(AUG_DIM)],
        out_shape=[t_shape(FOX_QK_DIM),
                   jax.ShapeDtypeStruct((b, N_HEADS, s, FOX_QK_DIM), BF16),
                   t_shape(AUG_DIM)],
        scratch_shapes=[pltpu.VMEM((N_HEADS, 1), F32)],
        compiler_params=_cparams("parallel", "arbitrary"),
        name="fox_proj",
    )(h3, g, w_t, b_f)


def _fox_attn_body(q_ref, k_ref, vt_ref, o_ref, *scratch):
    T = FOX_TILE
    n_q = q_ref.shape[3] // T
    n_heads = q_ref.shape[1]
    s_refs, p_refs = scratch[:len(scratch) // 2], scratch[len(scratch) // 2:]
    causal = (lax.broadcasted_iota(jnp.int32, (T, T), 0)
              <= lax.broadcasted_iota(jnp.int32, (T, T), 1))
    for qi in range(n_q):
        cols = slice(qi * T, (qi + 1) * T)
        for hh in range(n_heads):
            s_ref = s_refs[(qi % FOX_SLOTS) * n_heads + hh]
            p_ref = p_refs[(qi % FOX_SLOTS) * n_heads + hh]
            q_t = q_ref[0, hh, :, cols]
            peak = None
            for kt in range(qi + 1):
                keys = slice(kt * T, (kt + 1) * T)
                s = _dot(k_ref[0, hh, keys, :], q_t)
                if kt == qi:
                    s = jnp.where(causal, s, NEG_INF)
                s_ref[keys, :] = s
                part = _fold(s, jnp.max)
                peak = part if peak is None else jnp.maximum(peak, part)
            m = jnp.max(peak, axis=0, keepdims=True)
            for kt in range(qi + 1):
                keys = slice(kt * T, (kt + 1) * T)
                p_ref[keys, :] = jnp.exp2((s_ref[keys, :] - m).astype(BF16))
            extent = (qi + 1) * T
            acc = _dot(vt_ref[0, hh, :, :extent], p_ref[:extent, :])
            o = acc[:HEAD_DIM] * (1.0 / acc[HEAD_DIM:HEAD_DIM + 1])
            o_ref[0, hh * HEAD_DIM:(hh + 1) * HEAD_DIM, cols] = o.astype(BF16)


def _fox_attn(q_t, k, v_t, heads_per_step=2):
    b, nh, kdim, s = q_t.shape
    dh = HEAD_DIM
    T = FOX_TILE
    hp = heads_per_step
    t_spec = lambda rows: pl.BlockSpec((1, hp, rows, s), lambda i, h: (i, h, 0, 0))
    return pl.pallas_call(
        _fox_attn_body,
        grid=(b, nh // hp),
        in_specs=[t_spec(kdim), pl.BlockSpec((1, hp, s, kdim), lambda i, h: (i, h, 0, 0)), t_spec(AUG_DIM)],
        out_specs=pl.BlockSpec((1, hp * dh, s), lambda i, h: (i, h, 0)),
        out_shape=jax.ShapeDtypeStruct((b, nh * dh, s), BF16),
        scratch_shapes=([pltpu.VMEM((s, T), F32)] * (FOX_SLOTS * hp)
                        + [pltpu.VMEM((s, T), BF16)] * (FOX_SLOTS * hp)),
        compiler_params=_cparams("parallel", "parallel"),
        name="fox_attn",
    )(q_t, k, v_t)


def kernel(x, norm_g, ffn_w_gate, ffn_w_up, ffn_w_down, rel_bias, nsa_w_in, nsa_cmp_pe, nsa_cmp_w1,
           nsa_cmp_b1, nsa_cmp_w2, nsa_w_out, fox_w_in, fox_b_f, fox_w_out):
    b, s, d = x.shape
    depth = norm_g.shape[0]
    n = b * s
    t_last = lambda w: jnp.swapaxes(w, -1, -2)
    wg, wu, wd = ffn_w_gate.astype(BF16), ffn_w_up.astype(BF16), ffn_w_down.astype(BF16)
    c0 = Q_DIM
    col = lambda a: nsa_w_in[:, :, c0 + a * KV_DIM:c0 + (a + 1) * KV_DIM]
    gate_cols = np.arange(N_GATES).reshape(3, KV_GROUPS, Q_PER_GROUP).transpose(1, 0, 2).reshape(-1)
    w_gates = nsa_w_in[:, :, c0 + 6 * KV_DIM:][:, :, gate_cols]
    nsa_w_k = jnp.concatenate([col(0), col(1), col(2), col(4)], axis=-1).astype(BF16)
    nsa_w_t = t_last(jnp.concatenate([nsa_w_in[:, :, :c0], col(3), col(5), w_gates], axis=-1)).astype(BF16)
    nsa_w_out_b = nsa_w_out.astype(BF16)
    fox_w_t = t_last(fox_w_in).astype(BF16)
    fox_w_out_b = fox_w_out.astype(BF16)
    cmp_w1_b, cmp_w2_b = nsa_cmp_w1.astype(BF16), nsa_cmp_w2.astype(BF16)
    cmp_w2t_b = t_last(nsa_cmp_w2).astype(BF16)
    cmp_pe = nsa_cmp_pe.reshape(nsa_cmp_pe.shape[0], 2, 1, CMP_BLOCK * HEAD_DIM)
    cmp_b1 = nsa_cmp_b1[:, :, None, :]
    fox_bf = fox_b_f[:, :, None]
    gains = norm_g[:, :, None, :]

    bias_c, bias_t = _bias_tables(rel_bias, s)

    h = x.reshape(n, d)
    for i in range(depth):
        g = gains[i]
        j = i // 2
        h = _ffn(h, g[0], g[1], wg, wu, wd, i, 0)
        h3 = h.reshape(b, s, d)
        if i % 2 == 0:
            q_t, kk, v_t, gates = _nsa_proj(h3, g[2], nsa_w_k, nsa_w_t, j)
            k_cmp, v_cmp_t = _compress(kk, cmp_pe, cmp_w1_b, cmp_b1, cmp_w2_b, cmp_w2t_b, j)
            o_cmp, chosen = _nsa_select(q_t, k_cmp, v_cmp_t, bias_c)
            o_t = _nsa_attn(q_t, kk, v_t, o_cmp, chosen, bias_t, gates)
            w_out = nsa_w_out_b
        else:
            q_t, k, v_t = _fox_proj(h3, g[2], fox_w_t, fox_bf, j)
            o_t = _fox_attn(q_t, k, v_t)
            w_out = fox_w_out_b
        h = _ffn(h, g[4], g[5], wg, wu, wd, i, 1, mixer=(o_t, w_out, j, g[3]))
    return h.reshape(b, s, d)
```

```python
import math

import numpy as np
import jax
import jax.numpy as jnp
from jax import lax
from jax.experimental import pallas as pl
from jax.experimental.pallas import tpu as pltpu

N_HEADS = 16
HEAD_DIM = 64
KV_GROUPS = 4
Q_PER_GROUP = N_HEADS // KV_GROUPS
CMP_BLOCK = 32
CMP_STRIDE = 16
SEL_BLOCK = 64
SEL_SHIFT = 6
SEL_TOPK = 16
WINDOW = 512
NUM_BUCKETS = 32
MAX_DISTANCE = 128
RMS_EPS = 1e-6
NEG_INF = -1e30
FORCED_SCORE = 1e9
Q_DIM = N_HEADS * HEAD_DIM
KV_DIM = KV_GROUPS * HEAD_DIM
N_GATES = 3 * N_HEADS
LOG2E = math.log2(math.e)
Q_SCALE = HEAD_DIM ** -0.5 * LOG2E

SUBLANES = 8
ATT_TILE = 256
WIN_TILES = WINDOW // ATT_TILE
FAR_CHUNK = 512
GROUPS_PER_STEP = 2
FOX_TILE = 256
FOX_SLOTS = 2
AUG_DIM = HEAD_DIM + SUBLANES
FOX_QK_DIM = 2 * HEAD_DIM
VMEM_LIMIT = 56 * 1024 * 1024

BF16 = jnp.bfloat16
F32 = jnp.float32


def _cparams(*sem):
    return pltpu.CompilerParams(dimension_semantics=sem, vmem_limit_bytes=VMEM_LIMIT)


def _rms(x, g):
    return x * lax.rsqrt(jnp.mean(x * x, axis=-1, keepdims=True) + RMS_EPS) * g


def _dot(a, b):
    return jnp.dot(a, b, preferred_element_type=F32)


def _dot_nt(a, b):
    return lax.dot_general(a, b, (((1,), (1,)), ((), ())), preferred_element_type=F32)


def _dot_tn(a, b):
    return lax.dot_general(a, b, (((0,), (0,)), ((), ())), preferred_element_type=F32)


def _split3(x):
    hi = x.astype(BF16)
    r1 = x - hi.astype(F32)
    mid = r1.astype(BF16)
    lo = (r1 - mid.astype(F32)).astype(BF16)
    return hi, mid, lo


def _unit_rows(n_ones, width):
    row = lax.broadcasted_iota(jnp.int32, (SUBLANES, width), 0)
    return jnp.where(row < n_ones, 1.0, 0.0).astype(BF16)


def _fold(x, op):
    parts = x.reshape(x.shape[0] // SUBLANES, SUBLANES, x.shape[1])
    return op(parts, axis=0)


def _swiglu_halfstep(h, gpre_ref, gpost_ref, wg_ref, wu_ref, wd_ref):
    xn = _rms(h, gpre_ref[...]).astype(BF16)
    g = _dot(xn, wg_ref[...])
    u = _dot(xn, wu_ref[...])
    a = (g * jax.nn.sigmoid(g) * u).astype(BF16)
    return h + 0.5 * _rms(_dot(a, wd_ref[...]), gpost_ref[...])


def _ffn_body(h_ref, gpre_ref, gpost_ref, wg_ref, wu_ref, wd_ref, o_ref):
    o_ref[...] = _swiglu_halfstep(h_ref[...], gpre_ref, gpost_ref, wg_ref, wu_ref, wd_ref)


def _mix_ffn_body(h_ref, ot_ref, wo_ref, gmix_ref, gpre_ref, gpost_ref, wg_ref, wu_ref, wd_ref, o_ref):
    y = _dot_tn(ot_ref[0], wo_ref[...])
    h = h_ref[...] + _rms(y, gmix_ref[...])
    o_ref[...] = _swiglu_halfstep(h, gpre_ref, gpost_ref, wg_ref, wu_ref, wd_ref)


def _ffn(h, g_pre, g_post, wg, wu, wd, layer, half, mixer=None, tm=512):
    n, d = h.shape
    f = wg.shape[-1]
    once = dict(pipeline_mode=pl.Buffered(1))
    row = pl.BlockSpec((tm, d), lambda i: (i, 0))
    vec = pl.BlockSpec((1, d), lambda i: (0, 0))
    ffn_specs = [
        vec, vec,
        pl.BlockSpec((None, None, d, f), lambda i: (layer, half, 0, 0), **once),
        pl.BlockSpec((None, None, d, f), lambda i: (layer, half, 0, 0), **once),
        pl.BlockSpec((None, None, f, d), lambda i: (layer, half, 0, 0), **once),
    ]
    ffn_args = (g_pre, g_post, wg, wu, wd)
    if mixer is None:
        body, specs, args = _ffn_body, [row] + ffn_specs, (h,) + ffn_args
    else:
        o_t, w_out, mix_layer, g_mix = mixer
        kdim, s = o_t.shape[1:]
        per_seq = s // tm
        mix_specs = [
            pl.BlockSpec((1, kdim, tm), lambda i: (i // per_seq, 0, i % per_seq)),
            pl.BlockSpec((None, kdim, d), lambda i: (mix_layer, 0, 0), **once),
            vec,
        ]
        body, specs, args = _mix_ffn_body, [row] + mix_specs + ffn_specs, (h, o_t, w_out, g_mix) + ffn_args
    return pl.pallas_call(
        body,
        grid=(n // tm,),
        in_specs=specs,
        out_specs=row,
        out_shape=jax.ShapeDtypeStruct((n, d), F32),
        compiler_params=_cparams("parallel"),
        name="ffn" if mixer is None else "mix_ffn",
    )(*args)


def _nsa_proj_body(h_ref, g_ref, w_ref, wt_ref, qt_ref, k_ref, vt_ref, gate_ref):
    xn = _rms(h_ref[0], g_ref[...]).astype(BF16)
    res = _dot(xn, w_ref[...])
    for a in range(4 * KV_GROUPS):
        k_ref[0, a] = res[:, a * HEAD_DIM:(a + 1) * HEAD_DIM].astype(BF16)
    res_t = _dot_nt(wt_ref[...], xn)
    for hd in range(N_HEADS):
        qt_ref[0, hd] = (res_t[hd * HEAD_DIM:(hd + 1) * HEAD_DIM] * Q_SCALE).astype(BF16)
    extra = _unit_rows(1, res_t.shape[1])
    for a in range(2 * KV_GROUPS):
        lo = Q_DIM + a * HEAD_DIM
        vt_ref[0, a] = jnp.concatenate([res_t[lo:lo + HEAD_DIM].astype(BF16), extra], axis=0)
    gates = jax.nn.sigmoid(res_t[Q_DIM + 2 * KV_DIM:])
    width = 3 * Q_PER_GROUP
    for grp in range(KV_GROUPS):
        gate_ref[0, grp] = gates[grp * width:(grp + 1) * width]


def _nsa_proj(h3, g, w, w_t, layer, tm=1024):
    b, s, d = h3.shape
    return pl.pallas_call(
        _nsa_proj_body,
        grid=(b, s // tm),
        in_specs=[
            pl.BlockSpec((1, tm, d), lambda i, j: (i, j, 0)),
            pl.BlockSpec((1, d), lambda i, j: (0, 0)),
            pl.BlockSpec((None, d, w.shape[-1]), lambda i, j: (layer, 0, 0)),
            pl.BlockSpec((None, w_t.shape[1], d), lambda i, j: (layer, 0, 0)),
        ],
        out_specs=[
            pl.BlockSpec((1, N_HEADS, HEAD_DIM, tm), lambda i, j: (i, 0, 0, j)),
            pl.BlockSpec((1, 4 * KV_GROUPS, tm, HEAD_DIM), lambda i, j: (i, 0, j, 0)),
            pl.BlockSpec((1, 2 * KV_GROUPS, AUG_DIM, tm), lambda i, j: (i, 0, 0, j)),
            pl.BlockSpec((1, KV_GROUPS, 3 * Q_PER_GROUP, tm), lambda i, j: (i, 0, 0, j)),
        ],
        out_shape=[
            jax.ShapeDtypeStruct((b, N_HEADS, HEAD_DIM, s), BF16),
            jax.ShapeDtypeStruct((b, 4 * KV_GROUPS, s, HEAD_DIM), BF16),
            jax.ShapeDtypeStruct((b, 2 * KV_GROUPS, AUG_DIM, s), BF16),
            jax.ShapeDtypeStruct((b, KV_GROUPS, 3 * Q_PER_GROUP, s), F32),
        ],
        compiler_params=_cparams("parallel", "parallel"),
        name="nsa_proj",
    )(h3, g, w, w_t)


def _compress_body(x_ref, pe_ref, w1_ref, b1_ref, w2_ref, w2t_ref, o_ref, ot_ref):
    n_chunk = x_ref.shape[3]
    half = CMP_STRIDE * HEAD_DIM
    x = x_ref[0, 0].reshape(KV_GROUPS * n_chunk, half)
    top = _dot(x, w1_ref[:half, :])
    bot = _dot(x, w1_ref[half:, :])
    bot_next = pltpu.roll(bot, KV_GROUPS * n_chunk - 1, 0)
    pe = jnp.broadcast_to(pe_ref[...].astype(BF16), (8, 2 * half))
    const = _dot(pe, w1_ref[...])[0:1] + b1_ref[...]
    hid = jax.nn.gelu(top + bot_next + const).astype(BF16)
    out = _dot(hid, w2_ref[...])
    row = lax.broadcasted_iota(jnp.int32, out.shape, 0) & (n_chunk - 1)
    o_ref[0, 0] = jnp.where(row < n_chunk - 1, out, 0.0).reshape(KV_GROUPS, n_chunk, HEAD_DIM).astype(BF16)
    out_t = _dot_nt(w2t_ref[...], hid)
    col = lax.broadcasted_iota(jnp.int32, out_t.shape, 1) & (n_chunk - 1)
    out_t = jnp.where(col < n_chunk - 1, out_t, 0.0).astype(BF16)
    for grp in range(KV_GROUPS):
        ot_ref[0, 0, grp] = out_t[:, grp * n_chunk:(grp + 1) * n_chunk]


def _compress(kk, pe, w1, b1, w2, w2t, layer):
    b, _, s, dh = kk.shape
    n_chunk = s // CMP_STRIDE
    x = kk[:, :2 * KV_GROUPS].reshape(b, 2, KV_GROUPS, n_chunk, CMP_STRIDE * dh)
    hidden = w1.shape[-1]
    return pl.pallas_call(
        _compress_body,
        grid=(b, 2),
        in_specs=[
            pl.BlockSpec((1, 1, KV_GROUPS, n_chunk, CMP_STRIDE * dh), lambda i, a: (i, a, 0, 0, 0)),
            pl.BlockSpec((None, None, 1, CMP_BLOCK * dh), lambda i, a: (layer, a, 0, 0)),
            pl.BlockSpec((None, None, CMP_BLOCK * dh, hidden), lambda i, a: (layer, a, 0, 0)),
            pl.BlockSpec((None, None, 1, hidden), lambda i, a: (layer, a, 0, 0)),
            pl.BlockSpec((None, None, hidden, dh), lambda i, a: (layer, a, 0, 0)),
            pl.BlockSpec((None, None, dh, hidden), lambda i, a: (layer, a, 0, 0)),
        ],
        out_specs=[
            pl.BlockSpec((1, 1, KV_GROUPS, n_chunk, dh), lambda i, a: (i, a, 0, 0, 0)),
            pl.BlockSpec((1, 1, KV_GROUPS, dh, n_chunk), lambda i, a: (i, a, 0, 0, 0)),
        ],
        out_shape=[
            jax.ShapeDtypeStruct((b, 2, KV_GROUPS, n_chunk, dh), BF16),
            jax.ShapeDtypeStruct((b, 2, KV_GROUPS, dh, n_chunk), BF16),
        ],
        compiler_params=_cparams("parallel", "parallel"),
        name="nsa_compress",
    )(x, pe, w1, b1, w2, w2t)


def _t5_bucket_np(rel):
    n = np.maximum(rel, 0)
    max_exact = NUM_BUCKETS // 2
    nf = np.maximum(n, 1).astype(np.float32)
    ratio = np.log(nf / np.float32(max_exact)) / np.float32(math.log(MAX_DISTANCE / max_exact))
    large = max_exact + (ratio * np.float32(NUM_BUCKETS - max_exact)).astype(np.int32)
    large = np.minimum(large, NUM_BUCKETS - 1)
    return np.where(n < max_exact, n, large).astype(np.int32)


def _bucket_maps(s):
    n_chunk = s // CMP_STRIDE
    t = np.arange(s)[None, :]
    blk_end = np.arange(n_chunk)[:, None] * CMP_STRIDE + CMP_BLOCK - 1
    rel_c = t - blk_end
    map_c = np.where(rel_c >= 0, _t5_bucket_np(rel_c), -1).astype(np.int32)
    j = np.arange(ATT_TILE)[:, None]
    i = np.arange(ATT_TILE)[None, :]
    diag = np.where(i - j >= 0, _t5_bucket_np(i - j), -1)
    sub = _t5_bucket_np(ATT_TILE + i - j)
    edge = np.where(j > i, _t5_bucket_np(WINDOW + i - j), -1)
    map_t = np.stack([diag, sub, edge]).astype(np.int32)
    assert _t5_bucket_np(np.arange(ATT_TILE + 1, s + WINDOW)).min() == _FAR_BUCKET
    return map_c, map_t


_FAR_BUCKET = NUM_BUCKETS - 1
TILE_DIAG, TILE_SUB, TILE_EDGE = 0, 1, 2


def _bias_body(rb_ref, mc_ref, mt_ref, bc_ref, bt_ref):
    hd = pl.program_id(0)

    def lookup(bucket, shift):
        acc = jnp.zeros(bucket.shape, F32)
        for bk in range(NUM_BUCKETS):
            acc = jnp.where(bucket == bk, (rb_ref[bk, hd] - shift) * LOG2E, acc)
        return jnp.where(bucket < 0, NEG_INF, acc)

    bc_ref[0] = lookup(mc_ref[...], 0.0)
    for d in range(mt_ref.shape[0]):
        bt_ref[d, 0] = lookup(mt_ref[d], rb_ref[_FAR_BUCKET, hd])


def _bias_tables(rel_bias, s):
    map_c, map_t = _bucket_maps(s)
    n_chunk = map_c.shape[0]
    n_tab = map_t.shape[0]
    T = ATT_TILE
    R = Q_PER_GROUP
    return pl.pallas_call(
        _bias_body,
        grid=(N_HEADS,),
        in_specs=[
            pl.BlockSpec(memory_space=pltpu.SMEM),
            pl.BlockSpec((n_chunk, s), lambda i: (0, 0)),
            pl.BlockSpec((n_tab, T, T), lambda i: (0, 0, 0)),
        ],
        out_specs=[
            pl.BlockSpec((1, n_chunk, s), lambda i: (i, 0, 0)),
            pl.BlockSpec((n_tab, 1, T, T), lambda i: (0, i // R, 0, i % R)),
        ],
        out_shape=[
            jax.ShapeDtypeStruct((N_HEADS, n_chunk, s), F32),
            jax.ShapeDtypeStruct((n_tab, KV_GROUPS, T, R * T), F32),
        ],
        compiler_params=_cparams("parallel"),
        name="t5_bias_tables",
    )(rel_bias, jnp.asarray(map_c), jnp.asarray(map_t))


def _nsa_select_body(q_ref, kc_ref, vct_ref, bc_ref, ovl_ref, oc_ref, ch_ref):
    T = ATT_TILE
    R = Q_PER_GROUP
    n_sel = ovl_ref.shape[0]
    ovl = ovl_ref[...]
    j_blk = lax.broadcasted_iota(jnp.int32, (n_sel, T), 0)
    j_slab = lax.broadcasted_iota(jnp.int32, (SUBLANES, T), 0)
    for u in range(q_ref.shape[3] // T):
        qi = pl.program_id(1) * (q_ref.shape[3] // T) + u
        cols = slice(u * T, (u + 1) * T)
        q_t = jnp.concatenate([q_ref[0, r, :, cols] for r in range(R)], axis=1)
        bias = jnp.concatenate([bc_ref[r, :, cols] for r in range(R)], axis=1)
        s_c = _dot(kc_ref[0, 0, 0], q_t) + bias
        m_c = jnp.max(s_c, axis=0, keepdims=True)
        p_c = jnp.exp2(s_c - m_c)
        p_c = p_c * (1.0 / jnp.sum(p_c, axis=0, keepdims=True))
        t_col = qi * T + (lax.broadcasted_iota(jnp.int32, (1, R * T), 1) & (T - 1))
        p_c = jnp.where(t_col >= CMP_BLOCK - 1, p_c, 0.0)
        o_c = _dot(vct_ref[0, 0, 0], p_c.astype(BF16))
        for r in range(R):
            oc_ref[0, r, :, cols] = o_c[:, r * T:(r + 1) * T]

        p_sum = p_c[:, :T]
        for r in range(1, R):
            p_sum = p_sum + p_c[:, r * T:(r + 1) * T]
        hi, mid, lo = _split3(p_sum)
        imp = _dot(ovl, hi) + _dot(ovl, mid) + _dot(ovl, lo)
        cur = (qi * T + lax.broadcasted_iota(jnp.int32, (n_sel, T), 1)) >> SEL_SHIFT
        forced = (j_blk == 0) | (j_blk == cur) | (j_blk == cur - 1)
        imp = jnp.where(forced, FORCED_SCORE, jnp.where(j_blk <= cur, imp, NEG_INF))
        slabs = [imp[lo:lo + SUBLANES] for lo in range(0, n_sel, SUBLANES)]
        ranks = [jnp.zeros((SUBLANES, T), F32) for _ in slabs]
        for i in range(n_sel):
            row = imp[i:i + 1, :]
            for k, slab in enumerate(slabs):
                lo = k * SUBLANES
                if lo > i:
                    hit = jnp.where(row >= slab, 1.0, 0.0)
                elif lo + SUBLANES - 1 <= i:
                    hit = jnp.where(row > slab, 1.0, 0.0)
                else:
                    hit = jnp.where(j_slab > i - lo, jnp.where(row >= slab, 1.0, 0.0),
                                    jnp.where(row > slab, 1.0, 0.0))
                ranks[k] = ranks[k] + hit
        rank = jnp.concatenate(ranks, axis=0)
        ch_ref[0, 0, :, cols] = jnp.where(rank < min(SEL_TOPK, n_sel), 1.0, 0.0)


def _nsa_select(q_t, k_cmp, v_cmp_t, bias_c, tiles_per_step=4):
    b, _, dh, s = q_t.shape
    R = Q_PER_GROUP
    n_chunk = s // CMP_STRIDE
    n_sel = s // SEL_BLOCK
    tq = tiles_per_step * ATT_TILE
    c_start = np.arange(n_chunk)[None, :] * CMP_STRIDE
    j = np.arange(n_sel)[:, None]
    overlap = (c_start < (j + 1) * SEL_BLOCK) & (c_start + CMP_BLOCK > j * SEL_BLOCK)
    overlap[:, n_chunk - 1] = False
    return pl.pallas_call(
        _nsa_select_body,
        grid=(KV_GROUPS, s // tq, b),
        in_specs=[
            pl.BlockSpec((1, R, dh, tq), lambda g, t, i: (i, g, 0, t)),
            pl.BlockSpec((1, 1, 1, n_chunk, dh), lambda g, t, i: (i, 0, g, 0, 0)),
            pl.BlockSpec((1, 1, 1, dh, n_chunk), lambda g, t, i: (i, 1, g, 0, 0)),
            pl.BlockSpec((R, n_chunk, tq), lambda g, t, i: (g, 0, t)),
            pl.BlockSpec((n_sel, n_chunk), lambda g, t, i: (0, 0)),
        ],
        out_specs=[
            pl.BlockSpec((1, R, dh, tq), lambda g, t, i: (i, g, 0, t)),
            pl.BlockSpec((1, 1, n_sel, tq), lambda g, t, i: (i, g, 0, t)),
        ],
        out_shape=[
            jax.ShapeDtypeStruct((b, N_HEADS, dh, s), F32),
            jax.ShapeDtypeStruct((b, KV_GROUPS, n_sel, s), F32),
        ],
        compiler_params=_cparams("parallel", "parallel", "parallel"),
        name="nsa_select",
    )(q_t, k_cmp, v_cmp_t, bias_c, jnp.asarray(overlap, BF16))


def _nsa_attn_body(q_ref, ks_ref, kw_ref, vst_ref, vwt_ref, bt_ref, gate_ref, oc_ref, ch_ref,
                   o_ref, rows_ref, s_ref, near_ref, win_ref, peak_ref, acc_ref):
    T = ATT_TILE
    R = Q_PER_GROUP
    C = FAR_CHUNK
    N = R * T
    groups = range(ks_ref.shape[1])
    qi = pl.program_id(2)
    n_far = jnp.maximum(qi - 1, 0)
    n_chunks = (n_far + C // T - 1) // (C // T)
    qs = [jnp.concatenate([q_ref[0, g * R + r] for r in range(R)], axis=1) for g in groups]

    def add_block_rows(g, which, s, off):
        first = off // SEL_BLOCK
        pieces = [s[b * SEL_BLOCK:(b + 1) * SEL_BLOCK] + rows_ref[g, which, pl.ds(first + b, 1), :]
                  for b in range(s.shape[0] // SEL_BLOCK)]
        return jnp.concatenate(pieces, axis=0)

    def near_logits(g, k_ref, d, table, masked):
        off = pl.multiple_of(jnp.maximum(qi - d, 0) * T, T)
        s = _dot(k_ref[0, g, pl.ds(off, T), :], qs[g])
        if table is not None:
            s = s + bt_ref[table, g]
        if masked:
            s = add_block_rows(g, 0, s, off)
        if d > 0:
            s = jnp.where(qi >= d, s, NEG_INF)
        return s, off

    def logits_pass(g, k_ref, tiles, masked, out_ref):
        offs = []
        peak = None
        for idx, (d, table) in enumerate(tiles):
            s, off = near_logits(g, k_ref, d, table, masked)
            out_ref[g, idx * T:(idx + 1) * T, :] = s
            part = _fold(s, jnp.max)
            peak = part if peak is None else jnp.maximum(peak, part)
            offs.append(off)
        return offs, peak

    def values_pass(g, logit_ref, vt_ref, offs, m):
        acc = None
        for idx, off in enumerate(offs):
            p = jnp.exp2((logit_ref[g, idx * T:(idx + 1) * T, :] - m).astype(BF16))
            pv = _dot(vt_ref[0, g, :, pl.ds(off, T)], p)
            acc = pv if acc is None else acc + pv
        return acc

    win_tiles = [(0, TILE_DIAG), (1, TILE_SUB)]
    win_tiles += [(d, None) for d in range(2, WIN_TILES)] + [(WIN_TILES, TILE_EDGE)]
    near_sel = [(1, TILE_SUB), (0, TILE_DIAG)]
    win_offs, m_win, sel_offs = [], [], []
    blk = lax.broadcasted_iota(jnp.int32, (ch_ref.shape[2], N), 0)
    for g in groups:
        add = (ch_ref[0, g] - 1.0) * -NEG_INF
        add = jnp.concatenate([add] * R, axis=1)
        rows_ref[g, 0] = add
        rows_ref[g, 1] = jnp.where(blk < n_far * (T // SEL_BLOCK), add, NEG_INF)
        offs, peak = logits_pass(g, kw_ref, win_tiles, False, win_ref)
        win_offs.append(offs)
        m_win.append(jnp.max(peak, axis=0, keepdims=True))
        offs, peak = logits_pass(g, ks_ref, near_sel, True, near_ref)
        sel_offs.append(offs)
        peak_ref[g] = peak

    def far_logits(c, carry):
        off = pl.multiple_of(c * C, C)
        for g in groups:
            s = _dot(ks_ref[0, g, pl.ds(off, C), :], qs[g])
            s = add_block_rows(g, 1, s, off)
            s_ref[g, pl.ds(off, C), :] = s
            peak_ref[g] = jnp.maximum(peak_ref[g], _fold(s, jnp.max))
        return carry

    lax.fori_loop(0, n_chunks, far_logits, 0)

    m_sel = []
    for g in groups:
        m_sel.append(jnp.max(peak_ref[g], axis=0, keepdims=True))
        acc_ref[g] = values_pass(g, near_ref, vst_ref, sel_offs[g], m_sel[g])

    def far_values(c, carry):
        off = pl.multiple_of(c * C, C)
        for g in groups:
            p = jnp.exp2((s_ref[g, pl.ds(off, C), :] - m_sel[g]).astype(BF16))
            acc_ref[g] += _dot(vst_ref[0, g, :, pl.ds(off, C)], p)
        return carry

    lax.fori_loop(0, n_chunks, far_values, 0)

    for g in groups:
        gate = gate_ref[0, g]
        gate_of = lambda br: jnp.concatenate(
            [gate[br * R + r:br * R + r + 1, :] for r in range(R)], axis=1)
        win = values_pass(g, win_ref, vwt_ref, win_offs[g], m_win[g])
        sel = acc_ref[g]
        w_win = gate_of(2) * (1.0 / win[HEAD_DIM:HEAD_DIM + 1])
        w_sel = gate_of(1) * (1.0 / sel[HEAD_DIM:HEAD_DIM + 1])
        o_c = jnp.concatenate([oc_ref[0, g * R + r] for r in range(R)], axis=1)
        o = gate_of(0) * o_c + w_sel * sel[:HEAD_DIM] + w_win * win[:HEAD_DIM]
        for r in range(R):
            hd = g * R + r
            o_ref[0, hd * HEAD_DIM:(hd + 1) * HEAD_DIM, :] = o[:, r * T:(r + 1) * T].astype(BF16)


def _nsa_attn(q_t, kk, v_t, o_cmp, chosen, bias_t, gates):
    b, _, dh, s = q_t.shape
    T = ATT_TILE
    R = Q_PER_GROUP
    N = R * T
    gp = GROUPS_PER_STEP
    per = KV_GROUPS // gp
    n_sel = s // SEL_BLOCK
    far_keys = max(s - 2 * T, FAR_CHUNK)
    once = dict(pipeline_mode=pl.Buffered(1))
    k_spec = lambda a: pl.BlockSpec((1, gp, s, dh), lambda g, i, t: (i, a * per + g, 0, 0))
    vt_spec = lambda a: pl.BlockSpec((1, gp, AUG_DIM, s), lambda g, i, t: (i, a * per + g, 0, 0))
    return pl.pallas_call(
        _nsa_attn_body,
        grid=(per, b, s // T),
        in_specs=[
            pl.BlockSpec((1, gp * R, dh, T), lambda g, i, t: (i, g, 0, t)),
            k_spec(2), k_spec(3), vt_spec(0), vt_spec(1),
            pl.BlockSpec((bias_t.shape[0], gp, T, N), lambda g, i, t: (0, g, 0, 0), **once),
            pl.BlockSpec((1, gp, 3 * R, T), lambda g, i, t: (i, g, 0, t)),
            pl.BlockSpec((1, gp * R, dh, T), lambda g, i, t: (i, g, 0, t)),
            pl.BlockSpec((1, gp, n_sel, T), lambda g, i, t: (i, g, 0, t)),
        ],
        out_specs=pl.BlockSpec((1, gp * R * dh, T), lambda g, i, t: (i, g, t)),
        out_shape=jax.ShapeDtypeStruct((b, Q_DIM, s), BF16),
        scratch_shapes=[
            pltpu.VMEM((gp, 2, n_sel, N), F32),
            pltpu.VMEM((gp, far_keys, N), F32),
            pltpu.VMEM((gp, 2 * T, N), F32),
            pltpu.VMEM((gp, (WIN_TILES + 1) * T, N), F32),
            pltpu.VMEM((gp, SUBLANES, N), F32),
            pltpu.VMEM((gp, AUG_DIM, N), F32),
        ],
        compiler_params=_cparams("parallel", "parallel", "arbitrary"),
        name="nsa_attn",
    )(q_t, kk, kk, v_t, v_t, bias_t, gates, o_cmp, chosen)


def _fox_proj_body(h_ref, g_ref, wt_ref, bf_ref, q_ref, k_ref, vt_ref, carry_ref):
    tm = h_ref.shape[1]

    @pl.when(pl.program_id(1) == 0)
    def _():
        carry_ref[...] = jnp.zeros_like(carry_ref)

    xn = _rms(h_ref[0], g_ref[...]).astype(BF16)
    res_t = _dot_nt(wt_ref[...], xn)
    log_f = jax.nn.log_sigmoid(res_t[3 * Q_DIM:] + bf_ref[...])
    upper = jnp.where(lax.broadcasted_iota(jnp.int32, (tm, tm), 0)
                      <= lax.broadcasted_iota(jnp.int32, (tm, tm), 1), 1.0, 0.0).astype(BF16)
    hi, mid, lo = _split3(log_f)
    cum = _dot(hi, upper) + _dot(mid, upper) + _dot(lo, upper) + carry_ref[...]
    carry_ref[...] = cum[:, tm - 1:tm]
    terms = [t.astype(F32) for t in _split3(-LOG2E * cum)]
    pad = FOX_QK_DIM - HEAD_DIM
    k_zeros = jnp.zeros((pad - len(terms), tm), F32)
    q_extra = jnp.concatenate([_unit_rows(len(terms), tm), jnp.zeros((pad - SUBLANES, tm), BF16)], axis=0)
    v_extra = _unit_rows(1, tm)
    for hd in range(N_HEADS):
        rows = slice(hd * HEAD_DIM, (hd + 1) * HEAD_DIM)
        q_h = (res_t[rows] * Q_SCALE).astype(BF16)
        k_h = res_t[Q_DIM + hd * HEAD_DIM:Q_DIM + (hd + 1) * HEAD_DIM]
        v_h = res_t[2 * Q_DIM + hd * HEAD_DIM:2 * Q_DIM + (hd + 1) * HEAD_DIM].astype(BF16)
        k_t = jnp.concatenate([k_h] + [t[hd:hd + 1] for t in terms] + [k_zeros], axis=0)
        q_ref[0, hd] = jnp.concatenate([q_h, q_extra], axis=0)
        k_ref[0, hd] = k_t.T.astype(BF16)
        vt_ref[0, hd] = jnp.concatenate([v_h, v_extra], axis=0)


def _fox_proj(h3, g, w_t, b_f, layer, tm=1024):
    b, s, d = h3.shape
    t_spec = lambda rows: pl.BlockSpec((1, N_HEADS, rows, tm), lambda i, j: (i, 0, 0, j))
    t_shape = lambda rows: jax.ShapeDtypeStruct((b, N_HEADS, rows, s), BF16)
    return pl.pallas_call(
        _fox_proj_body,
        grid=(b, s // tm),
        in_specs=[
            pl.BlockSpec((1, tm, d), lambda i, j: (i, j, 0)),
            pl.BlockSpec((1, d), lambda i, j: (0, 0)),
            pl.BlockSpec((None, w_t.shape[1], d), lambda i, j: (layer, 0, 0)),
            pl.BlockSpec((None, N_HEADS, 1), lambda i, j: (layer, 0, 0)),
        ],
        out_specs=[t_spec(FOX_QK_DIM),
                   pl.BlockSpec((1, N_HEADS, tm, FOX_QK_DIM), lambda i, j: (i, 0, j, 0)),
                   t_spec(AUG_DIM)],
        out_shape=[t_shape(FOX_QK_DIM),
                   jax.ShapeDtypeStruct((b, N_HEADS, s, FOX_QK_DIM), BF16),
                   t_shape(AUG_DIM)],
        scratch_shapes=[pltpu.VMEM((N_HEADS, 1), F32)],
        compiler_params=_cparams("parallel", "arbitrary"),
        name="fox_proj",
    )(h3, g, w_t, b_f)


def _fox_attn_body(q_ref, k_ref, vt_ref, o_ref, *scratch):
    T = FOX_TILE
    n_q = q_ref.shape[3] // T
    n_heads = q_ref.shape[1]
    s_refs, p_refs = scratch[:len(scratch) // 2], scratch[len(scratch) // 2:]
    causal = (lax.broadcasted_iota(jnp.int32, (T, T), 0)
              <= lax.broadcasted_iota(jnp.int32, (T, T), 1))
    for qi in range(n_q):
        cols = slice(qi * T, (qi + 1) * T)
        for hh in range(n_heads):
            s_ref = s_refs[(qi % FOX_SLOTS) * n_heads + hh]
            p_ref = p_refs[(qi % FOX_SLOTS) * n_heads + hh]
            q_t = q_ref[0, hh, :, cols]
            peak = None
            for kt in range(qi + 1):
                keys = slice(kt * T, (kt + 1) * T)
                s = _dot(k_ref[0, hh, keys, :], q_t)
                if kt == qi:
                    s = jnp.where(causal, s, NEG_INF)
                s_ref[keys, :] = s
                part = _fold(s, jnp.max)
                peak = part if peak is None else jnp.maximum(peak, part)
            m = jnp.max(peak, axis=0, keepdims=True)
            for kt in range(qi + 1):
                keys = slice(kt * T, (kt + 1) * T)
                p_ref[keys, :] = jnp.exp2((s_ref[keys, :] - m).astype(BF16))
            extent = (qi + 1) * T
            acc = _dot(vt_ref[0, hh, :, :extent], p_ref[:extent, :])
            o = acc[:HEAD_DIM] * (1.0 / acc[HEAD_DIM:HEAD_DIM + 1])
            o_ref[0, hh * HEAD_DIM:(hh + 1) * HEAD_DIM, cols] = o.astype(BF16)


def _fox_attn(q_t, k, v_t, heads_per_step=2):
    b, nh, kdim, s = q_t.shape
    dh = HEAD_DIM
    T = FOX_TILE
    hp = heads_per_step
    t_spec = lambda rows: pl.BlockSpec((1, hp, rows, s), lambda i, h: (i, h, 0, 0))
    return pl.pallas_call(
        _fox_attn_body,
        grid=(b, nh // hp),
        in_specs=[t_spec(kdim), pl.BlockSpec((1, hp, s, kdim), lambda i, h: (i, h, 0, 0)), t_spec(AUG_DIM)],
        out_specs=pl.BlockSpec((1, hp * dh, s), lambda i, h: (i, h, 0)),
        out_shape=jax.ShapeDtypeStruct((b, nh * dh, s), BF16),
        scratch_shapes=([pltpu.VMEM((s, T), F32)] * (FOX_SLOTS * hp)
                        + [pltpu.VMEM((s, T), BF16)] * (FOX_SLOTS * hp)),
        compiler_params=_cparams("parallel", "parallel"),
        name="fox_attn",
    )(q_t, k, v_t)


def kernel(x, norm_g, ffn_w_gate, ffn_w_up, ffn_w_down, rel_bias, nsa_w_in, nsa_cmp_pe, nsa_cmp_w1,
           nsa_cmp_b1, nsa_cmp_w2, nsa_w_out, fox_w_in, fox_b_f, fox_w_out):
    b, s, d = x.shape
    depth = norm_g.shape[0]
    n = b * s
    t_last = lambda w: jnp.swapaxes(w, -1, -2)
    wg, wu, wd = ffn_w_gate.astype(BF16), ffn_w_up.astype(BF16), ffn_w_down.astype(BF16)
    c0 = Q_DIM
    col = lambda a: nsa_w_in[:, :, c0 + a * KV_DIM:c0 + (a + 1) * KV_DIM]
    gate_cols = np.arange(N_GATES).reshape(3, KV_GROUPS, Q_PER_GROUP).transpose(1, 0, 2).reshape(-1)
    w_gates = nsa_w_in[:, :, c0 + 6 * KV_DIM:][:, :, gate_cols]
    nsa_w_k = jnp.concatenate([col(0), col(1), col(2), col(4)], axis=-1).astype(BF16)
    nsa_w_t = t_last(jnp.concatenate([nsa_w_in[:, :, :c0], col(3), col(5), w_gates], axis=-1)).astype(BF16)
    nsa_w_out_b = nsa_w_out.astype(BF16)
    fox_w_t = t_last(fox_w_in).astype(BF16)
    fox_w_out_b = fox_w_out.astype(BF16)
    cmp_w1_b, cmp_w2_b = nsa_cmp_w1.astype(BF16), nsa_cmp_w2.astype(BF16)
    cmp_w2t_b = t_last(nsa_cmp_w2).astype(BF16)
    cmp_pe = nsa_cmp_pe.reshape(nsa_cmp_pe.shape[0], 2, 1, CMP_BLOCK * HEAD_DIM)
    cmp_b1 = nsa_cmp_b1[:, :, None, :]
    fox_bf = fox_b_f[:, :, None]
    gains = norm_g[:, :, None, :]

    bias_c, bias_t = _bias_tables(rel_bias, s)

    h = x.reshape(n, d)
    for i in range(depth):
        g = gains[i]
        j = i // 2
        h = _ffn(h, g[0], g[1], wg, wu, wd, i, 0)
        h3 = h.reshape(b, s, d)
        if i % 2 == 0:
            q_t, kk, v_t, gates = _nsa_proj(h3, g[2], nsa_w_k, nsa_w_t, j)
            k_cmp, v_cmp_t = _compress(kk, cmp_pe, cmp_w1_b, cmp_b1, cmp_w2_b, cmp_w2t_b, j)
            o_cmp, chosen = _nsa_select(q_t, k_cmp, v_cmp_t, bias_c)
            o_t = _nsa_attn(q_t, kk, v_t, o_cmp, chosen, bias_t, gates)
            w_out = nsa_w_out_b
        else:
            q_t, k, v_t = _fox_proj(h3, g[2], fox_w_t, fox_bf, j)
            o_t = _fox_attn(q_t, k, v_t)
            w_out = fox_w_out_b
        h = _ffn(h, g[4], g[5], wg, wu, wd, i, 1, mixer=(o_t, w_out, j, g[3]))
    return h.reshape(b, s, d)
```

```python
import math

import numpy as np
import jax
import jax.numpy as jnp
from jax import lax
from jax.experimental import pallas as pl
from jax.experimental.pallas import tpu as pltpu

N_HEADS = 16
HEAD_DIM = 64
KV_GROUPS = 4
Q_PER_GROUP = N_HEADS // KV_GROUPS
CMP_BLOCK = 32
CMP_STRIDE = 16
SEL_BLOCK = 64
SEL_SHIFT = 6
SEL_TOPK = 16
WINDOW = 512
NUM_BUCKETS = 32
MAX_DISTANCE = 128
RMS_EPS = 1e-6
NEG_INF = -1e30
FORCED_SCORE = 1e9
Q_DIM = N_HEADS * HEAD_DIM
KV_DIM = KV_GROUPS * HEAD_DIM
N_GATES = 3 * N_HEADS
LOG2E = math.log2(math.e)
Q_SCALE = HEAD_DIM ** -0.5 * LOG2E

SUBLANES = 8
ATT_TILE = 256
WIN_TILES = WINDOW // ATT_TILE
FAR_CHUNK = 512
GROUPS_PER_STEP = 2
FOX_TILE = 256
FOX_SLOTS = 2
AUG_DIM = HEAD_DIM + SUBLANES
FOX_QK_DIM = 2 * HEAD_DIM
VMEM_LIMIT = 56 * 1024 * 1024

BF16 = jnp.bfloat16
F32 = jnp.float32


def _cparams(*sem):
    return pltpu.CompilerParams(dimension_semantics=sem, vmem_limit_bytes=VMEM_LIMIT)


def _rms(x, g):
    return x * lax.rsqrt(jnp.mean(x * x, axis=-1, keepdims=True) + RMS_EPS) * g


def _dot(a, b):
    return jnp.dot(a, b, preferred_element_type=F32)


def _dot_nt(a, b):
    return lax.dot_general(a, b, (((1,), (1,)), ((), ())), preferred_element_type=F32)


def _dot_tn(a, b):
    return lax.dot_general(a, b, (((0,), (0,)), ((), ())), preferred_element_type=F32)


def _split3(x):
    hi = x.astype(BF16)
    r1 = x - hi.astype(F32)
    mid = r1.astype(BF16)
    lo = (r1 - mid.astype(F32)).astype(BF16)
    return hi, mid, lo


def _unit_rows(n_ones, width):
    row = lax.broadcasted_iota(jnp.int32, (SUBLANES, width), 0)
    return jnp.where(row < n_ones, 1.0, 0.0).astype(BF16)


def _fold(x, op):
    parts = x.reshape(x.shape[0] // SUBLANES, SUBLANES, x.shape[1])
    return op(parts, axis=0)


def _swiglu_halfstep(h, gpre_ref, gpost_ref, wg_ref, wu_ref, wd_ref):
    xn = _rms(h, gpre_ref[...]).astype(BF16)
    g = _dot(xn, wg_ref[...])
    u = _dot(xn, wu_ref[...])
    a = (g * jax.nn.sigmoid(g) * u).astype(BF16)
    return h + 0.5 * _rms(_dot(a, wd_ref[...]), gpost_ref[...])


def _ffn_body(h_ref, gpre_ref, gpost_ref, wg_ref, wu_ref, wd_ref, o_ref):
    o_ref[...] = _swiglu_halfstep(h_ref[...], gpre_ref, gpost_ref, wg_ref, wu_ref, wd_ref)


def _mix_ffn_body(h_ref, ot_ref, wo_ref, gmix_ref, gpre_ref, gpost_ref, wg_ref, wu_ref, wd_ref, o_ref):
    y = _dot_tn(ot_ref[0], wo_ref[...])
    h = h_ref[...] + _rms(y, gmix_ref[...])
    o_ref[...] = _swiglu_halfstep(h, gpre_ref, gpost_ref, wg_ref, wu_ref, wd_ref)


def _ffn(h, g_pre, g_post, wg, wu, wd, layer, half, mixer=None, tm=512):
    n, d = h.shape
    f = wg.shape[-1]
    once = dict(pipeline_mode=pl.Buffered(1))
    row = pl.BlockSpec((tm, d), lambda i: (i, 0))
    vec = pl.BlockSpec((1, d), lambda i: (0, 0))
    ffn_specs = [
        vec, vec,
        pl.BlockSpec((None, None, d, f), lambda i: (layer, half, 0, 0), **once),
        pl.BlockSpec((None, None, d, f), lambda i: (layer, half, 0, 0), **once),
        pl.BlockSpec((None, None, f, d), lambda i: (layer, half, 0, 0), **once),
    ]
    ffn_args = (g_pre, g_post, wg, wu, wd)
    if mixer is None:
        body, specs, args = _ffn_body, [row] + ffn_specs, (h,) + ffn_args
    else:
        o_t, w_out, mix_layer, g_mix = mixer
        kdim, s = o_t.shape[1:]
        per_seq = s // tm
        mix_specs = [
            pl.BlockSpec((1, kdim, tm), lambda i: (i // per_seq, 0, i % per_seq)),
            pl.BlockSpec((None, kdim, d), lambda i: (mix_layer, 0, 0), **once),
            vec,
        ]
        body, specs, args = _mix_ffn_body, [row] + mix_specs + ffn_specs, (h, o_t, w_out, g_mix) + ffn_args
    return pl.pallas_call(
        body,
        grid=(n // tm,),
        in_specs=specs,
        out_specs=row,
        out_shape=jax.ShapeDtypeStruct((n, d), F32),
        compiler_params=_cparams("parallel"),
        name="ffn" if mixer is None else "mix_ffn",
    )(*args)


def _nsa_proj_body(h_ref, g_ref, w_ref, wt_ref, qt_ref, kc_ref, k_ref, vt_ref, gate_ref):
    xn = _rms(h_ref[0], g_ref[...]).astype(BF16)
    res = _dot(xn, w_ref[...])
    for a in range(2 * KV_GROUPS):
        kc_ref[0, a] = res[:, a * HEAD_DIM:(a + 1) * HEAD_DIM].astype(BF16)
    for a in range(2 * KV_GROUPS):
        lo = KV_DIM * 2 + a * HEAD_DIM
        k_ref[0, a] = res[:, lo:lo + HEAD_DIM].astype(BF16)
    res_t = _dot_nt(wt_ref[...], xn)
    for hd in range(N_HEADS):
        qt_ref[0, hd] = (res_t[hd * HEAD_DIM:(hd + 1) * HEAD_DIM] * Q_SCALE).astype(BF16)
    extra = _unit_rows(1, res_t.shape[1])
    for a in range(2 * KV_GROUPS):
        lo = Q_DIM + a * HEAD_DIM
        vt_ref[0, a] = jnp.concatenate([res_t[lo:lo + HEAD_DIM].astype(BF16), extra], axis=0)
    gates = jax.nn.sigmoid(res_t[Q_DIM + 2 * KV_DIM:])
    width = 3 * Q_PER_GROUP
    for grp in range(KV_GROUPS):
        gate_ref[0, grp] = gates[grp * width:(grp + 1) * width]


def _nsa_proj(h3, g, w, w_t, layer, tm=1024):
    b, s, d = h3.shape
    return pl.pallas_call(
        _nsa_proj_body,
        grid=(b, s // tm),
        in_specs=[
            pl.BlockSpec((1, tm, d), lambda i, j: (i, j, 0)),
            pl.BlockSpec((1, d), lambda i, j: (0, 0)),
            pl.BlockSpec((None, d, w.shape[-1]), lambda i, j: (layer, 0, 0)),
            pl.BlockSpec((None, w_t.shape[1], d), lambda i, j: (layer, 0, 0)),
        ],
        out_specs=[
            pl.BlockSpec((1, N_HEADS, HEAD_DIM, tm), lambda i, j: (i, 0, 0, j)),
            pl.BlockSpec((1, 2 * KV_GROUPS, tm, HEAD_DIM), lambda i, j: (i, 0, j, 0)),
            pl.BlockSpec((1, 2 * KV_GROUPS, tm, HEAD_DIM), lambda i, j: (i, 0, j, 0)),
            pl.BlockSpec((1, 2 * KV_GROUPS, AUG_DIM, tm), lambda i, j: (i, 0, 0, j)),
            pl.BlockSpec((1, KV_GROUPS, 3 * Q_PER_GROUP, tm), lambda i, j: (i, 0, 0, j)),
        ],
        out_shape=[
            jax.ShapeDtypeStruct((b, N_HEADS, HEAD_DIM, s), BF16),
            jax.ShapeDtypeStruct((b, 2 * KV_GROUPS, s, HEAD_DIM), BF16),
            jax.ShapeDtypeStruct((b, 2 * KV_GROUPS, s, HEAD_DIM), BF16),
            jax.ShapeDtypeStruct((b, 2 * KV_GROUPS, AUG_DIM, s), BF16),
            jax.ShapeDtypeStruct((b, KV_GROUPS, 3 * Q_PER_GROUP, s), F32),
        ],
        compiler_params=_cparams("parallel", "parallel"),
        name="nsa_proj",
    )(h3, g, w, w_t)


def _compress_body(x_ref, pe_ref, w1_ref, b1_ref, w2_ref, w2t_ref, o_ref, ot_ref):
    n_chunk = x_ref.shape[3]
    half = CMP_STRIDE * HEAD_DIM
    x = x_ref[0, 0].reshape(KV_GROUPS * n_chunk, half)
    top = _dot(x, w1_ref[:half, :])
    bot = _dot(x, w1_ref[half:, :])
    bot_next = pltpu.roll(bot, KV_GROUPS * n_chunk - 1, 0)
    pe = jnp.broadcast_to(pe_ref[...].astype(BF16), (8, 2 * half))
    const = _dot(pe, w1_ref[...])[0:1] + b1_ref[...]
    hid = jax.nn.gelu(top + bot_next + const).astype(BF16)
    out = _dot(hid, w2_ref[...])
    row = lax.broadcasted_iota(jnp.int32, out.shape, 0) & (n_chunk - 1)
    o_ref[0, 0] = jnp.where(row < n_chunk - 1, out, 0.0).reshape(KV_GROUPS, n_chunk, HEAD_DIM).astype(BF16)
    out_t = _dot_nt(w2t_ref[...], hid)
    col = lax.broadcasted_iota(jnp.int32, out_t.shape, 1) & (n_chunk - 1)
    out_t = jnp.where(col < n_chunk - 1, out_t, 0.0).astype(BF16)
    for grp in range(KV_GROUPS):
        ot_ref[0, 0, grp] = out_t[:, grp * n_chunk:(grp + 1) * n_chunk]


def _compress(kcv, pe, w1, b1, w2, w2t, layer):
    b, _, s, dh = kcv.shape
    n_chunk = s // CMP_STRIDE
    x = kcv.reshape(b, 2, KV_GROUPS, n_chunk, CMP_STRIDE * dh)
    hidden = w1.shape[-1]
    return pl.pallas_call(
        _compress_body,
        grid=(b, 2),
        in_specs=[
            pl.BlockSpec((1, 1, KV_GROUPS, n_chunk, CMP_STRIDE * dh), lambda i, a: (i, a, 0, 0, 0)),
            pl.BlockSpec((None, None, 1, CMP_BLOCK * dh), lambda i, a: (layer, a, 0, 0)),
            pl.BlockSpec((None, None, CMP_BLOCK * dh, hidden), lambda i, a: (layer, a, 0, 0)),
            pl.BlockSpec((None, None, 1, hidden), lambda i, a: (layer, a, 0, 0)),
            pl.BlockSpec((None, None, hidden, dh), lambda i, a: (layer, a, 0, 0)),
            pl.BlockSpec((None, None, dh, hidden), lambda i, a: (layer, a, 0, 0)),
        ],
        out_specs=[
            pl.BlockSpec((1, 1, KV_GROUPS, n_chunk, dh), lambda i, a: (i, a, 0, 0, 0)),
            pl.BlockSpec((1, 1, KV_GROUPS, dh, n_chunk), lambda i, a: (i, a, 0, 0, 0)),
        ],
        out_shape=[
            jax.ShapeDtypeStruct((b, 2, KV_GROUPS, n_chunk, dh), BF16),
            jax.ShapeDtypeStruct((b, 2, KV_GROUPS, dh, n_chunk), BF16),
        ],
        compiler_params=_cparams("parallel", "parallel"),
        name="nsa_compress",
    )(x, pe, w1, b1, w2, w2t)


def _t5_bucket_np(rel):
    n = np.maximum(rel, 0)
    max_exact = NUM_BUCKETS // 2
    nf = np.maximum(n, 1).astype(np.float32)
    ratio = np.log(nf / np.float32(max_exact)) / np.float32(math.log(MAX_DISTANCE / max_exact))
    large = max_exact + (ratio * np.float32(NUM_BUCKETS - max_exact)).astype(np.int32)
    large = np.minimum(large, NUM_BUCKETS - 1)
    return np.where(n < max_exact, n, large).astype(np.int32)


def _bucket_maps(s):
    n_chunk = s // CMP_STRIDE
    t = np.arange(s)[None, :]
    blk_end = np.arange(n_chunk)[:, None] * CMP_STRIDE + CMP_BLOCK - 1
    rel_c = t - blk_end
    map_c = np.where(rel_c >= 0, _t5_bucket_np(rel_c), -1).astype(np.int32)
    j = np.arange(ATT_TILE)[:, None]
    i = np.arange(ATT_TILE)[None, :]
    diag = np.where(i - j >= 0, _t5_bucket_np(i - j), -1)
    sub = _t5_bucket_np(ATT_TILE + i - j)
    edge = np.where(j > i, _t5_bucket_np(WINDOW + i - j), -1)
    map_t = np.stack([diag, sub, edge]).astype(np.int32)
    assert _t5_bucket_np(np.arange(ATT_TILE + 1, s + WINDOW)).min() == _FAR_BUCKET
    return map_c, map_t


_FAR_BUCKET = NUM_BUCKETS - 1
TILE_DIAG, TILE_SUB, TILE_EDGE = 0, 1, 2


def _bias_body(rb_ref, mc_ref, mt_ref, bc_ref, bt_ref):
    hd = pl.program_id(0)

    def lookup(bucket, shift):
        acc = jnp.zeros(bucket.shape, F32)
        for bk in range(NUM_BUCKETS):
            acc = jnp.where(bucket == bk, (rb_ref[bk, hd] - shift) * LOG2E, acc)
        return jnp.where(bucket < 0, NEG_INF, acc)

    bc_ref[0] = lookup(mc_ref[...], 0.0)
    for d in range(mt_ref.shape[0]):
        bt_ref[d, 0] = lookup(mt_ref[d], rb_ref[_FAR_BUCKET, hd])


def _bias_tables(rel_bias, s):
    map_c, map_t = _bucket_maps(s)
    n_chunk = map_c.shape[0]
    n_tab = map_t.shape[0]
    T = ATT_TILE
    R = Q_PER_GROUP
    return pl.pallas_call(
        _bias_body,
        grid=(N_HEADS,),
        in_specs=[
            pl.BlockSpec(memory_space=pltpu.SMEM),
            pl.BlockSpec((n_chunk, s), lambda i: (0, 0)),
            pl.BlockSpec((n_tab, T, T), lambda i: (0, 0, 0)),
        ],
        out_specs=[
            pl.BlockSpec((1, n_chunk, s), lambda i: (i, 0, 0)),
            pl.BlockSpec((n_tab, 1, T, T), lambda i: (0, i // R, 0, i % R)),
        ],
        out_shape=[
            jax.ShapeDtypeStruct((N_HEADS, n_chunk, s), F32),
            jax.ShapeDtypeStruct((n_tab, KV_GROUPS, T, R * T), F32),
        ],
        compiler_params=_cparams("parallel"),
        name="t5_bias_tables",
    )(rel_bias, jnp.asarray(map_c), jnp.asarray(map_t))


def _nsa_select_body(q_ref, kc_ref, vct_ref, bc_ref, ovl_ref, oc_ref, ch_ref):
    T = ATT_TILE
    R = Q_PER_GROUP
    n_sel = ovl_ref.shape[0]
    ovl = ovl_ref[...]
    j_blk = lax.broadcasted_iota(jnp.int32, (n_sel, T), 0)
    j_slab = lax.broadcasted_iota(jnp.int32, (SUBLANES, T), 0)
    for u in range(q_ref.shape[3] // T):
        qi = pl.program_id(1) * (q_ref.shape[3] // T) + u
        cols = slice(u * T, (u + 1) * T)
        q_t = jnp.concatenate([q_ref[0, r, :, cols] for r in range(R)], axis=1)
        bias = jnp.concatenate([bc_ref[r, :, cols] for r in range(R)], axis=1)
        s_c = _dot(kc_ref[0, 0, 0], q_t) + bias
        m_c = jnp.max(s_c, axis=0, keepdims=True)
        p_c = jnp.exp2(s_c - m_c)
        p_c = p_c * (1.0 / jnp.sum(p_c, axis=0, keepdims=True))
        t_col = qi * T + (lax.broadcasted_iota(jnp.int32, (1, R * T), 1) & (T - 1))
        p_c = jnp.where(t_col >= CMP_BLOCK - 1, p_c, 0.0)
        o_c = _dot(vct_ref[0, 0, 0], p_c.astype(BF16))
        for r in range(R):
            oc_ref[0, r, :, cols] = o_c[:, r * T:(r + 1) * T]

        p_sum = p_c[:, :T]
        for r in range(1, R):
            p_sum = p_sum + p_c[:, r * T:(r + 1) * T]
        hi, mid, lo = _split3(p_sum)
        imp = _dot(ovl, hi) + _dot(ovl, mid) + _dot(ovl, lo)
        cur = (qi * T + lax.broadcasted_iota(jnp.int32, (n_sel, T), 1)) >> SEL_SHIFT
        forced = (j_blk == 0) | (j_blk == cur) | (j_blk == cur - 1)
        imp = jnp.where(forced, FORCED_SCORE, jnp.where(j_blk <= cur, imp, NEG_INF))
        slabs = [imp[lo:lo + SUBLANES] for lo in range(0, n_sel, SUBLANES)]
        ranks = [jnp.zeros((SUBLANES, T), F32) for _ in slabs]
        for i in range(n_sel):
            row = imp[i:i + 1, :]
            for k, slab in enumerate(slabs):
                lo = k * SUBLANES
                if lo > i:
                    hit = jnp.where(row >= slab, 1.0, 0.0)
                elif lo + SUBLANES - 1 <= i:
                    hit = jnp.where(row > slab, 1.0, 0.0)
                else:
                    hit = jnp.where(j_slab > i - lo, jnp.where(row >= slab, 1.0, 0.0),
                                    jnp.where(row > slab, 1.0, 0.0))
                ranks[k] = ranks[k] + hit
        rank = jnp.concatenate(ranks, axis=0)
        ch_ref[0, 0, :, cols] = jnp.where(rank < min(SEL_TOPK, n_sel), 1.0, 0.0)


def _nsa_select(q_t, k_cmp, v_cmp_t, bias_c, tiles_per_step=8):
    b, _, dh, s = q_t.shape
    R = Q_PER_GROUP
    n_chunk = s // CMP_STRIDE
    n_sel = s // SEL_BLOCK
    tq = min(tiles_per_step * ATT_TILE, s)
    c_start = np.arange(n_chunk)[None, :] * CMP_STRIDE
    j = np.arange(n_sel)[:, None]
    overlap = (c_start < (j + 1) * SEL_BLOCK) & (c_start + CMP_BLOCK > j * SEL_BLOCK)
    overlap[:, n_chunk - 1] = False
    return pl.pallas_call(
        _nsa_select_body,
        grid=(KV_GROUPS, s // tq, b),
        in_specs=[
            pl.BlockSpec((1, R, dh, tq), lambda g, t, i: (i, g, 0, t)),
            pl.BlockSpec((1, 1, 1, n_chunk, dh), lambda g, t, i: (i, 0, g, 0, 0)),
            pl.BlockSpec((1, 1, 1, dh, n_chunk), lambda g, t, i: (i, 1, g, 0, 0)),
            pl.BlockSpec((R, n_chunk, tq), lambda g, t, i: (g, 0, t)),
            pl.BlockSpec((n_sel, n_chunk), lambda g, t, i: (0, 0)),
        ],
        out_specs=[
            pl.BlockSpec((1, R, dh, tq), lambda g, t, i: (i, g, 0, t)),
            pl.BlockSpec((1, 1, n_sel, tq), lambda g, t, i: (i, g, 0, t)),
        ],
        out_shape=[
            jax.ShapeDtypeStruct((b, N_HEADS, dh, s), F32),
            jax.ShapeDtypeStruct((b, KV_GROUPS, n_sel, s), F32),
        ],
        compiler_params=_cparams("parallel", "parallel", "parallel"),
        name="nsa_select",
    )(q_t, k_cmp, v_cmp_t, bias_c, jnp.asarray(overlap, BF16))


def _nsa_attn_body(q_ref, ks_ref, kw_ref, vst_ref, vwt_ref, bt_ref, gate_ref, oc_ref, ch_ref,
                   o_ref, rows_ref, s_ref, near_ref, win_ref, peak_ref, acc_ref):
    T = ATT_TILE
    R = Q_PER_GROUP
    C = FAR_CHUNK
    N = R * T
    groups = range(ks_ref.shape[1])
    qi = pl.program_id(2)
    n_far = jnp.maximum(qi - 1, 0)
    n_chunks = (n_far + C // T - 1) // (C // T)
    qs = [jnp.concatenate([q_ref[0, g * R + r] for r in range(R)], axis=1) for g in groups]

    def add_block_rows(g, which, s, off):
        first = off // SEL_BLOCK
        pieces = [s[b * SEL_BLOCK:(b + 1) * SEL_BLOCK] + rows_ref[g, which, pl.ds(first + b, 1), :]
                  for b in range(s.shape[0] // SEL_BLOCK)]
        return jnp.concatenate(pieces, axis=0)

    def near_logits(g, k_ref, d, table, masked):
        off = pl.multiple_of(jnp.maximum(qi - d, 0) * T, T)
        s = _dot(k_ref[0, g, pl.ds(off, T), :], qs[g])
        if table is not None:
            s = s + bt_ref[table, g]
        if masked:
            s = add_block_rows(g, 0, s, off)
        if d > 0:
            s = jnp.where(qi >= d, s, NEG_INF)
        return s, off

    def logits_pass(g, k_ref, tiles, masked, out_ref):
        offs = []
        peak = None
        for idx, (d, table) in enumerate(tiles):
            s, off = near_logits(g, k_ref, d, table, masked)
            out_ref[g, idx * T:(idx + 1) * T, :] = s
            part = _fold(s, jnp.max)
            peak = part if peak is None else jnp.maximum(peak, part)
            offs.append(off)
        return offs, peak

    def values_pass(g, logit_ref, vt_ref, offs, m):
        acc = None
        for idx, off in enumerate(offs):
            p = jnp.exp2((logit_ref[g, idx * T:(idx + 1) * T, :] - m).astype(BF16))
            pv = _dot(vt_ref[0, g, :, pl.ds(off, T)], p)
            acc = pv if acc is None else acc + pv
        return acc

    win_tiles = [(0, TILE_DIAG), (1, TILE_SUB)]
    win_tiles += [(d, None) for d in range(2, WIN_TILES)] + [(WIN_TILES, TILE_EDGE)]
    near_sel = [(1, TILE_SUB), (0, TILE_DIAG)]
    win_offs, m_win, sel_offs = [], [], []
    blk = lax.broadcasted_iota(jnp.int32, (ch_ref.shape[2], N), 0)
    for g in groups:
        add = (ch_ref[0, g] - 1.0) * -NEG_INF
        add = jnp.concatenate([add] * R, axis=1)
        rows_ref[g, 0] = add
        rows_ref[g, 1] = jnp.where(blk < n_far * (T // SEL_BLOCK), add, NEG_INF)
        offs, peak = logits_pass(g, kw_ref, win_tiles, False, win_ref)
        win_offs.append(offs)
        m_win.append(jnp.max(peak, axis=0, keepdims=True))
        offs, peak = logits_pass(g, ks_ref, near_sel, True, near_ref)
        sel_offs.append(offs)
        peak_ref[g] = peak

    def far_logits(c, carry):
        off = pl.multiple_of(c * C, C)
        for g in groups:
            s = _dot(ks_ref[0, g, pl.ds(off, C), :], qs[g])
            s = add_block_rows(g, 1, s, off)
            s_ref[g, pl.ds(off, C), :] = s
            peak_ref[g] = jnp.maximum(peak_ref[g], _fold(s, jnp.max))
        return carry

    lax.fori_loop(0, n_chunks, far_logits, 0)

    m_sel = []
    for g in groups:
        m_sel.append(jnp.max(peak_ref[g], axis=0, keepdims=True))
        acc_ref[g] = values_pass(g, near_ref, vst_ref, sel_offs[g], m_sel[g])

    def far_values(c, carry):
        off = pl.multiple_of(c * C, C)
        for g in groups:
            p = jnp.exp2((s_ref[g, pl.ds(off, C), :] - m_sel[g]).astype(BF16))
            acc_ref[g] += _dot(vst_ref[0, g, :, pl.ds(off, C)], p)
        return carry

    lax.fori_loop(0, n_chunks, far_values, 0)

    for g in groups:
        gate = gate_ref[0, g]
        gate_of = lambda br: jnp.concatenate(
            [gate[br * R + r:br * R + r + 1, :] for r in range(R)], axis=1)
        win = values_pass(g, win_ref, vwt_ref, win_offs[g], m_win[g])
        sel = acc_ref[g]
        w_win = gate_of(2) * (1.0 / win[HEAD_DIM:HEAD_DIM + 1])
        w_sel = gate_of(1) * (1.0 / sel[HEAD_DIM:HEAD_DIM + 1])
        o_c = jnp.concatenate([oc_ref[0, g * R + r] for r in range(R)], axis=1)
        o = gate_of(0) * o_c + w_sel * sel[:HEAD_DIM] + w_win * win[:HEAD_DIM]
        for r in range(R):
            hd = g * R + r
            o_ref[0, hd * HEAD_DIM:(hd + 1) * HEAD_DIM, :] = o[:, r * T:(r + 1) * T].astype(BF16)


def _nsa_attn(q_t, kk, v_t, o_cmp, chosen, bias_t, gates):
    b, _, dh, s = q_t.shape
    T = ATT_TILE
    R = Q_PER_GROUP
    N = R * T
    gp = GROUPS_PER_STEP
    per = KV_GROUPS // gp
    n_sel = s // SEL_BLOCK
    far_keys = max(s - 2 * T, FAR_CHUNK)
    once = dict(pipeline_mode=pl.Buffered(1))
    k_spec = lambda a: pl.BlockSpec((1, gp, s, dh), lambda g, i, t: (i, a * per + g, 0, 0))
    vt_spec = lambda a: pl.BlockSpec((1, gp, AUG_DIM, s), lambda g, i, t: (i, a * per + g, 0, 0))
    return pl.pallas_call(
        _nsa_attn_body,
        grid=(per, b, s // T),
        in_specs=[
            pl.BlockSpec((1, gp * R, dh, T), lambda g, i, t: (i, g, 0, t)),
            k_spec(0), k_spec(1), vt_spec(0), vt_spec(1),
            pl.BlockSpec((bias_t.shape[0], gp, T, N), lambda g, i, t: (0, g, 0, 0), **once),
            pl.BlockSpec((1, gp, 3 * R, T), lambda g, i, t: (i, g, 0, t)),
            pl.BlockSpec((1, gp * R, dh, T), lambda g, i, t: (i, g, 0, t)),
            pl.BlockSpec((1, gp, n_sel, T), lambda g, i, t: (i, g, 0, t)),
        ],
        out_specs=pl.BlockSpec((1, gp * R * dh, T), lambda g, i, t: (i, g, t)),
        out_shape=jax.ShapeDtypeStruct((b, Q_DIM, s), BF16),
        scratch_shapes=[
            pltpu.VMEM((gp, 2, n_sel, N), F32),
            pltpu.VMEM((gp, far_keys, N), F32),
            pltpu.VMEM((gp, 2 * T, N), F32),
            pltpu.VMEM((gp, (WIN_TILES + 1) * T, N), F32),
            pltpu.VMEM((gp, SUBLANES, N), F32),
            pltpu.VMEM((gp, AUG_DIM, N), F32),
        ],
        compiler_params=_cparams("parallel", "parallel", "arbitrary"),
        name="nsa_attn",
    )(q_t, kk, kk, v_t, v_t, bias_t, gates, o_cmp, chosen)


def _fox_proj_body(h_ref, g_ref, wt_ref, bf_ref, q_ref, k_ref, vt_ref, carry_ref):
    tm = h_ref.shape[1]

    @pl.when(pl.program_id(1) == 0)
    def _():
        carry_ref[...] = jnp.zeros_like(carry_ref)

    xn = _rms(h_ref[0], g_ref[...]).astype(BF16)
    res_t = _dot_nt(wt_ref[...], xn)
    log_f = jax.nn.log_sigmoid(res_t[3 * Q_DIM:] + bf_ref[...])
    upper = jnp.where(lax.broadcasted_iota(jnp.int32, (tm, tm), 0)
                      <= lax.broadcasted_iota(jnp.int32, (tm, tm), 1), 1.0, 0.0).astype(BF16)
    hi, mid, lo = _split3(log_f)
    cum = _dot(hi, upper) + _dot(mid, upper) + _dot(lo, upper) + carry_ref[...]
    carry_ref[...] = cum[:, tm - 1:tm]
    terms = [t.astype(F32) for t in _split3(-LOG2E * cum)]
    pad = FOX_QK_DIM - HEAD_DIM
    k_zeros = jnp.zeros((pad - len(terms), tm), F32)
    q_extra = jnp.concatenate([_unit_rows(len(terms), tm), jnp.zeros((pad - SUBLANES, tm), BF16)], axis=0)
    v_extra = _unit_rows(1, tm)
    for hd in range(N_HEADS):
        rows = slice(hd * HEAD_DIM, (hd + 1) * HEAD_DIM)
        q_h = (res_t[rows] * Q_SCALE).astype(BF16)
        k_h = res_t[Q_DIM + hd * HEAD_DIM:Q_DIM + (hd + 1) * HEAD_DIM]
        v_h = res_t[2 * Q_DIM + hd * HEAD_DIM:2 * Q_DIM + (hd + 1) * HEAD_DIM].astype(BF16)
        k_t = jnp.concatenate([k_h] + [t[hd:hd + 1] for t in terms] + [k_zeros], axis=0)
        q_ref[0, hd] = jnp.concatenate([q_h, q_extra], axis=0)
        k_ref[0, hd] = k_t.T.astype(BF16)
        vt_ref[0, hd] = jnp.concatenate([v_h, v_extra], axis=0)


def _fox_proj(h3, g, w_t, b_f, layer, tm=1024):
    b, s, d = h3.shape
    t_spec = lambda rows: pl.BlockSpec((1, N_HEADS, rows, tm), lambda i, j: (i, 0, 0, j))
    t_shape = lambda rows: jax.ShapeDtypeStruct((b, N_HEADS, rows, s), BF16)
    return pl.pallas_call(
        _fox_proj_body,
        grid=(b, s // tm),
        in_specs=[
            pl.BlockSpec((1, tm, d), lambda i, j: (i, j, 0)),
            pl.BlockSpec((1, d), lambda i, j: (0, 0)),
            pl.BlockSpec((None, w_t.shape[1], d), lambda i, j: (layer, 0, 0)),
            pl.BlockSpec((None, N_HEADS, 1), lambda i, j: (layer, 0, 0)),
        ],
        out_specs=[t_spec(FOX_QK_DIM),
                   pl.BlockSpec((1, N_HEADS, tm, FOX_QK_DIM), lambda i, j: (i, 0, j, 0)),
                   t_spec(AUG_DIM)],
        out_shape=[t_shape(FOX_QK_DIM),
                   jax.ShapeDtypeStruct((b, N_HEADS, s, FOX_QK_DIM), BF16),
                   t_shape(AUG_DIM)],
        scratch_shapes=[pltpu.VMEM((N_HEADS, 1), F32)],
        compiler_params=_cparams("parallel", "arbitrary"),
        name="fox_proj",
    )(h3, g, w_t, b_f)


def _fox_attn_body(q_ref, k_ref, vt_ref, o_ref, *scratch):
    T = FOX_TILE
    n_q = q_ref.shape[3] // T
    n_heads = q_ref.shape[1]
    s_refs, p_refs = scratch[:len(scratch) // 2], scratch[len(scratch) // 2:]
    causal = (lax.broadcasted_iota(jnp.int32, (T, T), 0)
              <= lax.broadcasted_iota(jnp.int32, (T, T), 1))
    for qi in range(n_q):
        cols = slice(qi * T, (qi + 1) * T)
        for hh in range(n_heads):
            s_ref = s_refs[(qi % FOX_SLOTS) * n_heads + hh]
            p_ref = p_refs[(qi % FOX_SLOTS) * n_heads + hh]
            q_t = q_ref[0, hh, :, cols]
            peak = None
            for kt in range(qi + 1):
                keys = slice(kt * T, (kt + 1) * T)
                s = _dot(k_ref[0, hh, keys, :], q_t)
                if kt == qi:
                    s = jnp.where(causal, s, NEG_INF)
                s_ref[keys, :] = s
                part = _fold(s, jnp.max)
                peak = part if peak is None else jnp.maximum(peak, part)
            m = jnp.max(peak, axis=0, keepdims=True)
            for kt in range(qi + 1):
                keys = slice(kt * T, (kt + 1) * T)
                p_ref[keys, :] = jnp.exp2((s_ref[keys, :] - m).astype(BF16))
            extent = (qi + 1) * T
            acc = _dot(vt_ref[0, hh, :, :extent], p_ref[:extent, :])
            o = acc[:HEAD_DIM] * (1.0 / acc[HEAD_DIM:HEAD_DIM + 1])
            o_ref[0, hh * HEAD_DIM:(hh + 1) * HEAD_DIM, cols] = o.astype(BF16)


def _fox_attn(q_t, k, v_t, heads_per_step=2):
    b, nh, kdim, s = q_t.shape
    dh = HEAD_DIM
    T = FOX_TILE
    hp = heads_per_step
    t_spec = lambda rows: pl.BlockSpec((1, hp, rows, s), lambda i, h: (i, h, 0, 0))
    return pl.pallas_call(
        _fox_attn_body,
        grid=(b, nh // hp),
        in_specs=[t_spec(kdim), pl.BlockSpec((1, hp, s, kdim), lambda i, h: (i, h, 0, 0)), t_spec(AUG_DIM)],
        out_specs=pl.BlockSpec((1, hp * dh, s), lambda i, h: (i, h, 0)),
        out_shape=jax.ShapeDtypeStruct((b, nh * dh, s), BF16),
        scratch_shapes=([pltpu.VMEM((s, T), F32)] * (FOX_SLOTS * hp)
                        + [pltpu.VMEM((s, T), BF16)] * (FOX_SLOTS * hp)),
        compiler_params=_cparams("parallel", "parallel"),
        name="fox_attn",
    )(q_t, k, v_t)


def kernel(x, norm_g, ffn_w_gate, ffn_w_up, ffn_w_down, rel_bias, nsa_w_in, nsa_cmp_pe, nsa_cmp_w1,
           nsa_cmp_b1, nsa_cmp_w2, nsa_w_out, fox_w_in, fox_b_f, fox_w_out):
    b, s, d = x.shape
    depth = norm_g.shape[0]
    n = b * s
    t_last = lambda w: jnp.swapaxes(w, -1, -2)
    wg, wu, wd = ffn_w_gate.astype(BF16), ffn_w_up.astype(BF16), ffn_w_down.astype(BF16)
    c0 = Q_DIM
    col = lambda a: nsa_w_in[:, :, c0 + a * KV_DIM:c0 + (a + 1) * KV_DIM]
    gate_cols = np.arange(N_GATES).reshape(3, KV_GROUPS, Q_PER_GROUP).transpose(1, 0, 2).reshape(-1)
    w_gates = nsa_w_in[:, :, c0 + 6 * KV_DIM:][:, :, gate_cols]
    nsa_w_k = jnp.concatenate([col(0), col(1), col(2), col(4)], axis=-1).astype(BF16)
    nsa_w_t = t_last(jnp.concatenate([nsa_w_in[:, :, :c0], col(3), col(5), w_gates], axis=-1)).astype(BF16)
    nsa_w_out_b = nsa_w_out.astype(BF16)
    fox_w_t = t_last(fox_w_in).astype(BF16)
    fox_w_out_b = fox_w_out.astype(BF16)
    cmp_w1_b, cmp_w2_b = nsa_cmp_w1.astype(BF16), nsa_cmp_w2.astype(BF16)
    cmp_w2t_b = t_last(nsa_cmp_w2).astype(BF16)
    cmp_pe = nsa_cmp_pe.reshape(nsa_cmp_pe.shape[0], 2, 1, CMP_BLOCK * HEAD_DIM)
    cmp_b1 = nsa_cmp_b1[:, :, None, :]
    fox_bf = fox_b_f[:, :, None]
    gains = norm_g[:, :, None, :]

    bias_c, bias_t = _bias_tables(rel_bias, s)

    h = x.reshape(n, d)
    for i in range(depth):
        g = gains[i]
        j = i // 2
        h = _ffn(h, g[0], g[1], wg, wu, wd, i, 0)
        h3 = h.reshape(b, s, d)
        if i % 2 == 0:
            q_t, kcv, kk, v_t, gates = _nsa_proj(h3, g[2], nsa_w_k, nsa_w_t, j)
            k_cmp, v_cmp_t = _compress(kcv, cmp_pe, cmp_w1_b, cmp_b1, cmp_w2_b, cmp_w2t_b, j)
            o_cmp, chosen = _nsa_select(q_t, k_cmp, v_cmp_t, bias_c)
            o_t = _nsa_attn(q_t, kk, v_t, o_cmp, chosen, bias_t, gates)
            w_out = nsa_w_out_b
        else:
            q_t, k, v_t = _fox_proj(h3, g[2], fox_w_t, fox_bf, j)
            o_t = _fox_attn(q_t, k, v_t)
            w_out = fox_w_out_b
        h = _ffn(h, g[4], g[5], wg, wu, wd, i, 1, mixer=(o_t, w_out, j, g[3]))
    return h.reshape(b, s, d)
```

```python
import math

import numpy as np
import jax
import jax.numpy as jnp
from jax import lax
from jax.experimental import pallas as pl
from jax.experimental.pallas import tpu as pltpu

N_HEADS = 16
HEAD_DIM = 64
KV_GROUPS = 4
Q_PER_GROUP = N_HEADS // KV_GROUPS
CMP_BLOCK = 32
CMP_STRIDE = 16
SEL_BLOCK = 64
SEL_SHIFT = 6
SEL_TOPK = 16
WINDOW = 512
NUM_BUCKETS = 32
MAX_DISTANCE = 128
RMS_EPS = 1e-6
NEG_INF = -1e30
FORCED_SCORE = 1e9
Q_DIM = N_HEADS * HEAD_DIM
KV_DIM = KV_GROUPS * HEAD_DIM
N_GATES = 3 * N_HEADS
LOG2E = math.log2(math.e)
Q_SCALE = HEAD_DIM ** -0.5 * LOG2E

SUBLANES = 8
ATT_TILE = 256
WIN_TILES = WINDOW // ATT_TILE
FAR_CHUNK = 512
GROUPS_PER_STEP = 2
FOX_TILE = 256
FOX_SLOTS = 2
AUG_DIM = HEAD_DIM + SUBLANES
FOX_QK_DIM = 2 * HEAD_DIM
VMEM_LIMIT = 56 * 1024 * 1024

BF16 = jnp.bfloat16
F32 = jnp.float32


def _cparams(*sem):
    return pltpu.CompilerParams(dimension_semantics=sem, vmem_limit_bytes=VMEM_LIMIT)


def _rms(x, g):
    return x * lax.rsqrt(jnp.mean(x * x, axis=-1, keepdims=True) + RMS_EPS) * g


def _dot(a, b):
    return jnp.dot(a, b, preferred_element_type=F32)


def _dot_nt(a, b):
    return lax.dot_general(a, b, (((1,), (1,)), ((), ())), preferred_element_type=F32)


def _dot_tn(a, b):
    return lax.dot_general(a, b, (((0,), (0,)), ((), ())), preferred_element_type=F32)


def _split3(x):
    hi = x.astype(BF16)
    r1 = x - hi.astype(F32)
    mid = r1.astype(BF16)
    lo = (r1 - mid.astype(F32)).astype(BF16)
    return hi, mid, lo


def _unit_rows(n_ones, width):
    row = lax.broadcasted_iota(jnp.int32, (SUBLANES, width), 0)
    return jnp.where(row < n_ones, 1.0, 0.0).astype(BF16)


def _fold(x, op):
    parts = x.reshape(x.shape[0] // SUBLANES, SUBLANES, x.shape[1])
    return op(parts, axis=0)


def _swiglu_halfstep(h, gpre_ref, gpost_ref, wg_ref, wu_ref, wd_ref):
    xn = _rms(h, gpre_ref[...]).astype(BF16)
    g = _dot(xn, wg_ref[...])
    u = _dot(xn, wu_ref[...])
    a = (g * jax.nn.sigmoid(g) * u).astype(BF16)
    return h + 0.5 * _rms(_dot(a, wd_ref[...]), gpost_ref[...])


def _ffn_body(h_ref, gpre_ref, gpost_ref, wg_ref, wu_ref, wd_ref, o_ref):
    o_ref[...] = _swiglu_halfstep(h_ref[...], gpre_ref, gpost_ref, wg_ref, wu_ref, wd_ref)


def _mix_ffn_body(h_ref, ot_ref, wo_ref, gmix_ref, gpre_ref, gpost_ref, wg_ref, wu_ref, wd_ref, o_ref):
    y = _dot_tn(ot_ref[0], wo_ref[...])
    h = h_ref[...] + _rms(y, gmix_ref[...])
    o_ref[...] = _swiglu_halfstep(h, gpre_ref, gpost_ref, wg_ref, wu_ref, wd_ref)


def _ffn(h, g_pre, g_post, wg, wu, wd, layer, half, mixer=None, tm=512):
    n, d = h.shape
    f = wg.shape[-1]
    once = dict(pipeline_mode=pl.Buffered(1))
    row = pl.BlockSpec((tm, d), lambda i: (i, 0))
    vec = pl.BlockSpec((1, d), lambda i: (0, 0))
    ffn_specs = [
        vec, vec,
        pl.BlockSpec((None, None, d, f), lambda i: (layer, half, 0, 0), **once),
        pl.BlockSpec((None, None, d, f), lambda i: (layer, half, 0, 0), **once),
        pl.BlockSpec((None, None, f, d), lambda i: (layer, half, 0, 0), **once),
    ]
    ffn_args = (g_pre, g_post, wg, wu, wd)
    if mixer is None:
        body, specs, args = _ffn_body, [row] + ffn_specs, (h,) + ffn_args
    else:
        o_t, w_out, mix_layer, g_mix = mixer
        kdim, s = o_t.shape[1:]
        per_seq = s // tm
        mix_specs = [
            pl.BlockSpec((1, kdim, tm), lambda i: (i // per_seq, 0, i % per_seq)),
            pl.BlockSpec((None, kdim, d), lambda i: (mix_layer, 0, 0), **once),
            vec,
        ]
        body, specs, args = _mix_ffn_body, [row] + mix_specs + ffn_specs, (h, o_t, w_out, g_mix) + ffn_args
    return pl.pallas_call(
        body,
        grid=(n // tm,),
        in_specs=specs,
        out_specs=row,
        out_shape=jax.ShapeDtypeStruct((n, d), F32),
        compiler_params=_cparams("parallel"),
        name="ffn" if mixer is None else "mix_ffn",
    )(*args)


def _nsa_proj_body(h_ref, g_ref, w_ref, wt_ref, qt_ref, kc_ref, k_ref, vt_ref, gate_ref):
    xn = _rms(h_ref[0], g_ref[...]).astype(BF16)
    res = _dot(xn, w_ref[...])
    for a in range(2 * KV_GROUPS):
        kc_ref[0, a] = res[:, a * HEAD_DIM:(a + 1) * HEAD_DIM].astype(BF16)
    for a in range(2 * KV_GROUPS):
        lo = KV_DIM * 2 + a * HEAD_DIM
        k_ref[0, a] = res[:, lo:lo + HEAD_DIM].astype(BF16)
    res_t = _dot_nt(wt_ref[...], xn)
    for hd in range(N_HEADS):
        qt_ref[0, hd] = (res_t[hd * HEAD_DIM:(hd + 1) * HEAD_DIM] * Q_SCALE).astype(BF16)
    extra = _unit_rows(1, res_t.shape[1])
    for a in range(2 * KV_GROUPS):
        lo = Q_DIM + a * HEAD_DIM
        vt_ref[0, a] = jnp.concatenate([res_t[lo:lo + HEAD_DIM].astype(BF16), extra], axis=0)
    gates = jax.nn.sigmoid(res_t[Q_DIM + 2 * KV_DIM:])
    width = 3 * Q_PER_GROUP
    for grp in range(KV_GROUPS):
        gate_ref[0, grp] = gates[grp * width:(grp + 1) * width]


def _nsa_proj(h3, g, w, w_t, layer, tm=1024):
    b, s, d = h3.shape
    return pl.pallas_call(
        _nsa_proj_body,
        grid=(b, s // tm),
        in_specs=[
            pl.BlockSpec((1, tm, d), lambda i, j: (i, j, 0)),
            pl.BlockSpec((1, d), lambda i, j: (0, 0)),
            pl.BlockSpec((None, d, w.shape[-1]), lambda i, j: (layer, 0, 0)),
            pl.BlockSpec((None, w_t.shape[1], d), lambda i, j: (layer, 0, 0)),
        ],
        out_specs=[
            pl.BlockSpec((1, N_HEADS, HEAD_DIM, tm), lambda i, j: (i, 0, 0, j)),
            pl.BlockSpec((1, 2 * KV_GROUPS, tm, HEAD_DIM), lambda i, j: (i, 0, j, 0)),
            pl.BlockSpec((1, 2 * KV_GROUPS, tm, HEAD_DIM), lambda i, j: (i, 0, j, 0)),
            pl.BlockSpec((1, 2 * KV_GROUPS, AUG_DIM, tm), lambda i, j: (i, 0, 0, j)),
            pl.BlockSpec((1, KV_GROUPS, 3 * Q_PER_GROUP, tm), lambda i, j: (i, 0, 0, j)),
        ],
        out_shape=[
            jax.ShapeDtypeStruct((b, N_HEADS, HEAD_DIM, s), BF16),
            jax.ShapeDtypeStruct((b, 2 * KV_GROUPS, s, HEAD_DIM), BF16),
            jax.ShapeDtypeStruct((b, 2 * KV_GROUPS, s, HEAD_DIM), BF16),
            jax.ShapeDtypeStruct((b, 2 * KV_GROUPS, AUG_DIM, s), BF16),
            jax.ShapeDtypeStruct((b, KV_GROUPS, 3 * Q_PER_GROUP, s), F32),
        ],
        compiler_params=_cparams("parallel", "parallel"),
        name="nsa_proj",
    )(h3, g, w, w_t)


def _compress_body(x_ref, pe_ref, w1_ref, b1_ref, w2_ref, w2t_ref, o_ref, ot_ref):
    n_chunk = x_ref.shape[3]
    half = CMP_STRIDE * HEAD_DIM
    x = x_ref[0, 0].reshape(KV_GROUPS * n_chunk, half)
    top = _dot(x, w1_ref[:half, :])
    bot = _dot(x, w1_ref[half:, :])
    bot_next = pltpu.roll(bot, KV_GROUPS * n_chunk - 1, 0)
    pe = jnp.broadcast_to(pe_ref[...].astype(BF16), (8, 2 * half))
    const = _dot(pe, w1_ref[...])[0:1] + b1_ref[...]
    hid = jax.nn.gelu(top + bot_next + const).astype(BF16)
    out = _dot(hid, w2_ref[...])
    row = lax.broadcasted_iota(jnp.int32, out.shape, 0) & (n_chunk - 1)
    o_ref[0, 0] = jnp.where(row < n_chunk - 1, out, 0.0).reshape(KV_GROUPS, n_chunk, HEAD_DIM).astype(BF16)
    out_t = _dot_nt(w2t_ref[...], hid)
    col = lax.broadcasted_iota(jnp.int32, out_t.shape, 1) & (n_chunk - 1)
    out_t = jnp.where(col < n_chunk - 1, out_t, 0.0).astype(BF16)
    for grp in range(KV_GROUPS):
        ot_ref[0, 0, grp] = out_t[:, grp * n_chunk:(grp + 1) * n_chunk]


def _compress(kcv, pe, w1, b1, w2, w2t, layer):
    b, _, s, dh = kcv.shape
    n_chunk = s // CMP_STRIDE
    x = kcv.reshape(b, 2, KV_GROUPS, n_chunk, CMP_STRIDE * dh)
    hidden = w1.shape[-1]
    return pl.pallas_call(
        _compress_body,
        grid=(b, 2),
        in_specs=[
            pl.BlockSpec((1, 1, KV_GROUPS, n_chunk, CMP_STRIDE * dh), lambda i, a: (i, a, 0, 0, 0)),
            pl.BlockSpec((None, None, 1, CMP_BLOCK * dh), lambda i, a: (layer, a, 0, 0)),
            pl.BlockSpec((None, None, CMP_BLOCK * dh, hidden), lambda i, a: (layer, a, 0, 0)),
            pl.BlockSpec((None, None, 1, hidden), lambda i, a: (layer, a, 0, 0)),
            pl.BlockSpec((None, None, hidden, dh), lambda i, a: (layer, a, 0, 0)),
            pl.BlockSpec((None, None, dh, hidden), lambda i, a: (layer, a, 0, 0)),
        ],
        out_specs=[
            pl.BlockSpec((1, 1, KV_GROUPS, n_chunk, dh), lambda i, a: (i, a, 0, 0, 0)),
            pl.BlockSpec((1, 1, KV_GROUPS, dh, n_chunk), lambda i, a: (i, a, 0, 0, 0)),
        ],
        out_shape=[
            jax.ShapeDtypeStruct((b, 2, KV_GROUPS, n_chunk, dh), BF16),
            jax.ShapeDtypeStruct((b, 2, KV_GROUPS, dh, n_chunk), BF16),
        ],
        compiler_params=_cparams("parallel", "parallel"),
        name="nsa_compress",
    )(x, pe, w1, b1, w2, w2t)


def _t5_bucket_np(rel):
    n = np.maximum(rel, 0)
    max_exact = NUM_BUCKETS // 2
    nf = np.maximum(n, 1).astype(np.float32)
    ratio = np.log(nf / np.float32(max_exact)) / np.float32(math.log(MAX_DISTANCE / max_exact))
    large = max_exact + (ratio * np.float32(NUM_BUCKETS - max_exact)).astype(np.int32)
    large = np.minimum(large, NUM_BUCKETS - 1)
    return np.where(n < max_exact, n, large).astype(np.int32)


def _bucket_maps(s):
    n_chunk = s // CMP_STRIDE
    t = np.arange(s)[None, :]
    blk_end = np.arange(n_chunk)[:, None] * CMP_STRIDE + CMP_BLOCK - 1
    rel_c = t - blk_end
    map_c = np.where(rel_c >= 0, _t5_bucket_np(rel_c), -1).astype(np.int32)
    j = np.arange(ATT_TILE)[:, None]
    i = np.arange(ATT_TILE)[None, :]
    diag = np.where(i - j >= 0, _t5_bucket_np(i - j), -1)
    sub = _t5_bucket_np(ATT_TILE + i - j)
    edge = np.where(j > i, _t5_bucket_np(WINDOW + i - j), -1)
    map_t = np.stack([diag, sub, edge]).astype(np.int32)
    assert _t5_bucket_np(np.arange(ATT_TILE + 1, s + WINDOW)).min() == _FAR_BUCKET
    return map_c, map_t


_FAR_BUCKET = NUM_BUCKETS - 1
TILE_DIAG, TILE_SUB, TILE_EDGE = 0, 1, 2


def _bias_body(rb_ref, mc_ref, mt_ref, bc_ref, bt_ref):
    hd = pl.program_id(0)

    def lookup(bucket, shift):
        level = [(rb_ref[bk, hd] - shift) * LOG2E for bk in range(NUM_BUCKETS)]
        bit = 1
        while len(level) > 1:
            odd = (bucket & bit) != 0
            level = [jnp.where(odd, level[k + 1], level[k]) for k in range(0, len(level), 2)]
            bit *= 2
        return jnp.where(bucket < 0, NEG_INF, level[0])

    n_chunk, s = mc_ref.shape
    lane = 128
    far = rb_ref[_FAR_BUCKET, hd] * LOG2E
    for c0 in range(0, n_chunk, SUBLANES):
        first = c0 * CMP_STRIDE + CMP_BLOCK - 1
        last = (c0 + SUBLANES - 1) * CMP_STRIDE + CMP_BLOCK - 1 + MAX_DISTANCE
        lo = min(first // lane * lane, s)
        hi = min(-(-last // lane) * lane, s)
        rows = slice(c0, c0 + SUBLANES)
        if lo > 0:
            bc_ref[0, rows, :lo] = jnp.full((SUBLANES, lo), NEG_INF, F32)
        if hi > lo:
            bc_ref[0, rows, lo:hi] = lookup(mc_ref[rows, lo:hi], 0.0)
        if hi < s:
            bc_ref[0, rows, hi:] = jnp.full((SUBLANES, s - hi), far, F32)
    for d in range(mt_ref.shape[0]):
        bt_ref[d, 0] = lookup(mt_ref[d], rb_ref[_FAR_BUCKET, hd])


def _bias_tables(rel_bias, s):
    map_c, map_t = _bucket_maps(s)
    n_chunk = map_c.shape[0]
    n_tab = map_t.shape[0]
    T = ATT_TILE
    R = Q_PER_GROUP
    return pl.pallas_call(
        _bias_body,
        grid=(N_HEADS,),
        in_specs=[
            pl.BlockSpec(memory_space=pltpu.SMEM),
            pl.BlockSpec((n_chunk, s), lambda i: (0, 0)),
            pl.BlockSpec((n_tab, T, T), lambda i: (0, 0, 0)),
        ],
        out_specs=[
            pl.BlockSpec((1, n_chunk, s), lambda i: (i, 0, 0)),
            pl.BlockSpec((n_tab, 1, T, T), lambda i: (0, i // R, 0, i % R)),
        ],
        out_shape=[
            jax.ShapeDtypeStruct((N_HEADS, n_chunk, s), F32),
            jax.ShapeDtypeStruct((n_tab, KV_GROUPS, T, R * T), F32),
        ],
        compiler_params=_cparams("parallel"),
        name="t5_bias_tables",
    )(rel_bias, jnp.asarray(map_c), jnp.asarray(map_t))


def _nsa_select_body(q_ref, kc_ref, vct_ref, bc_ref, ovl_ref, oc_ref, ch_ref):
    T = ATT_TILE
    R = Q_PER_GROUP
    n_sel = ovl_ref.shape[0]
    ovl = ovl_ref[...]
    j_blk = lax.broadcasted_iota(jnp.int32, (n_sel, T), 0)
    j_slab = lax.broadcasted_iota(jnp.int32, (SUBLANES, T), 0)
    for u in range(q_ref.shape[3] // T):
        qi = pl.program_id(1) * (q_ref.shape[3] // T) + u
        cols = slice(u * T, (u + 1) * T)
        q_t = jnp.concatenate([q_ref[0, r, :, cols] for r in range(R)], axis=1)
        bias = jnp.concatenate([bc_ref[r, :, cols] for r in range(R)], axis=1)
        s_c = _dot(kc_ref[0, 0, 0], q_t) + bias
        m_c = jnp.max(s_c, axis=0, keepdims=True)
        p_c = jnp.exp2(s_c - m_c)
        p_c = p_c * (1.0 / jnp.sum(p_c, axis=0, keepdims=True))
        t_col = qi * T + (lax.broadcasted_iota(jnp.int32, (1, R * T), 1) & (T - 1))
        p_c = jnp.where(t_col >= CMP_BLOCK - 1, p_c, 0.0)
        o_c = _dot(vct_ref[0, 0, 0], p_c.astype(BF16))
        for r in range(R):
            oc_ref[0, r, :, cols] = o_c[:, r * T:(r + 1) * T]

        p_sum = p_c[:, :T]
        for r in range(1, R):
            p_sum = p_sum + p_c[:, r * T:(r + 1) * T]
        hi, mid, lo = _split3(p_sum)
        imp = _dot(ovl, hi) + _dot(ovl, mid) + _dot(ovl, lo)
        cur = (qi * T + lax.broadcasted_iota(jnp.int32, (n_sel, T), 1)) >> SEL_SHIFT
        forced = (j_blk == 0) | (j_blk == cur) | (j_blk == cur - 1)
        imp = jnp.where(forced, FORCED_SCORE, jnp.where(j_blk <= cur, imp, NEG_INF))
        slabs = [imp[lo:lo + SUBLANES] for lo in range(0, n_sel, SUBLANES)]
        ranks = [jnp.zeros((SUBLANES, T), F32) for _ in slabs]
        for i in range(n_sel):
            row = imp[i:i + 1, :]
            for k, slab in enumerate(slabs):
                lo = k * SUBLANES
                if lo > i:
                    hit = jnp.where(row >= slab, 1.0, 0.0)
                elif lo + SUBLANES - 1 <= i:
                    hit = jnp.where(row > slab, 1.0, 0.0)
                else:
                    hit = jnp.where(j_slab > i - lo, jnp.where(row >= slab, 1.0, 0.0),
                                    jnp.where(row > slab, 1.0, 0.0))
                ranks[k] = ranks[k] + hit
        rank = jnp.concatenate(ranks, axis=0)
        ch_ref[0, 0, :, cols] = jnp.where(rank < min(SEL_TOPK, n_sel), 1.0, 0.0)


def _nsa_select(q_t, k_cmp, v_cmp_t, bias_c, tiles_per_step=8):
    b, _, dh, s = q_t.shape
    R = Q_PER_GROUP
    n_chunk = s // CMP_STRIDE
    n_sel = s // SEL_BLOCK
    tq = min(tiles_per_step * ATT_TILE, s)
    c_start = np.arange(n_chunk)[None, :] * CMP_STRIDE
    j = np.arange(n_sel)[:, None]
    overlap = (c_start < (j + 1) * SEL_BLOCK) & (c_start + CMP_BLOCK > j * SEL_BLOCK)
    overlap[:, n_chunk - 1] = False
    return pl.pallas_call(
        _nsa_select_body,
        grid=(KV_GROUPS, s // tq, b),
        in_specs=[
            pl.BlockSpec((1, R, dh, tq), lambda g, t, i: (i, g, 0, t)),
            pl.BlockSpec((1, 1, 1, n_chunk, dh), lambda g, t, i: (i, 0, g, 0, 0)),
            pl.BlockSpec((1, 1, 1, dh, n_chunk), lambda g, t, i: (i, 1, g, 0, 0)),
            pl.BlockSpec((R, n_chunk, tq), lambda g, t, i: (g, 0, t)),
            pl.BlockSpec((n_sel, n_chunk), lambda g, t, i: (0, 0)),
        ],
        out_specs=[
            pl.BlockSpec((1, R, dh, tq), lambda g, t, i: (i, g, 0, t)),
            pl.BlockSpec((1, 1, n_sel, tq), lambda g, t, i: (i, g, 0, t)),
        ],
        out_shape=[
            jax.ShapeDtypeStruct((b, N_HEADS, dh, s), F32),
            jax.ShapeDtypeStruct((b, KV_GROUPS, n_sel, s), F32),
        ],
        compiler_params=_cparams("parallel", "parallel", "parallel"),
        name="nsa_select",
    )(q_t, k_cmp, v_cmp_t, bias_c, jnp.asarray(overlap, BF16))


def _nsa_attn_body(q_ref, ks_ref, kw_ref, vst_ref, vwt_ref, bt_ref, gate_ref, oc_ref, ch_ref,
                   o_ref, rows_ref, s_ref, near_ref, win_ref, peak_ref, acc_ref):
    T = ATT_TILE
    R = Q_PER_GROUP
    C = FAR_CHUNK
    N = R * T
    groups = range(ks_ref.shape[1])
    qi = pl.program_id(2)
    n_far = jnp.maximum(qi - 1, 0)
    n_chunks = (n_far + C // T - 1) // (C // T)
    qs = [jnp.concatenate([q_ref[0, g * R + r] for r in range(R)], axis=1) for g in groups]

    def add_block_rows(g, which, s, off):
        first = off // SEL_BLOCK
        pieces = [s[b * SEL_BLOCK:(b + 1) * SEL_BLOCK] + rows_ref[g, which, pl.ds(first + b, 1), :]
                  for b in range(s.shape[0] // SEL_BLOCK)]
        return jnp.concatenate(pieces, axis=0)

    def near_logits(g, k_ref, d, table, masked):
        off = pl.multiple_of(jnp.maximum(qi - d, 0) * T, T)
        s = _dot(k_ref[0, g, pl.ds(off, T), :], qs[g])
        if table is not None:
            s = s + bt_ref[table, g]
        if masked:
            s = add_block_rows(g, 0, s, off)
        if d > 0:
            s = jnp.where(qi >= d, s, NEG_INF)
        return s, off

    def logits_pass(g, k_ref, tiles, masked, out_ref):
        offs = []
        peak = None
        for idx, (d, table) in enumerate(tiles):
            s, off = near_logits(g, k_ref, d, table, masked)
            out_ref[g, idx * T:(idx + 1) * T, :] = s
            part = _fold(s, jnp.max)
            peak = part if peak is None else jnp.maximum(peak, part)
            offs.append(off)
        return offs, peak

    def values_pass(g, logit_ref, vt_ref, offs, m):
        acc = None
        for idx, off in enumerate(offs):
            p = jnp.exp2((logit_ref[g, idx * T:(idx + 1) * T, :] - m).astype(BF16))
            pv = _dot(vt_ref[0, g, :, pl.ds(off, T)], p)
            acc = pv if acc is None else acc + pv
        return acc

    win_tiles = [(0, TILE_DIAG), (1, TILE_SUB)]
    win_tiles += [(d, None) for d in range(2, WIN_TILES)] + [(WIN_TILES, TILE_EDGE)]
    near_sel = [(1, TILE_SUB), (0, TILE_DIAG)]
    win_offs, m_win, sel_offs = [], [], []
    blk = lax.broadcasted_iota(jnp.int32, (ch_ref.shape[2], N), 0)
    for g in groups:
        add = (ch_ref[0, g] - 1.0) * -NEG_INF
        add = jnp.concatenate([add] * R, axis=1)
        rows_ref[g, 0] = add
        rows_ref[g, 1] = jnp.where(blk < n_far * (T // SEL_BLOCK), add, NEG_INF)
        offs, peak = logits_pass(g, kw_ref, win_tiles, False, win_ref)
        win_offs.append(offs)
        m_win.append(jnp.max(peak, axis=0, keepdims=True))
        offs, peak = logits_pass(g, ks_ref, near_sel, True, near_ref)
        sel_offs.append(offs)
        peak_ref[g] = peak

    def far_logits(c, carry):
        off = pl.multiple_of(c * C, C)
        for g in groups:
            s = _dot(ks_ref[0, g, pl.ds(off, C), :], qs[g])
            s = add_block_rows(g, 1, s, off)
            s_ref[g, pl.ds(off, C), :] = s
            peak_ref[g] = jnp.maximum(peak_ref[g], _fold(s, jnp.max))
        return carry

    lax.fori_loop(0, n_chunks, far_logits, 0)

    m_sel = []
    for g in groups:
        m_sel.append(jnp.max(peak_ref[g], axis=0, keepdims=True))
        acc_ref[g] = values_pass(g, near_ref, vst_ref, sel_offs[g], m_sel[g])

    def far_values(c, carry):
        off = pl.multiple_of(c * C, C)
        for g in groups:
            p = jnp.exp2((s_ref[g, pl.ds(off, C), :] - m_sel[g]).astype(BF16))
            acc_ref[g] += _dot(vst_ref[0, g, :, pl.ds(off, C)], p)
        return carry

    lax.fori_loop(0, n_chunks, far_values, 0)

    for g in groups:
        gate = gate_ref[0, g]
        gate_of = lambda br: jnp.concatenate(
            [gate[br * R + r:br * R + r + 1, :] for r in range(R)], axis=1)
        win = values_pass(g, win_ref, vwt_ref, win_offs[g], m_win[g])
        sel = acc_ref[g]
        w_win = gate_of(2) * (1.0 / win[HEAD_DIM:HEAD_DIM + 1])
        w_sel = gate_of(1) * (1.0 / sel[HEAD_DIM:HEAD_DIM + 1])
        o_c = jnp.concatenate([oc_ref[0, g * R + r] for r in range(R)], axis=1)
        o = gate_of(0) * o_c + w_sel * sel[:HEAD_DIM] + w_win * win[:HEAD_DIM]
        for r in range(R):
            hd = g * R + r
            o_ref[0, hd * HEAD_DIM:(hd + 1) * HEAD_DIM, :] = o[:, r * T:(r + 1) * T].astype(BF16)


def _nsa_attn(q_t, kk, v_t, o_cmp, chosen, bias_t, gates):
    b, _, dh, s = q_t.shape
    T = ATT_TILE
    R = Q_PER_GROUP
    N = R * T
    gp = GROUPS_PER_STEP
    per = KV_GROUPS // gp
    n_sel = s // SEL_BLOCK
    far_keys = max(s - 2 * T, FAR_CHUNK)
    once = dict(pipeline_mode=pl.Buffered(1))
    k_spec = lambda a: pl.BlockSpec((1, gp, s, dh), lambda g, i, t: (i, a * per + g, 0, 0))
    vt_spec = lambda a: pl.BlockSpec((1, gp, AUG_DIM, s), lambda g, i, t: (i, a * per + g, 0, 0))
    return pl.pallas_call(
        _nsa_attn_body,
        grid=(per, b, s // T),
        in_specs=[
            pl.BlockSpec((1, gp * R, dh, T), lambda g, i, t: (i, g, 0, t)),
            k_spec(0), k_spec(1), vt_spec(0), vt_spec(1),
            pl.BlockSpec((bias_t.shape[0], gp, T, N), lambda g, i, t: (0, g, 0, 0), **once),
            pl.BlockSpec((1, gp, 3 * R, T), lambda g, i, t: (i, g, 0, t)),
            pl.BlockSpec((1, gp * R, dh, T), lambda g, i, t: (i, g, 0, t)),
            pl.BlockSpec((1, gp, n_sel, T), lambda g, i, t: (i, g, 0, t)),
        ],
        out_specs=pl.BlockSpec((1, gp * R * dh, T), lambda g, i, t: (i, g, t)),
        out_shape=jax.ShapeDtypeStruct((b, Q_DIM, s), BF16),
        scratch_shapes=[
            pltpu.VMEM((gp, 2, n_sel, N), F32),
            pltpu.VMEM((gp, far_keys, N), F32),
            pltpu.VMEM((gp, 2 * T, N), F32),
            pltpu.VMEM((gp, (WIN_TILES + 1) * T, N), F32),
            pltpu.VMEM((gp, SUBLANES, N), F32),
            pltpu.VMEM((gp, AUG_DIM, N), F32),
        ],
        compiler_params=_cparams("parallel", "parallel", "arbitrary"),
        name="nsa_attn",
    )(q_t, kk, kk, v_t, v_t, bias_t, gates, o_cmp, chosen)


def _fox_proj_body(h_ref, g_ref, wt_ref, bf_ref, q_ref, k_ref, vt_ref, carry_ref):
    tm = h_ref.shape[1]

    @pl.when(pl.program_id(1) == 0)
    def _():
        carry_ref[...] = jnp.zeros_like(carry_ref)

    xn = _rms(h_ref[0], g_ref[...]).astype(BF16)
    res_t = _dot_nt(wt_ref[...], xn)
    log_f = jax.nn.log_sigmoid(res_t[3 * Q_DIM:] + bf_ref[...])
    upper = jnp.where(lax.broadcasted_iota(jnp.int32, (tm, tm), 0)
                      <= lax.broadcasted_iota(jnp.int32, (tm, tm), 1), 1.0, 0.0).astype(BF16)
    hi, mid, lo = _split3(log_f)
    cum = _dot(hi, upper) + _dot(mid, upper) + _dot(lo, upper) + carry_ref[...]
    carry_ref[...] = cum[:, tm - 1:tm]
    terms = [t.astype(F32) for t in _split3(-LOG2E * cum)]
    pad = FOX_QK_DIM - HEAD_DIM
    k_zeros = jnp.zeros((pad - len(terms), tm), F32)
    q_extra = jnp.concatenate([_unit_rows(len(terms), tm), jnp.zeros((pad - SUBLANES, tm), BF16)], axis=0)
    v_extra = _unit_rows(1, tm)
    for hd in range(N_HEADS):
        rows = slice(hd * HEAD_DIM, (hd + 1) * HEAD_DIM)
        q_h = (res_t[rows] * Q_SCALE).astype(BF16)
        k_h = res_t[Q_DIM + hd * HEAD_DIM:Q_DIM + (hd + 1) * HEAD_DIM]
        v_h = res_t[2 * Q_DIM + hd * HEAD_DIM:2 * Q_DIM + (hd + 1) * HEAD_DIM].astype(BF16)
        k_t = jnp.concatenate([k_h] + [t[hd:hd + 1] for t in terms] + [k_zeros], axis=0)
        q_ref[0, hd] = jnp.concatenate([q_h, q_extra], axis=0)
        k_ref[0, hd] = k_t.T.astype(BF16)
        vt_ref[0, hd] = jnp.concatenate([v_h, v_extra], axis=0)


def _fox_proj(h3, g, w_t, b_f, layer, tm=1024):
    b, s, d = h3.shape
    t_spec = lambda rows: pl.BlockSpec((1, N_HEADS, rows, tm), lambda i, j: (i, 0, 0, j))
    t_shape = lambda rows: jax.ShapeDtypeStruct((b, N_HEADS, rows, s), BF16)
    return pl.pallas_call(
        _fox_proj_body,
        grid=(b, s // tm),
        in_specs=[
            pl.BlockSpec((1, tm, d), lambda i, j: (i, j, 0)),
            pl.BlockSpec((1, d), lambda i, j: (0, 0)),
            pl.BlockSpec((None, w_t.shape[1], d), lambda i, j: (layer, 0, 0)),
            pl.BlockSpec((None, N_HEADS, 1), lambda i, j: (layer, 0, 0)),
        ],
        out_specs=[t_spec(FOX_QK_DIM),
                   pl.BlockSpec((1, N_HEADS, tm, FOX_QK_DIM), lambda i, j: (i, 0, j, 0)),
                   t_spec(AUG_DIM)],
        out_shape=[t_shape(FOX_QK_DIM),
                   jax.ShapeDtypeStruct((b, N_HEADS, s, FOX_QK_DIM), BF16),
                   t_shape(AUG_DIM)],
        scratch_shapes=[pltpu.VMEM((N_HEADS, 1), F32)],
        compiler_params=_cparams("parallel", "arbitrary"),
        name="fox_proj",
    )(h3, g, w_t, b_f)


def _fox_attn_body(q_ref, k_ref, vt_ref, o_ref, *scratch):
    T = FOX_TILE
    n_q = q_ref.shape[3] // T
    n_heads = q_ref.shape[1]
    s_refs, p_refs = scratch[:len(scratch) // 2], scratch[len(scratch) // 2:]
    causal = (lax.broadcasted_iota(jnp.int32, (T, T), 0)
              <= lax.broadcasted_iota(jnp.int32, (T, T), 1))
    for qi in range(n_q):
        cols = slice(qi * T, (qi + 1) * T)
        for hh in range(n_heads):
            s_ref = s_refs[(qi % FOX_SLOTS) * n_heads + hh]
            p_ref = p_refs[(qi % FOX_SLOTS) * n_heads + hh]
            q_t = q_ref[0, hh, :, cols]
            peak = None
            for kt in range(qi + 1):
                keys = slice(kt * T, (kt + 1) * T)
                s = _dot(k_ref[0, hh, keys, :], q_t)
                if kt == qi:
                    s = jnp.where(causal, s, NEG_INF)
                s_ref[keys, :] = s
                part = _fold(s, jnp.max)
                peak = part if peak is None else jnp.maximum(peak, part)
            m = jnp.max(peak, axis=0, keepdims=True)
            for kt in range(qi + 1):
                keys = slice(kt * T, (kt + 1) * T)
                p_ref[keys, :] = jnp.exp2((s_ref[keys, :] - m).astype(BF16))
            extent = (qi + 1) * T
            acc = _dot(vt_ref[0, hh, :, :extent], p_ref[:extent, :])
            o = acc[:HEAD_DIM] * (1.0 / acc[HEAD_DIM:HEAD_DIM + 1])
            o_ref[0, hh * HEAD_DIM:(hh + 1) * HEAD_DIM, cols] = o.astype(BF16)


def _fox_attn(q_t, k, v_t, heads_per_step=2):
    b, nh, kdim, s = q_t.shape
    dh = HEAD_DIM
    T = FOX_TILE
    hp = heads_per_step
    t_spec = lambda rows: pl.BlockSpec((1, hp, rows, s), lambda i, h: (i, h, 0, 0))
    return pl.pallas_call(
        _fox_attn_body,
        grid=(b, nh // hp),
        in_specs=[t_spec(kdim), pl.BlockSpec((1, hp, s, kdim), lambda i, h: (i, h, 0, 0)), t_spec(AUG_DIM)],
        out_specs=pl.BlockSpec((1, hp * dh, s), lambda i, h: (i, h, 0)),
        out_shape=jax.ShapeDtypeStruct((b, nh * dh, s), BF16),
        scratch_shapes=([pltpu.VMEM((s, T), F32)] * (FOX_SLOTS * hp)
                        + [pltpu.VMEM((s, T), BF16)] * (FOX_SLOTS * hp)),
        compiler_params=_cparams("parallel", "parallel"),
        name="fox_attn",
    )(q_t, k, v_t)


def kernel(x, norm_g, ffn_w_gate, ffn_w_up, ffn_w_down, rel_bias, nsa_w_in, nsa_cmp_pe, nsa_cmp_w1,
           nsa_cmp_b1, nsa_cmp_w2, nsa_w_out, fox_w_in, fox_b_f, fox_w_out):
    b, s, d = x.shape
    depth = norm_g.shape[0]
    n = b * s
    t_last = lambda w: jnp.swapaxes(w, -1, -2)
    wg, wu, wd = ffn_w_gate.astype(BF16), ffn_w_up.astype(BF16), ffn_w_down.astype(BF16)
    c0 = Q_DIM
    col = lambda a: nsa_w_in[:, :, c0 + a * KV_DIM:c0 + (a + 1) * KV_DIM]
    gate_cols = np.arange(N_GATES).reshape(3, KV_GROUPS, Q_PER_GROUP).transpose(1, 0, 2).reshape(-1)
    w_gates = nsa_w_in[:, :, c0 + 6 * KV_DIM:][:, :, gate_cols]
    nsa_w_k = jnp.concatenate([col(0), col(1), col(2), col(4)], axis=-1).astype(BF16)
    nsa_w_t = t_last(jnp.concatenate([nsa_w_in[:, :, :c0], col(3), col(5), w_gates], axis=-1)).astype(BF16)
    nsa_w_out_b = nsa_w_out.astype(BF16)
    fox_w_t = t_last(fox_w_in).astype(BF16)
    fox_w_out_b = fox_w_out.astype(BF16)
    cmp_w1_b, cmp_w2_b = nsa_cmp_w1.astype(BF16), nsa_cmp_w2.astype(BF16)
    cmp_w2t_b = t_last(nsa_cmp_w2).astype(BF16)
    cmp_pe = nsa_cmp_pe.reshape(nsa_cmp_pe.shape[0], 2, 1, CMP_BLOCK * HEAD_DIM)
    cmp_b1 = nsa_cmp_b1[:, :, None, :]
    fox_bf = fox_b_f[:, :, None]
    gains = norm_g[:, :, None, :]

    bias_c, bias_t = _bias_tables(rel_bias, s)

    h = x.reshape(n, d)
    for i in range(depth):
        g = gains[i]
        j = i // 2
        h = _ffn(h, g[0], g[1], wg, wu, wd, i, 0)
        h3 = h.reshape(b, s, d)
        if i % 2 == 0:
            q_t, kcv, kk, v_t, gates = _nsa_proj(h3, g[2], nsa_w_k, nsa_w_t, j)
            k_cmp, v_cmp_t = _compress(kcv, cmp_pe, cmp_w1_b, cmp_b1, cmp_w2_b, cmp_w2t_b, j)
            o_cmp, chosen = _nsa_select(q_t, k_cmp, v_cmp_t, bias_c)
            o_t = _nsa_attn(q_t, kk, v_t, o_cmp, chosen, bias_t, gates)
            w_out = nsa_w_out_b
        else:
            q_t, k, v_t = _fox_proj(h3, g[2], fox_w_t, fox_bf, j)
            o_t = _fox_attn(q_t, k, v_t)
            w_out = fox_w_out_b
        h = _ffn(h, g[4], g[5], wg, wu, wd, i, 1, mixer=(o_t, w_out, j, g[3]))
    return h.reshape(b, s, d)
```

```python
import math

import numpy as np
import jax
import jax.numpy as jnp
from jax import lax
from jax.experimental import pallas as pl
from jax.experimental.pallas import tpu as pltpu

N_HEADS = 16
HEAD_DIM = 64
KV_GROUPS = 4
Q_PER_GROUP = N_HEADS // KV_GROUPS
CMP_BLOCK = 32
CMP_STRIDE = 16
SEL_BLOCK = 64
SEL_SHIFT = 6
SEL_TOPK = 16
WINDOW = 512
NUM_BUCKETS = 32
MAX_DISTANCE = 128
RMS_EPS = 1e-6
NEG_INF = -1e30
FORCED_SCORE = 1e9
Q_DIM = N_HEADS * HEAD_DIM
KV_DIM = KV_GROUPS * HEAD_DIM
N_GATES = 3 * N_HEADS
LOG2E = math.log2(math.e)
Q_SCALE = HEAD_DIM ** -0.5 * LOG2E

SUBLANES = 8
LANES = 128
ATT_TILE = 256
WIN_TILES = WINDOW // ATT_TILE
FAR_CHUNK = 512
GROUPS_PER_STEP = 2
FOX_TILE = 256
FOX_SLOTS = 2
AUG_DIM = HEAD_DIM + SUBLANES
FOX_QK_DIM = 2 * HEAD_DIM
VMEM_LIMIT = 56 * 1024 * 1024

BF16 = jnp.bfloat16
F32 = jnp.float32


def _cparams(*sem):
    return pltpu.CompilerParams(dimension_semantics=sem, vmem_limit_bytes=VMEM_LIMIT)


def _rms(x, g):
    return x * lax.rsqrt(jnp.mean(x * x, axis=-1, keepdims=True) + RMS_EPS) * g


def _dot(a, b):
    return jnp.dot(a, b, preferred_element_type=F32)


def _dot_nt(a, b):
    return lax.dot_general(a, b, (((1,), (1,)), ((), ())), preferred_element_type=F32)


def _dot_tn(a, b):
    return lax.dot_general(a, b, (((0,), (0,)), ((), ())), preferred_element_type=F32)


def _split3(x):
    hi = x.astype(BF16)
    r1 = x - hi.astype(F32)
    mid = r1.astype(BF16)
    lo = (r1 - mid.astype(F32)).astype(BF16)
    return hi, mid, lo


def _unit_rows(n_ones, width):
    row = lax.broadcasted_iota(jnp.int32, (SUBLANES, width), 0)
    return jnp.where(row < n_ones, 1.0, 0.0).astype(BF16)


def _fold(x, op):
    parts = x.reshape(x.shape[0] // SUBLANES, SUBLANES, x.shape[1])
    return op(parts, axis=0)


def _swiglu_halfstep(h, gpre_ref, gpost_ref, wg_ref, wu_ref, wd_ref):
    xn = _rms(h, gpre_ref[...]).astype(BF16)
    g = _dot(xn, wg_ref[...])
    u = _dot(xn, wu_ref[...])
    a = (g * jax.nn.sigmoid(g) * u).astype(BF16)
    return h + 0.5 * _rms(_dot(a, wd_ref[...]), gpost_ref[...])


def _ffn_body(h_ref, gpre_ref, gpost_ref, wg_ref, wu_ref, wd_ref, o_ref):
    o_ref[...] = _swiglu_halfstep(h_ref[...], gpre_ref, gpost_ref, wg_ref, wu_ref, wd_ref)


def _mix_ffn_body(h_ref, ot_ref, wo_ref, gmix_ref, gpre_ref, gpost_ref, wg_ref, wu_ref, wd_ref, o_ref):
    y = _dot_tn(ot_ref[0], wo_ref[...])
    h = h_ref[...] + _rms(y, gmix_ref[...])
    o_ref[...] = _swiglu_halfstep(h, gpre_ref, gpost_ref, wg_ref, wu_ref, wd_ref)


def _ffn(h, g_pre, g_post, wg, wu, wd, layer, half, mixer=None, tm=512):
    n, d = h.shape
    f = wg.shape[-1]
    once = dict(pipeline_mode=pl.Buffered(1))
    row = pl.BlockSpec((tm, d), lambda i: (i, 0))
    vec = pl.BlockSpec((1, d), lambda i: (0, 0))
    ffn_specs = [
        vec, vec,
        pl.BlockSpec((None, None, d, f), lambda i: (layer, half, 0, 0), **once),
        pl.BlockSpec((None, None, d, f), lambda i: (layer, half, 0, 0), **once),
        pl.BlockSpec((None, None, f, d), lambda i: (layer, half, 0, 0), **once),
    ]
    ffn_args = (g_pre, g_post, wg, wu, wd)
    if mixer is None:
        body, specs, args = _ffn_body, [row] + ffn_specs, (h,) + ffn_args
    else:
        o_t, w_out, mix_layer, g_mix = mixer
        kdim, s = o_t.shape[1:]
        per_seq = s // tm
        mix_specs = [
            pl.BlockSpec((1, kdim, tm), lambda i: (i // per_seq, 0, i % per_seq)),
            pl.BlockSpec((None, kdim, d), lambda i: (mix_layer, 0, 0), **once),
            vec,
        ]
        body, specs, args = _mix_ffn_body, [row] + mix_specs + ffn_specs, (h, o_t, w_out, g_mix) + ffn_args
    return pl.pallas_call(
        body,
        grid=(n // tm,),
        in_specs=specs,
        out_specs=row,
        out_shape=jax.ShapeDtypeStruct((n, d), F32),
        compiler_params=_cparams("parallel"),
        name="ffn" if mixer is None else "mix_ffn",
    )(*args)


def _nsa_proj_body(h_ref, g_ref, w_ref, wt_ref, qt_ref, kc_ref, k_ref, vt_ref, gate_ref):
    xn = _rms(h_ref[0], g_ref[...]).astype(BF16)
    res = _dot(xn, w_ref[...])
    for a in range(2 * KV_GROUPS):
        kc_ref[0, a] = res[:, a * HEAD_DIM:(a + 1) * HEAD_DIM].astype(BF16)
    for a in range(2 * KV_GROUPS):
        lo = KV_DIM * 2 + a * HEAD_DIM
        k_ref[0, a] = res[:, lo:lo + HEAD_DIM].astype(BF16)
    res_t = _dot_nt(wt_ref[...], xn)
    for hd in range(N_HEADS):
        qt_ref[0, hd] = (res_t[hd * HEAD_DIM:(hd + 1) * HEAD_DIM] * Q_SCALE).astype(BF16)
    extra = _unit_rows(1, res_t.shape[1])
    for a in range(2 * KV_GROUPS):
        lo = Q_DIM + a * HEAD_DIM
        vt_ref[0, a] = jnp.concatenate([res_t[lo:lo + HEAD_DIM].astype(BF16), extra], axis=0)
    gates = jax.nn.sigmoid(res_t[Q_DIM + 2 * KV_DIM:])
    width = 3 * Q_PER_GROUP
    for grp in range(KV_GROUPS):
        gate_ref[0, grp] = gates[grp * width:(grp + 1) * width]


def _nsa_proj(h3, g, w, w_t, layer, tm=1024):
    b, s, d = h3.shape
    return pl.pallas_call(
        _nsa_proj_body,
        grid=(b, s // tm),
        in_specs=[
            pl.BlockSpec((1, tm, d), lambda i, j: (i, j, 0)),
            pl.BlockSpec((1, d), lambda i, j: (0, 0)),
            pl.BlockSpec((None, d, w.shape[-1]), lambda i, j: (layer, 0, 0)),
            pl.BlockSpec((None, w_t.shape[1], d), lambda i, j: (layer, 0, 0)),
        ],
        out_specs=[
            pl.BlockSpec((1, N_HEADS, HEAD_DIM, tm), lambda i, j: (i, 0, 0, j)),
            pl.BlockSpec((1, 2 * KV_GROUPS, tm, HEAD_DIM), lambda i, j: (i, 0, j, 0)),
            pl.BlockSpec((1, 2 * KV_GROUPS, tm, HEAD_DIM), lambda i, j: (i, 0, j, 0)),
            pl.BlockSpec((1, 2 * KV_GROUPS, AUG_DIM, tm), lambda i, j: (i, 0, 0, j)),
            pl.BlockSpec((1, KV_GROUPS, 3 * Q_PER_GROUP, tm), lambda i, j: (i, 0, 0, j)),
        ],
        out_shape=[
            jax.ShapeDtypeStruct((b, N_HEADS, HEAD_DIM, s), BF16),
            jax.ShapeDtypeStruct((b, 2 * KV_GROUPS, s, HEAD_DIM), BF16),
            jax.ShapeDtypeStruct((b, 2 * KV_GROUPS, s, HEAD_DIM), BF16),
            jax.ShapeDtypeStruct((b, 2 * KV_GROUPS, AUG_DIM, s), BF16),
            jax.ShapeDtypeStruct((b, KV_GROUPS, 3 * Q_PER_GROUP, s), F32),
        ],
        compiler_params=_cparams("parallel", "parallel"),
        name="nsa_proj",
    )(h3, g, w, w_t)


def _compress_body(x_ref, pe_ref, w1_ref, b1_ref, w2_ref, w2t_ref, o_ref, ot_ref):
    n_chunk = x_ref.shape[3]
    half = CMP_STRIDE * HEAD_DIM
    x = x_ref[0, 0].reshape(KV_GROUPS * n_chunk, half)
    top = _dot(x, w1_ref[:half, :])
    bot = _dot(x, w1_ref[half:, :])
    bot_next = pltpu.roll(bot, KV_GROUPS * n_chunk - 1, 0)
    pe = jnp.broadcast_to(pe_ref[...].astype(BF16), (SUBLANES, 2 * half))
    const = _dot(pe, w1_ref[...])[0:1] + b1_ref[...]
    hid = jax.nn.gelu(top + bot_next + const).astype(BF16)
    out = _dot(hid, w2_ref[...])
    row = lax.broadcasted_iota(jnp.int32, out.shape, 0) & (n_chunk - 1)
    o_ref[0, 0] = jnp.where(row < n_chunk - 1, out, 0.0).reshape(KV_GROUPS, n_chunk, HEAD_DIM).astype(BF16)
    out_t = _dot_nt(w2t_ref[...], hid)
    col = lax.broadcasted_iota(jnp.int32, out_t.shape, 1) & (n_chunk - 1)
    out_t = jnp.where(col < n_chunk - 1, out_t, 0.0).astype(BF16)
    for grp in range(KV_GROUPS):
        ot_ref[0, 0, grp] = out_t[:, grp * n_chunk:(grp + 1) * n_chunk]


def _compress(kcv, pe, w1, b1, w2, w2t, layer):
    b, _, s, dh = kcv.shape
    n_chunk = s // CMP_STRIDE
    x = kcv.reshape(b, 2, KV_GROUPS, n_chunk, CMP_STRIDE * dh)
    hidden = w1.shape[-1]
    return pl.pallas_call(
        _compress_body,
        grid=(b, 2),
        in_specs=[
            pl.BlockSpec((1, 1, KV_GROUPS, n_chunk, CMP_STRIDE * dh), lambda i, a: (i, a, 0, 0, 0)),
            pl.BlockSpec((None, None, 1, CMP_BLOCK * dh), lambda i, a: (layer, a, 0, 0)),
            pl.BlockSpec((None, None, CMP_BLOCK * dh, hidden), lambda i, a: (layer, a, 0, 0)),
            pl.BlockSpec((None, None, 1, hidden), lambda i, a: (layer, a, 0, 0)),
            pl.BlockSpec((None, None, hidden, dh), lambda i, a: (layer, a, 0, 0)),
            pl.BlockSpec((None, None, dh, hidden), lambda i, a: (layer, a, 0, 0)),
        ],
        out_specs=[
            pl.BlockSpec((1, 1, KV_GROUPS, n_chunk, dh), lambda i, a: (i, a, 0, 0, 0)),
            pl.BlockSpec((1, 1, KV_GROUPS, dh, n_chunk), lambda i, a: (i, a, 0, 0, 0)),
        ],
        out_shape=[
            jax.ShapeDtypeStruct((b, 2, KV_GROUPS, n_chunk, dh), BF16),
            jax.ShapeDtypeStruct((b, 2, KV_GROUPS, dh, n_chunk), BF16),
        ],
        compiler_params=_cparams("parallel", "parallel"),
        name="nsa_compress",
    )(x, pe, w1, b1, w2, w2t)


def _t5_bucket_np(rel):
    n = np.maximum(rel, 0)
    max_exact = NUM_BUCKETS // 2
    nf = np.maximum(n, 1).astype(np.float32)
    ratio = np.log(nf / np.float32(max_exact)) / np.float32(math.log(MAX_DISTANCE / max_exact))
    large = max_exact + (ratio * np.float32(NUM_BUCKETS - max_exact)).astype(np.int32)
    large = np.minimum(large, NUM_BUCKETS - 1)
    return np.where(n < max_exact, n, large).astype(np.int32)


def _bucket_maps(s):
    n_chunk = s // CMP_STRIDE
    t = np.arange(s)[None, :]
    blk_end = np.arange(n_chunk)[:, None] * CMP_STRIDE + CMP_BLOCK - 1
    rel_c = t - blk_end
    map_c = np.where(rel_c >= 0, _t5_bucket_np(rel_c), -1).astype(np.int32)
    j = np.arange(ATT_TILE)[:, None]
    i = np.arange(ATT_TILE)[None, :]
    diag = np.where(i - j >= 0, _t5_bucket_np(i - j), -1)
    sub = _t5_bucket_np(ATT_TILE + i - j)
    edge = np.where(j > i, _t5_bucket_np(WINDOW + i - j), -1)
    map_t = np.stack([diag, sub, edge]).astype(np.int32)
    assert _t5_bucket_np(np.arange(ATT_TILE + 1, s + WINDOW)).min() == _FAR_BUCKET
    return map_c, map_t


_FAR_BUCKET = NUM_BUCKETS - 1
TILE_DIAG, TILE_SUB, TILE_EDGE = 0, 1, 2


def _bias_body(rb_ref, mc_ref, mt_ref, bc_ref, bt_ref):
    hd = pl.program_id(0)

    def lookup(bucket, shift):
        level = [(rb_ref[bk, hd] - shift) * LOG2E for bk in range(NUM_BUCKETS)]
        bit = 1
        while len(level) > 1:
            odd = (bucket & bit) != 0
            level = [jnp.where(odd, level[k + 1], level[k]) for k in range(0, len(level), 2)]
            bit *= 2
        return jnp.where(bucket < 0, NEG_INF, level[0])

    n_chunk, s = mc_ref.shape
    far = rb_ref[_FAR_BUCKET, hd] * LOG2E
    for c0 in range(0, n_chunk, SUBLANES):
        first = c0 * CMP_STRIDE + CMP_BLOCK - 1
        last = (c0 + SUBLANES - 1) * CMP_STRIDE + CMP_BLOCK - 1 + MAX_DISTANCE
        lo = min(first // LANES * LANES, s)
        hi = min(-(-last // LANES) * LANES, s)
        rows = slice(c0, c0 + SUBLANES)
        if lo > 0:
            bc_ref[0, rows, :lo] = jnp.full((SUBLANES, lo), NEG_INF, F32)
        if hi > lo:
            bc_ref[0, rows, lo:hi] = lookup(mc_ref[rows, lo:hi], 0.0)
        if hi < s:
            bc_ref[0, rows, hi:] = jnp.full((SUBLANES, s - hi), far, F32)
    for d in range(mt_ref.shape[0]):
        bt_ref[d, 0] = lookup(mt_ref[d], rb_ref[_FAR_BUCKET, hd])


def _bias_tables(rel_bias, s):
    map_c, map_t = _bucket_maps(s)
    n_chunk = map_c.shape[0]
    n_tab = map_t.shape[0]
    T = ATT_TILE
    R = Q_PER_GROUP
    return pl.pallas_call(
        _bias_body,
        grid=(N_HEADS,),
        in_specs=[
            pl.BlockSpec(memory_space=pltpu.SMEM),
            pl.BlockSpec((n_chunk, s), lambda i: (0, 0)),
            pl.BlockSpec((n_tab, T, T), lambda i: (0, 0, 0)),
        ],
        out_specs=[
            pl.BlockSpec((1, n_chunk, s), lambda i: (i, 0, 0)),
            pl.BlockSpec((n_tab, 1, T, T), lambda i: (0, i // R, 0, i % R)),
        ],
        out_shape=[
            jax.ShapeDtypeStruct((N_HEADS, n_chunk, s), F32),
            jax.ShapeDtypeStruct((n_tab, KV_GROUPS, T, R * T), F32),
        ],
        compiler_params=_cparams("parallel"),
        name="t5_bias_tables",
    )(rel_bias, jnp.asarray(map_c), jnp.asarray(map_t))


def _nsa_select_body(q_ref, kc_ref, vct_ref, bc_ref, ovl_ref, oc_ref, ch_ref):
    T = ATT_TILE
    R = Q_PER_GROUP
    n_sel = ovl_ref.shape[0]
    ovl = ovl_ref[...]
    j_blk = lax.broadcasted_iota(jnp.int32, (n_sel, T), 0)
    j_slab = lax.broadcasted_iota(jnp.int32, (SUBLANES, T), 0)
    for u in range(q_ref.shape[3] // T):
        qi = pl.program_id(1) * (q_ref.shape[3] // T) + u
        cols = slice(u * T, (u + 1) * T)
        q_t = jnp.concatenate([q_ref[0, r, :, cols] for r in range(R)], axis=1)
        bias = jnp.concatenate([bc_ref[r, :, cols] for r in range(R)], axis=1)
        s_c = _dot(kc_ref[0, 0, 0], q_t) + bias
        m_c = jnp.max(s_c, axis=0, keepdims=True)
        p_c = jnp.exp2(s_c - m_c)
        p_c = p_c * (1.0 / jnp.sum(p_c, axis=0, keepdims=True))
        t_col = qi * T + (lax.broadcasted_iota(jnp.int32, (1, R * T), 1) & (T - 1))
        p_c = jnp.where(t_col >= CMP_BLOCK - 1, p_c, 0.0)
        o_c = _dot(vct_ref[0, 0, 0], p_c.astype(BF16))
        for r in range(R):
            oc_ref[0, r, :, cols] = o_c[:, r * T:(r + 1) * T]

        p_sum = p_c[:, :T]
        for r in range(1, R):
            p_sum = p_sum + p_c[:, r * T:(r + 1) * T]
        hi, mid, lo = _split3(p_sum)
        imp = _dot(ovl, hi) + _dot(ovl, mid) + _dot(ovl, lo)
        cur = (qi * T + lax.broadcasted_iota(jnp.int32, (n_sel, T), 1)) >> SEL_SHIFT
        forced = (j_blk == 0) | (j_blk == cur) | (j_blk == cur - 1)
        imp = jnp.where(forced, FORCED_SCORE, jnp.where(j_blk <= cur, imp, NEG_INF))
        slabs = [imp[lo:lo + SUBLANES] for lo in range(0, n_sel, SUBLANES)]
        ranks = [jnp.zeros((SUBLANES, T), F32) for _ in slabs]
        for i in range(n_sel):
            row = imp[i:i + 1, :]
            for k, slab in enumerate(slabs):
                lo = k * SUBLANES
                if lo > i:
                    hit = jnp.where(row >= slab, 1.0, 0.0)
                elif lo + SUBLANES - 1 <= i:
                    hit = jnp.where(row > slab, 1.0, 0.0)
                else:
                    hit = jnp.where(j_slab > i - lo, jnp.where(row >= slab, 1.0, 0.0),
                                    jnp.where(row > slab, 1.0, 0.0))
                ranks[k] = ranks[k] + hit
        rank = jnp.concatenate(ranks, axis=0)
        ch_ref[0, 0, :, cols] = jnp.where(rank < min(SEL_TOPK, n_sel), 1.0, 0.0)


def _nsa_select(q_t, k_cmp, v_cmp_t, bias_c, tiles_per_step=8):
    b, _, dh, s = q_t.shape
    R = Q_PER_GROUP
    n_chunk = s // CMP_STRIDE
    n_sel = s // SEL_BLOCK
    tq = min(tiles_per_step * ATT_TILE, s)
    c_start = np.arange(n_chunk)[None, :] * CMP_STRIDE
    j = np.arange(n_sel)[:, None]
    overlap = (c_start < (j + 1) * SEL_BLOCK) & (c_start + CMP_BLOCK > j * SEL_BLOCK)
    overlap[:, n_chunk - 1] = False
    return pl.pallas_call(
        _nsa_select_body,
        grid=(KV_GROUPS, s // tq, b),
        in_specs=[
            pl.BlockSpec((1, R, dh, tq), lambda g, t, i: (i, g, 0, t)),
            pl.BlockSpec((1, 1, 1, n_chunk, dh), lambda g, t, i: (i, 0, g, 0, 0)),
            pl.BlockSpec((1, 1, 1, dh, n_chunk), lambda g, t, i: (i, 1, g, 0, 0)),
            pl.BlockSpec((R, n_chunk, tq), lambda g, t, i: (g, 0, t)),
            pl.BlockSpec((n_sel, n_chunk), lambda g, t, i: (0, 0)),
        ],
        out_specs=[
            pl.BlockSpec((1, R, dh, tq), lambda g, t, i: (i, g, 0, t)),
            pl.BlockSpec((1, 1, n_sel, tq), lambda g, t, i: (i, g, 0, t)),
        ],
        out_shape=[
            jax.ShapeDtypeStruct((b, N_HEADS, dh, s), F32),
            jax.ShapeDtypeStruct((b, KV_GROUPS, n_sel, s), F32),
        ],
        compiler_params=_cparams("parallel", "parallel", "parallel"),
        name="nsa_select",
    )(q_t, k_cmp, v_cmp_t, bias_c, jnp.asarray(overlap, BF16))


def _nsa_attn_body(q_ref, ks_ref, kw_ref, vst_ref, vwt_ref, bt_ref, gate_ref, oc_ref, ch_ref,
                   o_ref, rows_ref, s_ref, near_ref, win_ref, peak_ref, acc_ref):
    T = ATT_TILE
    R = Q_PER_GROUP
    C = FAR_CHUNK
    N = R * T
    groups = range(ks_ref.shape[1])
    qi = pl.program_id(2)
    n_far = jnp.maximum(qi - 1, 0)
    n_chunks = (n_far + C // T - 1) // (C // T)
    qs = [jnp.concatenate([q_ref[0, g * R + r] for r in range(R)], axis=1) for g in groups]

    def add_block_rows(g, which, s, off):
        first = off // SEL_BLOCK
        pieces = [s[b * SEL_BLOCK:(b + 1) * SEL_BLOCK] + rows_ref[g, which, pl.ds(first + b, 1), :]
                  for b in range(s.shape[0] // SEL_BLOCK)]
        return jnp.concatenate(pieces, axis=0)

    def near_logits(g, k_ref, d, table, masked):
        off = pl.multiple_of(jnp.maximum(qi - d, 0) * T, T)
        s = _dot(k_ref[0, g, pl.ds(off, T), :], qs[g])
        if table is not None:
            s = s + bt_ref[table, g]
        if masked:
            s = add_block_rows(g, 0, s, off)
        if d > 0:
            s = jnp.where(qi >= d, s, NEG_INF)
        return s, off

    def logits_pass(g, k_ref, tiles, masked, out_ref):
        offs = []
        peak = None
        for idx, (d, table) in enumerate(tiles):
            s, off = near_logits(g, k_ref, d, table, masked)
            out_ref[g, idx * T:(idx + 1) * T, :] = s
            part = _fold(s, jnp.max)
            peak = part if peak is None else jnp.maximum(peak, part)
            offs.append(off)
        return offs, peak

    def values_pass(g, logit_ref, vt_ref, offs, m):
        acc = None
        for idx, off in enumerate(offs):
            p = jnp.exp2((logit_ref[g, idx * T:(idx + 1) * T, :] - m).astype(BF16))
            pv = _dot(vt_ref[0, g, :, pl.ds(off, T)], p)
            acc = pv if acc is None else acc + pv
        return acc

    win_tiles = [(0, TILE_DIAG), (1, TILE_SUB)]
    win_tiles += [(d, None) for d in range(2, WIN_TILES)] + [(WIN_TILES, TILE_EDGE)]
    near_sel = [(1, TILE_SUB), (0, TILE_DIAG)]
    win_offs, m_win, sel_offs = [], [], []
    blk = lax.broadcasted_iota(jnp.int32, (ch_ref.shape[2], N), 0)
    for g in groups:
        add = (ch_ref[0, g] - 1.0) * -NEG_INF
        add = jnp.concatenate([add] * R, axis=1)
        rows_ref[g, 0] = add
        rows_ref[g, 1] = jnp.where(blk < n_far * (T // SEL_BLOCK), add, NEG_INF)
        offs, peak = logits_pass(g, kw_ref, win_tiles, False, win_ref)
        win_offs.append(offs)
        m_win.append(jnp.max(peak, axis=0, keepdims=True))
        offs, peak = logits_pass(g, ks_ref, near_sel, True, near_ref)
        sel_offs.append(offs)
        peak_ref[g] = peak

    def far_logits(c, carry):
        off = pl.multiple_of(c * C, C)
        for g in groups:
            s = _dot(ks_ref[0, g, pl.ds(off, C), :], qs[g])
            s = add_block_rows(g, 1, s, off)
            s_ref[g, pl.ds(off, C), :] = s
            peak_ref[g] = jnp.maximum(peak_ref[g], _fold(s, jnp.max))
        return carry

    lax.fori_loop(0, n_chunks, far_logits, 0)

    m_sel = []
    for g in groups:
        m_sel.append(jnp.max(peak_ref[g], axis=0, keepdims=True))
        acc_ref[g] = values_pass(g, near_ref, vst_ref, sel_offs[g], m_sel[g])

    def far_values(c, carry):
        off = pl.multiple_of(c * C, C)
        for g in groups:
            p = jnp.exp2((s_ref[g, pl.ds(off, C), :] - m_sel[g]).astype(BF16))
            acc_ref[g] += _dot(vst_ref[0, g, :, pl.ds(off, C)], p)
        return carry

    lax.fori_loop(0, n_chunks, far_values, 0)

    for g in groups:
        gate = gate_ref[0, g]
        gate_of = lambda br: jnp.concatenate(
            [gate[br * R + r:br * R + r + 1, :] for r in range(R)], axis=1)
        win = values_pass(g, win_ref, vwt_ref, win_offs[g], m_win[g])
        sel = acc_ref[g]
        w_win = gate_of(2) * (1.0 / win[HEAD_DIM:HEAD_DIM + 1])
        w_sel = gate_of(1) * (1.0 / sel[HEAD_DIM:HEAD_DIM + 1])
        o_c = jnp.concatenate([oc_ref[0, g * R + r] for r in range(R)], axis=1)
        o = gate_of(0) * o_c + w_sel * sel[:HEAD_DIM] + w_win * win[:HEAD_DIM]
        for r in range(R):
            hd = g * R + r
            o_ref[0, hd * HEAD_DIM:(hd + 1) * HEAD_DIM, :] = o[:, r * T:(r + 1) * T].astype(BF16)


def _nsa_attn(q_t, kk, v_t, o_cmp, chosen, bias_t, gates):
    b, _, dh, s = q_t.shape
    T = ATT_TILE
    R = Q_PER_GROUP
    N = R * T
    gp = GROUPS_PER_STEP
    per = KV_GROUPS // gp
    n_sel = s // SEL_BLOCK
    far_keys = max(s - 2 * T, FAR_CHUNK)
    once = dict(pipeline_mode=pl.Buffered(1))
    k_spec = lambda a: pl.BlockSpec((1, gp, s, dh), lambda g, i, t: (i, a * per + g, 0, 0))
    vt_spec = lambda a: pl.BlockSpec((1, gp, AUG_DIM, s), lambda g, i, t: (i, a * per + g, 0, 0))
    return pl.pallas_call(
        _nsa_attn_body,
        grid=(per, b, s // T),
        in_specs=[
            pl.BlockSpec((1, gp * R, dh, T), lambda g, i, t: (i, g, 0, t)),
            k_spec(0), k_spec(1), vt_spec(0), vt_spec(1),
            pl.BlockSpec((bias_t.shape[0], gp, T, N), lambda g, i, t: (0, g, 0, 0), **once),
            pl.BlockSpec((1, gp, 3 * R, T), lambda g, i, t: (i, g, 0, t)),
            pl.BlockSpec((1, gp * R, dh, T), lambda g, i, t: (i, g, 0, t)),
            pl.BlockSpec((1, gp, n_sel, T), lambda g, i, t: (i, g, 0, t)),
        ],
        out_specs=pl.BlockSpec((1, gp * R * dh, T), lambda g, i, t: (i, g, t)),
        out_shape=jax.ShapeDtypeStruct((b, Q_DIM, s), BF16),
        scratch_shapes=[
            pltpu.VMEM((gp, 2, n_sel, N), F32),
            pltpu.VMEM((gp, far_keys, N), F32),
            pltpu.VMEM((gp, 2 * T, N), F32),
            pltpu.VMEM((gp, (WIN_TILES + 1) * T, N), F32),
            pltpu.VMEM((gp, SUBLANES, N), F32),
            pltpu.VMEM((gp, AUG_DIM, N), F32),
        ],
        compiler_params=_cparams("parallel", "parallel", "arbitrary"),
        name="nsa_attn",
    )(q_t, kk, kk, v_t, v_t, bias_t, gates, o_cmp, chosen)


def _fox_proj_body(h_ref, g_ref, wt_ref, bf_ref, q_ref, k_ref, vt_ref, carry_ref):
    tm = h_ref.shape[1]

    @pl.when(pl.program_id(1) == 0)
    def _():
        carry_ref[...] = jnp.zeros_like(carry_ref)

    xn = _rms(h_ref[0], g_ref[...]).astype(BF16)
    res_t = _dot_nt(wt_ref[...], xn)
    log_f = jax.nn.log_sigmoid(res_t[3 * Q_DIM:] + bf_ref[...])
    upper = jnp.where(lax.broadcasted_iota(jnp.int32, (tm, tm), 0)
                      <= lax.broadcasted_iota(jnp.int32, (tm, tm), 1), 1.0, 0.0).astype(BF16)
    hi, mid, lo = _split3(log_f)
    cum = _dot(hi, upper) + _dot(mid, upper) + _dot(lo, upper) + carry_ref[...]
    carry_ref[...] = cum[:, tm - 1:tm]
    terms = [t.astype(F32) for t in _split3(-LOG2E * cum)]
    pad = FOX_QK_DIM - HEAD_DIM
    k_zeros = jnp.zeros((pad - len(terms), tm), F32)
    q_extra = jnp.concatenate([_unit_rows(len(terms), tm), jnp.zeros((pad - SUBLANES, tm), BF16)], axis=0)
    v_extra = _unit_rows(1, tm)
    for hd in range(N_HEADS):
        rows = slice(hd * HEAD_DIM, (hd + 1) * HEAD_DIM)
        q_h = (res_t[rows] * Q_SCALE).astype(BF16)
        k_h = res_t[Q_DIM + hd * HEAD_DIM:Q_DIM + (hd + 1) * HEAD_DIM]
        v_h = res_t[2 * Q_DIM + hd * HEAD_DIM:2 * Q_DIM + (hd + 1) * HEAD_DIM].astype(BF16)
        k_t = jnp.concatenate([k_h] + [t[hd:hd + 1] for t in terms] + [k_zeros], axis=0)
        q_ref[0, hd] = jnp.concatenate([q_h, q_extra], axis=0)
        k_ref[0, hd] = k_t.T.astype(BF16)
        vt_ref[0, hd] = jnp.concatenate([v_h, v_extra], axis=0)


def _fox_proj(h3, g, w_t, b_f, layer, tm=1024):
    b, s, d = h3.shape
    t_spec = lambda rows: pl.BlockSpec((1, N_HEADS, rows, tm), lambda i, j: (i, 0, 0, j))
    t_shape = lambda rows: jax.ShapeDtypeStruct((b, N_HEADS, rows, s), BF16)
    return pl.pallas_call(
        _fox_proj_body,
        grid=(b, s // tm),
        in_specs=[
            pl.BlockSpec((1, tm, d), lambda i, j: (i, j, 0)),
            pl.BlockSpec((1, d), lambda i, j: (0, 0)),
            pl.BlockSpec((None, w_t.shape[1], d), lambda i, j: (layer, 0, 0)),
            pl.BlockSpec((None, N_HEADS, 1), lambda i, j: (layer, 0, 0)),
        ],
        out_specs=[t_spec(FOX_QK_DIM),
                   pl.BlockSpec((1, N_HEADS, tm, FOX_QK_DIM), lambda i, j: (i, 0, j, 0)),
                   t_spec(AUG_DIM)],
        out_shape=[t_shape(FOX_QK_DIM),
                   jax.ShapeDtypeStruct((b, N_HEADS, s, FOX_QK_DIM), BF16),
                   t_shape(AUG_DIM)],
        scratch_shapes=[pltpu.VMEM((N_HEADS, 1), F32)],
        compiler_params=_cparams("parallel", "arbitrary"),
        name="fox_proj",
    )(h3, g, w_t, b_f)


def _fox_attn_body(q_ref, k_ref, vt_ref, o_ref, *scratch):
    T = FOX_TILE
    n_q = q_ref.shape[3] // T
    n_heads = q_ref.shape[1]
    s_refs, p_refs = scratch[:len(scratch) // 2], scratch[len(scratch) // 2:]
    causal = (lax.broadcasted_iota(jnp.int32, (T, T), 0)
              <= lax.broadcasted_iota(jnp.int32, (T, T), 1))
    for qi in range(n_q):
        cols = slice(qi * T, (qi + 1) * T)
        for hh in range(n_heads):
            s_ref = s_refs[(qi % FOX_SLOTS) * n_heads + hh]
            p_ref = p_refs[(qi % FOX_SLOTS) * n_heads + hh]
            q_t = q_ref[0, hh, :, cols]
            peak = None
            for kt in range(qi + 1):
                keys = slice(kt * T, (kt + 1) * T)
                s = _dot(k_ref[0, hh, keys, :], q_t)
                if kt == qi:
                    s = jnp.where(causal, s, NEG_INF)
                s_ref[keys, :] = s
                part = _fold(s, jnp.max)
                peak = part if peak is None else jnp.maximum(peak, part)
            m = jnp.max(peak, axis=0, keepdims=True)
            for kt in range(qi + 1):
                keys = slice(kt * T, (kt + 1) * T)
                p_ref[keys, :] = jnp.exp2((s_ref[keys, :] - m).astype(BF16))
            extent = (qi + 1) * T
            acc = _dot(vt_ref[0, hh, :, :extent], p_ref[:extent, :])
            o = acc[:HEAD_DIM] * (1.0 / acc[HEAD_DIM:HEAD_DIM + 1])
            o_ref[0, hh * HEAD_DIM:(hh + 1) * HEAD_DIM, cols] = o.astype(BF16)


def _fox_attn(q_t, k, v_t, heads_per_step=2):
    b, nh, kdim, s = q_t.shape
    dh = HEAD_DIM
    T = FOX_TILE
    hp = heads_per_step
    t_spec = lambda rows: pl.BlockSpec((1, hp, rows, s), lambda i, h: (i, h, 0, 0))
    return pl.pallas_call(
        _fox_attn_body,
        grid=(b, nh // hp),
        in_specs=[t_spec(kdim), pl.BlockSpec((1, hp, s, kdim), lambda i, h: (i, h, 0, 0)), t_spec(AUG_DIM)],
        out_specs=pl.BlockSpec((1, hp * dh, s), lambda i, h: (i, h, 0)),
        out_shape=jax.ShapeDtypeStruct((b, nh * dh, s), BF16),
        scratch_shapes=([pltpu.VMEM((s, T), F32)] * (FOX_SLOTS * hp)
                        + [pltpu.VMEM((s, T), BF16)] * (FOX_SLOTS * hp)),
        compiler_params=_cparams("parallel", "parallel"),
        name="fox_attn",
    )(q_t, k, v_t)


def kernel(x, norm_g, ffn_w_gate, ffn_w_up, ffn_w_down, rel_bias, nsa_w_in, nsa_cmp_pe, nsa_cmp_w1,
           nsa_cmp_b1, nsa_cmp_w2, nsa_w_out, fox_w_in, fox_b_f, fox_w_out):
    b, s, d = x.shape
    depth = norm_g.shape[0]
    n = b * s
    t_last = lambda w: jnp.swapaxes(w, -1, -2)
    wg, wu, wd = ffn_w_gate.astype(BF16), ffn_w_up.astype(BF16), ffn_w_down.astype(BF16)
    c0 = Q_DIM
    col = lambda a: nsa_w_in[:, :, c0 + a * KV_DIM:c0 + (a + 1) * KV_DIM]
    gate_cols = np.arange(N_GATES).reshape(3, KV_GROUPS, Q_PER_GROUP).transpose(1, 0, 2).reshape(-1)
    w_gates = nsa_w_in[:, :, c0 + 6 * KV_DIM:][:, :, gate_cols]
    nsa_w_k = jnp.concatenate([col(0), col(1), col(2), col(4)], axis=-1).astype(BF16)
    nsa_w_t = t_last(jnp.concatenate([nsa_w_in[:, :, :c0], col(3), col(5), w_gates], axis=-1)).astype(BF16)
    nsa_w_out_b = nsa_w_out.astype(BF16)
    fox_w_t = t_last(fox_w_in).astype(BF16)
    fox_w_out_b = fox_w_out.astype(BF16)
    cmp_w1_b, cmp_w2_b = nsa_cmp_w1.astype(BF16), nsa_cmp_w2.astype(BF16)
    cmp_w2t_b = t_last(nsa_cmp_w2).astype(BF16)
    cmp_pe = nsa_cmp_pe.reshape(nsa_cmp_pe.shape[0], 2, 1, CMP_BLOCK * HEAD_DIM)
    cmp_b1 = nsa_cmp_b1[:, :, None, :]
    fox_bf = fox_b_f[:, :, None]
    gains = norm_g[:, :, None, :]

    bias_c, bias_t = _bias_tables(rel_bias, s)

    h = x.reshape(n, d)
    for i in range(depth):
        g = gains[i]
        j = i // 2
        h = _ffn(h, g[0], g[1], wg, wu, wd, i, 0)
        h3 = h.reshape(b, s, d)
        if i % 2 == 0:
            q_t, kcv, kk, v_t, gates = _nsa_proj(h3, g[2], nsa_w_k, nsa_w_t, j)
            k_cmp, v_cmp_t = _compress(kcv, cmp_pe, cmp_w1_b, cmp_b1, cmp_w2_b, cmp_w2t_b, j)
            o_cmp, chosen = _nsa_select(q_t, k_cmp, v_cmp_t, bias_c)
            o_t = _nsa_attn(q_t, kk, v_t, o_cmp, chosen, bias_t, gates)
            w_out = nsa_w_out_b
        else:
            q_t, k, v_t = _fox_proj(h3, g[2], fox_w_t, fox_bf, j)
            o_t = _fox_attn(q_t, k, v_t)
            w_out = fox_w_out_b
        h = _ffn(h, g[4], g[5], wg, wu, wd, i, 1, mixer=(o_t, w_out, j, g[3]))
    return h.reshape(b, s, d)
```
